```python
import math
import jax
import jax.numpy as jnp
from jax import lax
import numpy as np

D_MODEL = 1024
BATCH = 4
SEQ = 4096
DEPTH = 4
DEC_BATCH = 128
DEC_SEQ = 8
PAST_LEN = 8192
PAGE_SIZE = 128

F32 = jnp.float32
N_EVEN = (DEPTH + 1) // 2
N_ODD = DEPTH // 2
PLE_DIM = 256
NORM_EPS = 1e-6
MASK_NEG = -1e30
F_FLOOR = 1e-30
DEEPNORM_ALPHA = (2 * DEPTH) ** 0.25
DEEPNORM_BETA = (8 * DEPTH) ** -0.25

A_HEADS = 4
A_DK = 128
A_DV = 128
A_CONV = 4
A_CHUNK = 64
A_QK = A_HEADS * A_DK
A_VW = A_HEADS * A_DV
A_CONV_CH = 2 * A_QK + A_VW

B_HEADS = 8
B_KV_HEADS = 2
B_HD = 64
B_GROUP = B_HEADS // B_KV_HEADS
B_QW = B_HEADS * B_HD
B_KVW = B_KV_HEADS * B_HD
WINDOW = 128
ROT_DIM = B_HD // 4
ROPE_THETA = 500000.0

OFF_A_GATE = A_CONV_CH
OFF_A_DECAY = OFF_A_GATE + A_VW
OFF_A_BETA = OFF_A_DECAY + A_HEADS
OFF_B_Q = OFF_A_BETA + A_HEADS
OFF_B_K = OFF_B_Q + B_QW
OFF_B_V = OFF_B_K + B_KVW
OFF_B_GATE = OFF_B_V + B_KVW
EVEN_IN = OFF_B_GATE + B_QW
EVEN_SPLITS = (OFF_A_GATE, OFF_A_DECAY, OFF_A_BETA, OFF_B_Q, OFF_B_K, OFF_B_V, OFF_B_GATE)

C_HEADS = 8
C_DK = D_MODEL // C_HEADS
C_DV = D_MODEL // C_HEADS
C_CHUNK = 32
ODD_IN = 4 * D_MODEL

kernel_name = 'hybrid_deltanet_swa_hgrn2_step'


def layer_norm(x, g, b):
    xf = x.astype(F32)
    mu = jnp.mean(xf, -1, keepdims=True)
    xc = xf - mu
    var = jnp.mean(xc * xc, -1, keepdims=True)
    return (xc * lax.rsqrt(var + NORM_EPS) * g.astype(F32) + b.astype(F32)).astype(x.dtype)


def rms_norm(x, g):
    xf = x.astype(F32)
    return (xf * lax.rsqrt(jnp.mean(xf * xf, -1, keepdims=True) + NORM_EPS) * g.astype(F32)).astype(x.dtype)


def l2_normalize(t):
    tf = t.astype(F32)
    return (tf * lax.rsqrt(jnp.sum(tf * tf, -1, keepdims=True) + NORM_EPS)).astype(t.dtype)


def masked_exp(mask, d):
    return jnp.where(mask, jnp.exp(jnp.where(mask, d, 0.0)), 0.0)


def causal_conv(x, buf, w):
    l = x.shape[1]
    xp = jnp.concatenate([buf.astype(x.dtype), x], axis=1)
    y = sum(xp[:, j:j + l] * w[j] for j in range(A_CONV))
    return jax.nn.silu(y), xp[:, xp.shape[1] - (A_CONV - 1):]


def rotary(x, pos):
    inv = ROPE_THETA ** (-jnp.arange(0, ROT_DIM, 2, dtype=F32) / ROT_DIM)
    ang = pos.astype(F32)[:, None] * inv[None, :]
    cos = jnp.cos(ang)[None, :, None, :]
    sin = jnp.sin(ang)[None, :, None, :]
    xf = x.astype(F32)
    x1 = xf[..., :ROT_DIM // 2]
    x2 = xf[..., ROT_DIM // 2:ROT_DIM]
    rot = jnp.concatenate([x1 * cos - x2 * sin, x2 * cos + x1 * sin], -1).astype(x.dtype)
    return jnp.concatenate([rot, x[..., ROT_DIM:]], -1)


def pad_chunks(t, c, nc, l):
    t = jnp.pad(t, [(0, 0), (0, nc * c - l)] + [(0, 0)] * (t.ndim - 2))
    t = t.reshape((t.shape[0], nc, c) + t.shape[2:])
    return jnp.moveaxis(t, 3, 1)


def unchunk(o, l):
    nc, n, h, c, dv = o.shape
    return o.transpose(1, 0, 3, 2, 4).reshape(n, nc * c, h, dv)[:, :l]


def gated_delta_rule(q, k, v, g, beta, s0):
    n, l, h, dk = q.shape
    out_dtype = v.dtype
    c = min(A_CHUNK, l)
    nc = -(-l // c)
    q, k, v, g, beta = (pad_chunks(t.astype(F32), c, nc, l) for t in (q, k, v, g, beta))
    q = q * dk ** -0.5
    G = jnp.cumsum(g, axis=-1)
    idx = jnp.arange(c)
    causal = idx[:, None] >= idx[None, :]
    strict = idx[:, None] > idx[None, :]
    decay = masked_exp(causal, G[..., :, None] - G[..., None, :])
    kb = k * beta[..., None]
    a_mat = jnp.where(strict, jnp.einsum('nhcid,nhcjd->nhcij', kb, k) * decay, 0.0)
    eye = jnp.eye(c, dtype=F32)
    t_inv = lax.linalg.triangular_solve(a_mat + eye, jnp.broadcast_to(eye, a_mat.shape), left_side=True, lower=True)
    eG = jnp.exp(G)
    u = jnp.einsum('nhcij,nhcje->nhcie', t_inv, v * beta[..., None])
    w = jnp.einsum('nhcij,nhcjd->nhcid', t_inv, kb * eG[..., None])
    qk = jnp.einsum('nhcid,nhcjd->nhcij', q, k) * decay
    q_dec = q * eG[..., None]
    k_dec = k * jnp.exp(G[..., -1:] - G)[..., None]
    g_last = eG[..., -1]

    def step(s, xs):
        qk_c, q_c, k_c, u_c, w_c, gl = xs
        v_new = u_c - jnp.einsum('nhid,nhde->nhie', w_c, s)
        o = jnp.einsum('nhid,nhde->nhie', q_c, s) + jnp.einsum('nhij,nhje->nhie', qk_c, v_new)
        s = s * gl[..., None, None] + jnp.einsum('nhid,nhie->nhde', k_c, v_new)
        return s, o

    xs = tuple(jnp.moveaxis(t, 2, 0) for t in (qk, q_dec, k_dec, u, w, g_last))
    s, o = lax.scan(step, s0.astype(F32), xs)
    return unchunk(o, l).astype(out_dtype), s.astype(s0.dtype)


def hgrn2_recurrence(q, k, v, logf, s0):
    n, l, h, dk = q.shape
    out_dtype = v.dtype
    c = min(C_CHUNK, l)
    nc = -(-l // c)
    q, k, v, logf = (pad_chunks(t.astype(F32), c, nc, l) for t in (q, k, v, logf))
    q = q * dk ** -0.5
    G = jnp.cumsum(logf, axis=-2)
    idx = jnp.arange(c)
    causal = (idx[:, None] >= idx[None, :])[..., None]

    def step(s, xs):
        q_c, k_c, v_c, G_c = xs
        dec = masked_exp(causal, G_c[:, :, :, None, :] - G_c[:, :, None, :, :])
        a = jnp.einsum('nhid,nhijd,nhjd->nhij', q_c, dec, k_c)
        o = jnp.einsum('nhid,nhde->nhie', q_c * jnp.exp(G_c), s) + jnp.einsum('nhij,nhje->nhie', a, v_c)
        gl = G_c[:, :, -1]
        s = s * jnp.exp(gl)[..., None] + jnp.einsum('nhid,nhie->nhde', k_c * jnp.exp(gl[:, :, None] - G_c), v_c)
        return s, o

    xs = tuple(jnp.moveaxis(t, 2, 0) for t in (q, k, v, G))
    s, o = lax.scan(step, s0.astype(F32), xs)
    return unchunk(o, l).astype(out_dtype), s.astype(s0.dtype)


def sink_attention(q, k, v, mask, sinks):
    s = jnp.einsum('...qhgd,...khd->...hgqk', q, k).astype(F32) * B_HD ** -0.5
    s = jnp.where(mask, s, MASK_NEG)
    sink = sinks.astype(F32).reshape(B_KV_HEADS, B_GROUP, 1, 1)
    m = jnp.maximum(jnp.max(s, -1, keepdims=True), sink)
    p = jnp.where(mask, jnp.exp(s - m), 0.0)
    probs = p / (jnp.sum(p, -1, keepdims=True) + jnp.exp(sink - m))
    return jnp.einsum('...hgqk,...khd->...qhgd', probs.astype(v.dtype), v)


def swa_prompt(q, k, v, sinks):
    n, l = q.shape[:2]
    nb = l // WINDOW
    qb = q.reshape(n, nb, WINDOW, B_KV_HEADS, B_GROUP, B_HD)
    kb = k.reshape(n, nb, WINDOW, B_KV_HEADS, B_HD)
    vb = v.reshape(n, nb, WINDOW, B_KV_HEADS, B_HD)
    shift = lambda t: jnp.concatenate([jnp.zeros_like(t[:, :1]), t[:, :-1]], axis=1)
    kx = jnp.concatenate([shift(kb), kb], axis=2)
    vx = jnp.concatenate([shift(vb), vb], axis=2)
    qi = jnp.arange(WINDOW)[:, None] + WINDOW
    kj = jnp.arange(2 * WINDOW)[None, :]
    rel = qi - kj
    band = (rel >= 0) & (rel <= WINDOW)
    valid = (jnp.arange(nb)[:, None, None] > 0) | (kj >= WINDOW)[None]
    mask = (band[None] & valid)[:, None, None]
    return sink_attention(qb, kx, vx, mask, sinks).reshape(n, l, B_QW)


def swa_sample(q, k, v, k_buf, v_buf, sinks):
    n, l = q.shape[:2]
    kx = jnp.concatenate([k_buf.astype(k.dtype), k], axis=1)
    vx = jnp.concatenate([v_buf.astype(v.dtype), v], axis=1)
    qi = jnp.arange(l)[:, None] + WINDOW
    kj = jnp.arange(WINDOW + l)[None, :]
    rel = qi - kj
    mask = (rel >= 0) & (rel <= WINDOW)
    o = sink_attention(q.reshape(n, l, B_KV_HEADS, B_GROUP, B_HD), kx, vx, mask, sinks)
    return o.reshape(n, l, B_QW), kx[:, l:], vx[:, l:]


def even_mixer(x, pos, conv_buf, s0, win_k, win_v, w_in, conv_w, a_log, dt_bias, norm_g, sinks, w_out):
    n, l, _ = x.shape
    qkv_a, gate_a, a_in, b_in, q_b, k_b, v_b, gate_b = jnp.split(x @ w_in, EVEN_SPLITS, axis=-1)
    qkv_a, new_conv = causal_conv(qkv_a, conv_buf, conv_w)
    q_a, k_a, v_a = jnp.split(qkv_a, (A_QK, 2 * A_QK), axis=-1)
    q_a = l2_normalize(q_a.reshape(n, l, A_HEADS, A_DK))
    k_a = l2_normalize(k_a.reshape(n, l, A_HEADS, A_DK))
    v_a = v_a.reshape(n, l, A_HEADS, A_DV)
    log_decay = -jnp.exp(a_log.astype(F32)) * jax.nn.softplus(a_in.astype(F32) + dt_bias.astype(F32))
    beta = jax.nn.sigmoid(b_in.astype(F32))
    o_a, new_s = gated_delta_rule(q_a, k_a, v_a, log_decay, beta, s0)
    o_a = rms_norm(o_a, norm_g).reshape(n, l, A_VW) * jax.nn.silu(gate_a)
    q_b = rotary(q_b.reshape(n, l, B_HEADS, B_HD), pos)
    k_b = rotary(k_b.reshape(n, l, B_KV_HEADS, B_HD), pos)
    v_b = v_b.reshape(n, l, B_KV_HEADS, B_HD)
    if win_k is None:
        o_b = swa_prompt(q_b, k_b, v_b, sinks)
        new_k, new_v = k_b[:, l - WINDOW:], v_b[:, l - WINDOW:]
    else:
        o_b, new_k, new_v = swa_sample(q_b, k_b, v_b, win_k, win_v, sinks)
    o_b = o_b * jax.nn.silu(gate_b)
    y = jnp.concatenate([o_a, o_b], axis=-1) @ w_out
    return y, new_conv, new_s, new_k, new_v


def odd_mixer(x, s0, lb, w_in, norm_g, w_out):
    n, l, _ = x.shape
    q, f, i, gate = jnp.split(x @ w_in, 4, axis=-1)
    q = jax.nn.silu(q).reshape(n, l, C_HEADS, C_DK)
    zf = f.astype(F32)
    fg = lb + (1.0 - lb) * jax.nn.sigmoid(zf)
    logf = jnp.log(jnp.maximum(fg, F_FLOOR)).reshape(n, l, C_HEADS, C_DK)
    k = ((1.0 - lb) * jax.nn.sigmoid(-zf)).reshape(n, l, C_HEADS, C_DK)
    v = i.reshape(n, l, C_HEADS, C_DV)
    o, new_s = hgrn2_recurrence(q, k, v, logf, s0)
    o = rms_norm(o, norm_g).reshape(n, l, D_MODEL) * jax.nn.silu(gate)
    return o @ w_out, new_s


def post_layer(x, y, p, g, b, w_ple_proj, w_ple_gate):
    h = layer_norm(DEEPNORM_ALPHA * x + y, g, b)
    return h + jax.nn.sigmoid(h @ w_ple_gate) * (p @ w_ple_proj)


def setup_inputs(seed: int = 0) -> dict:
    key = jax.random.key(seed)
    ks = jax.random.split(key, 32)
    nrm = lambda k, shape, scale=1.0: scale * jax.random.normal(k, shape, dtype=F32)
    a_log = jnp.log(jax.random.uniform(ks[11], (N_EVEN, A_HEADS), F32, 1.0, 16.0))
    dt = jnp.exp(jax.random.uniform(ks[12], (N_EVEN, A_HEADS), F32, math.log(1e-3), math.log(1e-1)))
    dt_bias = dt + jnp.log(-jnp.expm1(-dt))
    return {
        'x_prompt': nrm(ks[0], (BATCH, SEQ, D_MODEL)),
        'x_sample': nrm(ks[1], (DEC_BATCH, DEC_SEQ, D_MODEL)),
        'state_conv_a': nrm(ks[2], (N_EVEN, DEC_BATCH, A_CONV - 1, A_CONV_CH)),
        'state_delta_a': nrm(ks[3], (N_EVEN, DEC_BATCH, A_HEADS, A_DK, A_DV), 0.3),
        'cache_win_k': nrm(ks[4], (N_EVEN, DEC_BATCH, WINDOW, B_KV_HEADS, B_HD)),
        'cache_win_v': nrm(ks[5], (N_EVEN, DEC_BATCH, WINDOW, B_KV_HEADS, B_HD)),
        'state_hgrn_c': nrm(ks[6], (N_ODD, DEC_BATCH, C_HEADS, C_DK, C_DV), 0.3),
        'p_prompt': nrm(ks[7], (DEPTH, BATCH, SEQ, PLE_DIM)),
        'p_sample': nrm(ks[8], (DEPTH, DEC_BATCH, DEC_SEQ, PLE_DIM)),
        'w_in_even': nrm(ks[9], (N_EVEN, D_MODEL, EVEN_IN), D_MODEL ** -0.5),
        'conv_w_a': nrm(ks[10], (N_EVEN, A_CONV, A_CONV_CH), A_CONV ** -0.5),
        'a_log': a_log,
        'dt_bias': dt_bias,
        'norm_a': 1.0 + nrm(ks[13], (N_EVEN, A_DV), 0.02),
        'sinks_b': nrm(ks[14], (N_EVEN, B_HEADS)),
        'w_out_even': nrm(ks[15], (N_EVEN, D_MODEL, D_MODEL), DEEPNORM_BETA * D_MODEL ** -0.5),
        'w_in_odd': nrm(ks[16], (N_ODD, D_MODEL, ODD_IN), D_MODEL ** -0.5),
        'lb_raw': 1.0 + nrm(ks[17], (N_ODD, D_MODEL), 0.1),
        'norm_c': 1.0 + nrm(ks[18], (N_ODD, C_DV), 0.02),
        'w_out_odd': nrm(ks[19], (N_ODD, D_MODEL, D_MODEL), DEEPNORM_BETA * D_MODEL ** -0.5),
        'ln_g': 1.0 + nrm(ks[20], (DEPTH, D_MODEL), 0.02),
        'ln_b': nrm(ks[21], (DEPTH, D_MODEL), 0.02),
        'w_ple_proj': nrm(ks[22], (DEPTH, PLE_DIM, D_MODEL), PLE_DIM ** -0.5),
        'w_ple_gate': nrm(ks[23], (DEPTH, D_MODEL, D_MODEL), D_MODEL ** -0.5),
    }


def reference(x_prompt, x_sample, state_conv_a, state_delta_a, cache_win_k, cache_win_v, state_hgrn_c,
              p_prompt, p_sample, w_in_even, conv_w_a, a_log, dt_bias, norm_a, sinks_b, w_out_even,
              w_in_odd, lb_raw, norm_c, w_out_odd, ln_g, ln_b, w_ple_proj, w_ple_gate):
    n_p, l_p = x_prompt.shape[:2]
    l_s = x_sample.shape[1]
    pos_p = jnp.arange(l_p)
    pos_s = PAST_LEN + jnp.arange(l_s)
    lb_sm = jax.nn.softmax(lb_raw.astype(F32), axis=0)
    lb_all = jnp.cumsum(lb_sm, axis=0) - lb_sm[0]
    conv_p, conv_s, delta_p, delta_s = [], [], [], []
    wk_p, wk_s, wv_p, wv_s, hg_p, hg_s = [], [], [], [], [], []
    hp, hs = x_prompt, x_sample
    for layer in range(DEPTH):
        j = layer // 2
        if layer % 2 == 0:
            wts = (w_in_even[j], conv_w_a[j], a_log[j], dt_bias[j], norm_a[j], sinks_b[j], w_out_even[j])
            conv0 = jnp.zeros((n_p, A_CONV - 1, A_CONV_CH), hp.dtype)
            s0 = jnp.zeros((n_p, A_HEADS, A_DK, A_DV), hp.dtype)
            yp, c1, s1, k1, v1 = even_mixer(hp, pos_p, conv0, s0, None, None, *wts)
            ys, c2, s2, k2, v2 = even_mixer(hs, pos_s, state_conv_a[j], state_delta_a[j],
                                            cache_win_k[j], cache_win_v[j], *wts)
            conv_p.append(c1); conv_s.append(c2)
            delta_p.append(s1); delta_s.append(s2)
            wk_p.append(k1); wk_s.append(k2)
            wv_p.append(v1); wv_s.append(v2)
        else:
            s0 = jnp.zeros((n_p, C_HEADS, C_DK, C_DV), hp.dtype)
            yp, s1 = odd_mixer(hp, s0, lb_all[j], w_in_odd[j], norm_c[j], w_out_odd[j])
            ys, s2 = odd_mixer(hs, state_hgrn_c[j], lb_all[j], w_in_odd[j], norm_c[j], w_out_odd[j])
            hg_p.append(s1); hg_s.append(s2)
        hp = post_layer(hp, yp, p_prompt[layer], ln_g[layer], ln_b[layer], w_ple_proj[layer], w_ple_gate[layer])
        hs = post_layer(hs, ys, p_sample[layer], ln_g[layer], ln_b[layer], w_ple_proj[layer], w_ple_gate[layer])
    return (hp, hs, jnp.stack(conv_p), jnp.stack(conv_s), jnp.stack(delta_p), jnp.stack(delta_s),
            jnp.stack(wk_p), jnp.stack(wk_s), jnp.stack(wv_p), jnp.stack(wv_s), jnp.stack(hg_p), jnp.stack(hg_s))
```

```python
import functools
import math

import jax
import jax.numpy as jnp
from jax import lax
from jax.experimental import pallas as pl
from jax.experimental.pallas import tpu as pltpu

F32 = jnp.float32
BF16 = jnp.bfloat16

D_MODEL = 1024
DEPTH = 4
PAST_LEN = 8192
PLE_DIM = 256
NORM_EPS = 1e-6
MASK_NEG = -1e30
F_FLOOR = 1e-30
DEEPNORM_ALPHA = (2 * DEPTH) ** 0.25

A_HEADS = 4
A_DK = 128
A_CONV = 4
A_QK = 512
A_VW = 512
A_CONV_CH = 1536
A_CHUNK = 64

B_HEADS = 8
B_KV_HEADS = 2
B_HD = 64
B_QW = 512
B_KVW = 128
WINDOW = 128
ROT_DIM = 16
ROPE_THETA = 500000.0

C_HEADS = 8
C_DK = 128
C_CHUNK = 64

LANES = 128
VMEM_LIMIT = 56 * 1024 * 1024

E_QKV = 0
E_GA = 1536
E_QB = 2048
E_KB = 2560
E_VB = 2688
E_GB = 2816
E_AB = 3328
E_TOT = 3456

NT = (((1,), (1,)), ((), ()))
TN = (((0,), (0,)), ((), ()))


def _mm(a, b):
    return jnp.dot(a.astype(BF16), b.astype(BF16), preferred_element_type=F32)


def _mm_nt(a, b):
    return lax.dot_general(a.astype(BF16), b.astype(BF16), NT, preferred_element_type=F32)


def _mm_tn(a, b):
    return lax.dot_general(a.astype(BF16), b.astype(BF16), TN, preferred_element_type=F32)


def _sigmoid(x):
    return 1.0 / (1.0 + jnp.exp(-x))


def _silu(x):
    return x * _sigmoid(x)


def _cumsum_rows(x):
    n = x.shape[0]
    row = lax.broadcasted_iota(jnp.int32, x.shape, 0)
    s = 1
    while s < n:
        x = x + jnp.where(row >= s, pltpu.roll(x, s, axis=0), 0.0)
        s *= 2
    return x


def _params(n_grid):
    return pltpu.CompilerParams(dimension_semantics=("arbitrary",) * n_grid,
                                vmem_limit_bytes=VMEM_LIMIT)


def _const_spec(shape):
    nd = len(shape)
    return pl.BlockSpec(shape, lambda *_: (0,) * nd)


def _even_in_body(x_ref, w_ref, cw_ref, misc_ref, rot_ref, buf_ref,
                  qkv_ref, gb_ref, ga_ref, qb_ref, kvb_ref, gbt_ref, conv_ref, xp_ref,
                  *, nb, tl):
    t = pl.program_id(1)
    r = nb * tl
    x = x_ref[...].reshape(r, D_MODEL).astype(BF16)

    z = jnp.dot(x, w_ref[:, E_QKV:E_QKV + A_CONV_CH], preferred_element_type=F32)
    if buf_ref is not None:
        xp_ref[:, 5:8, :] = buf_ref[...]
    else:
        @pl.when(t == 0)
        def _():
            xp_ref[:, 5:8, :] = jnp.zeros((nb, 3, A_CONV_CH), F32)
    xp_ref[:, 8:8 + tl, :] = z.reshape(nb, tl, A_CONV_CH)
    y = xp_ref[:, 5:5 + tl, :] * cw_ref[0:1, :].reshape(1, 1, A_CONV_CH)
    for j in range(1, A_CONV):
        y = y + xp_ref[:, 5 + j:5 + j + tl, :] * cw_ref[j:j + 1, :].reshape(1, 1, A_CONV_CH)
    last = xp_ref[:, tl + 5:tl + 8, :]
    conv_ref[...] = last
    if buf_ref is None:
        xp_ref[:, 5:8, :] = last
    y = _silu(y).reshape(r, A_CONV_CH)
    for h in range(2 * A_HEADS):
        c = y[:, h * A_DK:(h + 1) * A_DK]
        inv = lax.rsqrt(jnp.sum(c * c, axis=-1, keepdims=True) + NORM_EPS)
        if h < A_HEADS:
            inv = inv * (A_DK ** -0.5)
        qkv_ref[:, :, h * A_DK:(h + 1) * A_DK] = (c * inv).reshape(nb, tl, A_DK)
    qkv_ref[:, :, 2 * A_QK:] = y[:, 2 * A_QK:].reshape(nb, tl, A_VW)

    ab = jnp.dot(x, w_ref[:, E_AB:E_AB + LANES], preferred_element_type=F32)
    neg_a = -jnp.exp(misc_ref[0:1, :])
    sp_in = ab + misc_ref[1:2, :]
    softplus = jnp.maximum(sp_in, 0.0) + jnp.log1p(jnp.exp(-jnp.abs(sp_in)))
    lane = lax.broadcasted_iota(jnp.int32, ab.shape, 1)
    gb = jnp.where(lane < A_HEADS, neg_a * softplus, _sigmoid(ab))
    gb_ref[...] = gb.reshape(nb, tl, LANES)

    ga_ref[...] = _silu(jnp.dot(x, w_ref[:, E_GA:E_GA + A_VW],
                                preferred_element_type=F32)).reshape(nb, tl, A_VW)
    gbt_ref[...] = _silu(jnp.dot(x, w_ref[:, E_GB:E_GB + B_QW],
                                 preferred_element_type=F32)).reshape(nb, tl, B_QW)

    cos_t = rot_ref[0]
    s_up = rot_ref[1]
    s_dn = rot_ref[2]
    zq = jnp.dot(x, w_ref[:, E_QB:E_QB + B_QW], preferred_element_type=F32)
    rep = B_QW // LANES
    cos4 = jnp.concatenate([cos_t] * rep, axis=1)
    up4 = jnp.concatenate([s_up] * rep, axis=1)
    dn4 = jnp.concatenate([s_dn] * rep, axis=1)
    half = ROT_DIM // 2
    qrot = zq * cos4 + pltpu.roll(zq, B_QW - half, axis=1) * up4 + pltpu.roll(zq, half, axis=1) * dn4
    qb_ref[...] = qrot.reshape(nb, tl, B_QW)
    zk = jnp.dot(x, w_ref[:, E_KB:E_KB + B_KVW], preferred_element_type=F32)
    krot = zk * cos_t + pltpu.roll(zk, LANES - half, axis=1) * s_up + pltpu.roll(zk, half, axis=1) * s_dn
    kvb_ref[:, :, 0:B_KVW] = krot.reshape(nb, tl, B_KVW)
    zv = jnp.dot(x, w_ref[:, E_VB:E_VB + B_KVW], preferred_element_type=F32)
    kvb_ref[:, :, B_KVW:] = zv.reshape(nb, tl, B_KVW)


def _even_in_kernel_buf(x_ref, w_ref, cw_ref, misc_ref, rot_ref, buf_ref, *rest, nb, tl):
    _even_in_body(x_ref, w_ref, cw_ref, misc_ref, rot_ref, buf_ref, *rest, nb=nb, tl=tl)


def _even_in_kernel_nobuf(x_ref, w_ref, cw_ref, misc_ref, rot_ref, *rest, nb, tl):
    _even_in_body(x_ref, w_ref, cw_ref, misc_ref, rot_ref, None, *rest, nb=nb, tl=tl)


def _even_in(x, w, cw, misc, rot, buf, *, nb, tl):
    n, l, _ = x.shape
    grid = (n // nb, l // tl)
    r = nb * tl
    row_map = lambda i, t: (i, t, 0)
    rot_map = (lambda i, t: (0, t, 0)) if buf is None else (lambda i, t: (0, 0, 0))
    in_specs = [
        pl.BlockSpec((nb, tl, D_MODEL), row_map),
        _const_spec((D_MODEL, E_TOT)),
        _const_spec((A_CONV, A_CONV_CH)),
        _const_spec((8, LANES)),
        pl.BlockSpec((3, r, LANES), rot_map),
    ]
    args = [x, w, cw, misc, rot]
    if buf is not None:
        in_specs.append(pl.BlockSpec((nb, A_CONV - 1, A_CONV_CH), lambda i, t: (i, 0, 0)))
        args.append(buf)
        body = functools.partial(_even_in_kernel_buf, nb=nb, tl=tl)
    else:
        body = functools.partial(_even_in_kernel_nobuf, nb=nb, tl=tl)
    widths = (A_CONV_CH, LANES, A_VW, B_QW, 2 * B_KVW, B_QW)
    out_shape = [jax.ShapeDtypeStruct((n, l, wd), F32) for wd in widths]
    out_specs = [pl.BlockSpec((nb, tl, wd), row_map) for wd in widths]
    out_shape.append(jax.ShapeDtypeStruct((n, A_CONV - 1, A_CONV_CH), F32))
    out_specs.append(pl.BlockSpec((nb, A_CONV - 1, A_CONV_CH), lambda i, t: (i, 0, 0)))
    return pl.pallas_call(
        body, grid=grid, in_specs=in_specs, out_specs=out_specs, out_shape=out_shape,
        scratch_shapes=[pltpu.VMEM((nb, tl + 8, A_CONV_CH), F32)],
        compiler_params=_params(2), name="even_in",
    )(*args)


def _tri_inv(a, c):
    ri = lax.broadcasted_iota(jnp.int32, (c, c), 0)
    ci = lax.broadcasted_iota(jnp.int32, (c, c), 1)
    p = jnp.where(ri == ci, 1.0, 0.0) - a
    m = a
    span = 2
    while span < c:
        m = _mm(m, m)
        p = p + _mm(p, m)
        span *= 2
    return p


def _delta_chunk(q, k, v, gbc, s, h, c):
    g_all = _cumsum_rows(gbc)
    if c < LANES:
        g_pad = jnp.concatenate([g_all, jnp.zeros((LANES - c, LANES), F32)], axis=0)
    else:
        g_pad = g_all
    g_t = g_pad.T
    g_col = g_all[:, h:h + 1]
    g_row = g_t[h:h + 1, 0:c]
    beta = gbc[:, A_HEADS + h:A_HEADS + h + 1]
    ri = lax.broadcasted_iota(jnp.int32, (c, c), 0)
    ci = lax.broadcasted_iota(jnp.int32, (c, c), 1)
    causal = ri >= ci
    decay = jnp.where(causal, jnp.exp(jnp.where(causal, g_col - g_row, 0.0)), 0.0)
    kb = k * beta
    a = jnp.where(ri > ci, _mm_nt(kb, k) * decay, 0.0)
    t_inv = _tri_inv(a, c)
    e_g = jnp.exp(g_col)
    uw = _mm(t_inv, jnp.concatenate([v * beta, kb * e_g], axis=1))
    u = uw[:, :A_DK]
    w = uw[:, A_DK:]
    qk = _mm_nt(q, k) * decay
    g_last = g_all[c - 1:c, h:h + 1]
    k_dec = k * jnp.exp(g_last - g_col)
    ws_qs = _mm(jnp.concatenate([w, q * e_g], axis=0), s)
    v_new = u - ws_qs[:c]
    o = ws_qs[c:] + _mm(qk, v_new)
    s_new = s * jnp.exp(g_last) + _mm_tn(k_dec, v_new)
    return o, s_new


def _delta_body(qkv_ref, gb_ref, s0_ref, o_ref, sout_ref, s_ref, *, nb, tl, c):
    t = pl.program_id(1)
    nt = pl.num_programs(1)

    @pl.when(t == 0)
    def _():
        if s0_ref is None:
            s_ref[...] = jnp.zeros(s_ref.shape, F32)
        else:
            s_ref[...] = s0_ref[...]

    def chunk(ic, carry):
        r0 = pl.multiple_of(ic * c, c)
        rows = pl.ds(r0, c)
        for b in range(nb):
            gbc = gb_ref[b, rows, :]
            for h in range(A_HEADS):
                q = qkv_ref[b, rows, h * A_DK:(h + 1) * A_DK]
                k = qkv_ref[b, rows, A_QK + h * A_DK:A_QK + (h + 1) * A_DK]
                v = qkv_ref[b, rows, 2 * A_QK + h * A_DK:2 * A_QK + (h + 1) * A_DK]
                o, s_new = _delta_chunk(q, k, v, gbc, s_ref[b, h], h, c)
                s_ref[b, h] = s_new
                o_ref[b, rows, h * A_DK:(h + 1) * A_DK] = o
        return carry

    lax.fori_loop(0, tl // c, chunk, 0)

    @pl.when(t == nt - 1)
    def _():
        sout_ref[...] = s_ref[...]


def _delta_kernel_s0(qkv_ref, gb_ref, s0_ref, o_ref, sout_ref, s_ref, *, nb, tl, c):
    _delta_body(qkv_ref, gb_ref, s0_ref, o_ref, sout_ref, s_ref, nb=nb, tl=tl, c=c)


def _delta_kernel_zero(qkv_ref, gb_ref, o_ref, sout_ref, s_ref, *, nb, tl, c):
    _delta_body(qkv_ref, gb_ref, None, o_ref, sout_ref, s_ref, nb=nb, tl=tl, c=c)


def _delta(qkv, gb, s0, *, nb, tl, c):
    n, l, _ = qkv.shape
    grid = (n // nb, l // tl)
    row_map = lambda i, t: (i, t, 0)
    st_map = lambda i, t: (i, 0, 0, 0)
    st_block = (nb, A_HEADS, A_DK, A_DK)
    in_specs = [pl.BlockSpec((nb, tl, A_CONV_CH), row_map), pl.BlockSpec((nb, tl, LANES), row_map)]
    args = [qkv, gb]
    if s0 is not None:
        in_specs.append(pl.BlockSpec(st_block, st_map))
        args.append(s0)
        body = functools.partial(_delta_kernel_s0, nb=nb, tl=tl, c=c)
    else:
        body = functools.partial(_delta_kernel_zero, nb=nb, tl=tl, c=c)
    return pl.pallas_call(
        body, grid=grid, in_specs=in_specs,
        out_specs=[pl.BlockSpec((nb, tl, A_VW), row_map), pl.BlockSpec(st_block, st_map)],
        out_shape=[jax.ShapeDtypeStruct((n, l, A_VW), F32),
                   jax.ShapeDtypeStruct((n, A_HEADS, A_DK, A_DK), F32)],
        scratch_shapes=[pltpu.VMEM(st_block, F32)],
        compiler_params=_params(2), name="delta",
    )(*args)


def _swa_block(q, kx, vx, sink_ref, min_key, o_store):
    nq = q.shape[0]
    nk = 2 * WINDOW
    kxr = pltpu.roll(kx, B_HD, axis=1)
    vxr = pltpu.roll(vx, B_HD, axis=1)
    lo = lax.broadcasted_iota(jnp.int32, (nk, LANES), 1) < B_HD
    ri = lax.broadcasted_iota(jnp.int32, (nq, nk), 0)
    ci = lax.broadcasted_iota(jnp.int32, (nq, nk), 1)
    mask = (ci >= jnp.maximum(ri, min_key)) & (ci <= ri + WINDOW)
    scale = B_HD ** -0.5
    for g in range(B_KV_HEADS):
        if g == 0:
            kbd = jnp.concatenate([jnp.where(lo, kx, 0.0), jnp.where(lo, 0.0, kxr)], axis=0)
            vbd = jnp.concatenate([jnp.where(lo, vx, 0.0), jnp.where(lo, 0.0, vxr)], axis=0)
        else:
            kbd = jnp.concatenate([jnp.where(lo, kxr, 0.0), jnp.where(lo, 0.0, kx)], axis=0)
            vbd = jnp.concatenate([jnp.where(lo, vxr, 0.0), jnp.where(lo, 0.0, vx)], axis=0)
        kbd = kbd.astype(BF16)
        vbd = vbd.astype(BF16)
        for jj in range(2):
            j = 2 * g + jj
            s = _mm_nt(q[:, j * LANES:(j + 1) * LANES], kbd) * scale
            probs = []
            for e in range(2):
                sink = sink_ref[2 * j + e]
                se = jnp.where(mask, s[:, e * nk:(e + 1) * nk], MASK_NEG)
                m = jnp.maximum(jnp.max(se, axis=-1, keepdims=True), sink)
                p = jnp.where(mask, jnp.exp(se - m), 0.0)
                denom = jnp.sum(p, axis=-1, keepdims=True) + jnp.exp(sink - m)
                probs.append(p / denom)
            o_store(j, _mm(jnp.concatenate(probs, axis=1), vbd))


def _swa_prompt_kernel(sink_ref, q_ref, kvc_ref, kvp_ref, o_ref):
    i = pl.program_id(1)
    kvc = kvc_ref[0]
    kvp = kvp_ref[0]
    kx = jnp.concatenate([kvp[:, :B_KVW], kvc[:, :B_KVW]], axis=0)
    vx = jnp.concatenate([kvp[:, B_KVW:], kvc[:, B_KVW:]], axis=0)

    def store(j, val):
        o_ref[0, :, j * LANES:(j + 1) * LANES] = val

    _swa_block(q_ref[0], kx, vx, sink_ref, jnp.where(i > 0, 0, WINDOW), store)


def _swa_prompt(qb, kvb, sinks):
    n, l, _ = qb.shape
    nblk = l // WINDOW
    return pl.pallas_call(
        _swa_prompt_kernel, grid=(n, nblk),
        in_specs=[pl.BlockSpec(memory_space=pltpu.SMEM),
                  pl.BlockSpec((1, WINDOW, B_QW), lambda b, i: (b, i, 0)),
                  pl.BlockSpec((1, WINDOW, 2 * B_KVW), lambda b, i: (b, i, 0)),
                  pl.BlockSpec((1, WINDOW, 2 * B_KVW), lambda b, i: (b, jnp.maximum(i - 1, 0), 0))],
        out_specs=pl.BlockSpec((1, WINDOW, B_QW), lambda b, i: (b, i, 0)),
        out_shape=jax.ShapeDtypeStruct((n, l, B_QW), F32),
        compiler_params=_params(2), name="swa_prompt",
    )(sinks, qb, kvb, kvb)


def _swa_sample_kernel(sink_ref, q_ref, kv_ref, ck_ref, cv_ref, o_ref, nk_ref, nv_ref, *, nb, ls):
    pad = jnp.zeros((WINDOW - ls, B_KVW), F32)
    for b in range(nb):
        kv = kv_ref[b]
        k_new = kv[:, :B_KVW]
        v_new = kv[:, B_KVW:]
        ck = ck_ref[b]
        cv = cv_ref[b]
        kx = jnp.concatenate([ck, k_new, pad], axis=0)
        vx = jnp.concatenate([cv, v_new, pad], axis=0)

        def store(j, val, b=b):
            o_ref[b, :, j * LANES:(j + 1) * LANES] = val

        _swa_block(q_ref[b], kx, vx, sink_ref, 0, store)
        nk_ref[b, 0:WINDOW - ls, :] = ck[ls:, :]
        nk_ref[b, WINDOW - ls:, :] = k_new
        nv_ref[b, 0:WINDOW - ls, :] = cv[ls:, :]
        nv_ref[b, WINDOW - ls:, :] = v_new


def _swa_sample(qb, kvb, ck, cv, sinks, *, nb):
    n, ls, _ = qb.shape
    row_map = lambda i: (i, 0, 0)
    cache_spec = pl.BlockSpec((nb, WINDOW, B_KVW), row_map)
    cache_shape = jax.ShapeDtypeStruct((n, WINDOW, B_KVW), F32)
    return pl.pallas_call(
        functools.partial(_swa_sample_kernel, nb=nb, ls=ls), grid=(n // nb,),
        in_specs=[pl.BlockSpec(memory_space=pltpu.SMEM),
                  pl.BlockSpec((nb, ls, B_QW), row_map),
                  pl.BlockSpec((nb, ls, 2 * B_KVW), row_map),
                  cache_spec, cache_spec],
        out_specs=[pl.BlockSpec((nb, ls, B_QW), row_map), cache_spec, cache_spec],
        out_shape=[jax.ShapeDtypeStruct((n, ls, B_QW), F32), cache_shape, cache_shape],
        compiler_params=_params(1), name="swa_sample",
    )(sinks, qb, kvb, ck, cv)


def _odd_in_kernel(x_ref, w_ref, lbraw_ref, q_ref, k_ref, v_ref, lf_ref, g_ref, *, j):
    x = x_ref[...].astype(BF16)
    raw = lbraw_ref[...]
    e = jnp.exp(raw - jnp.max(raw, axis=0, keepdims=True))
    sm = e / jnp.sum(e, axis=0, keepdims=True)
    lb = jnp.zeros((1, D_MODEL), F32)
    for i in range(1, j + 1):
        lb = lb + sm[i:i + 1, :]
    d = D_MODEL
    q_ref[...] = _silu(jnp.dot(x, w_ref[:, 0:d], preferred_element_type=F32)) * (C_DK ** -0.5)
    zf = jnp.dot(x, w_ref[:, d:2 * d], preferred_element_type=F32)
    fg = lb + (1.0 - lb) * _sigmoid(zf)
    lf_ref[...] = jnp.log(jnp.maximum(fg, F_FLOOR))
    k_ref[...] = (1.0 - lb) * _sigmoid(-zf)
    v_ref[...] = jnp.dot(x, w_ref[:, 2 * d:3 * d], preferred_element_type=F32)
    g_ref[...] = _silu(jnp.dot(x, w_ref[:, 3 * d:4 * d], preferred_element_type=F32))


def _odd_in(x2, w, lb_raw, j, *, tr):
    rows = x2.shape[0]
    n_odd = lb_raw.shape[0]
    row_spec = pl.BlockSpec((tr, D_MODEL), lambda i: (i, 0))
    shp = jax.ShapeDtypeStruct((rows, D_MODEL), F32)
    return pl.pallas_call(
        functools.partial(_odd_in_kernel, j=j), grid=(rows // tr,),
        in_specs=[row_spec, _const_spec((D_MODEL, 4 * D_MODEL)), _const_spec((n_odd, D_MODEL))],
        out_specs=[row_spec] * 5, out_shape=[shp] * 5,
        compiler_params=_params(1), name="odd_in",
    )(x2, w, lb_raw)


def _hgrn_chunk(q, k, v, lf, st, c):
    g = _cumsum_rows(lf)
    ri = lax.broadcasted_iota(jnp.int32, (c, c), 0)
    ci = lax.broadcasted_iota(jnp.int32, (c, c), 1)
    row = lax.broadcasted_iota(jnp.int32, (c, C_DK), 0)
    a = jnp.where(ri == ci, _mm_nt(q, k), 0.0)
    b = g
    h = 1
    while h < c:
        bn = pltpu.roll(b, c - h, axis=0)
        qh = q if h == 1 else q * jnp.exp(g - b)
        kh = k * jnp.exp(bn - g)
        lvl = ((ri // h) == (ci // h) + 1) & ((ri // (2 * h)) == (ci // (2 * h)))
        a = a + jnp.where(lvl, _mm_nt(qh, kh), 0.0)
        b = jnp.where((row & h) != 0, pltpu.roll(b, h, axis=0), b)
        h *= 2
    g_last = g[c - 1:c, :]
    o = _mm_nt(q * jnp.exp(g), st) + _mm(a, v)
    st_new = st * jnp.exp(g_last) + _mm_tn(v, k * jnp.exp(g_last - g))
    return o, st_new


def _hgrn_body(q_ref, k_ref, v_ref, lf_ref, s0_ref, o_ref, sout_ref, st_ref, *, nb, hb, tl, c):
    t = pl.program_id(2)
    nt = pl.num_programs(2)

    @pl.when(t == 0)
    def _():
        if s0_ref is None:
            st_ref[...] = jnp.zeros(st_ref.shape, F32)
        else:
            for b in range(nb):
                for h in range(hb):
                    st_ref[b, h] = s0_ref[b, h].T

    def chunk(ic, carry):
        r0 = pl.multiple_of(ic * c, c)
        rows = pl.ds(r0, c)
        for b in range(nb):
            for h in range(hb):
                cols = slice(h * C_DK, (h + 1) * C_DK)
                o, st_new = _hgrn_chunk(q_ref[b, rows, cols], k_ref[b, rows, cols], v_ref[b, rows, cols],
                                        lf_ref[b, rows, cols], st_ref[b, h], c)
                st_ref[b, h] = st_new
                o_ref[b, rows, cols] = o
        return carry

    lax.fori_loop(0, tl // c, chunk, 0)

    @pl.when(t == nt - 1)
    def _():
        for b in range(nb):
            for h in range(hb):
                sout_ref[b, h] = st_ref[b, h].T


def _hgrn_kernel_s0(q_ref, k_ref, v_ref, lf_ref, s0_ref, o_ref, sout_ref, st_ref, **kw):
    _hgrn_body(q_ref, k_ref, v_ref, lf_ref, s0_ref, o_ref, sout_ref, st_ref, **kw)


def _hgrn_kernel_zero(q_ref, k_ref, v_ref, lf_ref, o_ref, sout_ref, st_ref, **kw):
    _hgrn_body(q_ref, k_ref, v_ref, lf_ref, None, o_ref, sout_ref, st_ref, **kw)


def _hgrn(q, k, v, lf, s0, *, nb, hb, tl, c):
    n, l, _ = q.shape
    grid = (n // nb, C_HEADS // hb, l // tl)
    row_spec = pl.BlockSpec((nb, tl, hb * C_DK), lambda i, g, t: (i, t, g))
    st_block = (nb, hb, C_DK, C_DK)
    st_spec = pl.BlockSpec(st_block, lambda i, g, t: (i, g, 0, 0))
    in_specs = [row_spec] * 4
    args = [q, k, v, lf]
    kw = dict(nb=nb, hb=hb, tl=tl, c=c)
    if s0 is not None:
        in_specs = in_specs + [st_spec]
        args.append(s0)
        body = functools.partial(_hgrn_kernel_s0, **kw)
    else:
        body = functools.partial(_hgrn_kernel_zero, **kw)
    return pl.pallas_call(
        body, grid=grid, in_specs=in_specs, out_specs=[row_spec, st_spec],
        out_shape=[jax.ShapeDtypeStruct((n, l, D_MODEL), F32),
                   jax.ShapeDtypeStruct((n, C_HEADS, C_DK, C_DK), F32)],
        scratch_shapes=[pltpu.VMEM(st_block, F32)],
        compiler_params=_params(3), name="hgrn",
    )(*args)


def _group_rms(o, g_row, heads):
    outs = []
    for h in range(heads):
        c = o[:, h * LANES:(h + 1) * LANES]
        inv = lax.rsqrt(jnp.mean(c * c, axis=-1, keepdims=True) + NORM_EPS)
        outs.append(c * inv * g_row)
    return jnp.concatenate(outs, axis=1)


def _post_tail(x, y, p_ref, lng_ref, lnb_ref, wproj_ref, wgate_ref, out_ref):
    pre = DEEPNORM_ALPHA * x + y
    mu = jnp.mean(pre, axis=-1, keepdims=True)
    xc = pre - mu
    var = jnp.mean(xc * xc, axis=-1, keepdims=True)
    hn = xc * lax.rsqrt(var + NORM_EPS) * lng_ref[...] + lnb_ref[...]
    gate = _sigmoid(jnp.dot(hn.astype(BF16), wgate_ref[...], preferred_element_type=F32))
    emb = jnp.dot(p_ref[...].astype(BF16), wproj_ref[...], preferred_element_type=F32)
    out_ref[...] = hn + gate * emb


def _post_even_kernel(x_ref, oa_ref, ga_ref, ob_ref, gbt_ref, p_ref, wout_ref, ng_ref, lng_ref, lnb_ref,
                      wproj_ref, wgate_ref, out_ref):
    a = _group_rms(oa_ref[...], ng_ref[...], A_HEADS) * ga_ref[...]
    b = ob_ref[...] * gbt_ref[...]
    y = (jnp.dot(a.astype(BF16), wout_ref[0:A_VW, :], preferred_element_type=F32)
         + jnp.dot(b.astype(BF16), wout_ref[A_VW:, :], preferred_element_type=F32))
    _post_tail(x_ref[...], y, p_ref, lng_ref, lnb_ref, wproj_ref, wgate_ref, out_ref)


def _post_odd_kernel(x_ref, o_ref, g_ref, p_ref, wout_ref, ng_ref, lng_ref, lnb_ref,
                     wproj_ref, wgate_ref, out_ref):
    a = _group_rms(o_ref[...], ng_ref[...], C_HEADS) * g_ref[...]
    y = jnp.dot(a.astype(BF16), wout_ref[...], preferred_element_type=F32)
    _post_tail(x_ref[...], y, p_ref, lng_ref, lnb_ref, wproj_ref, wgate_ref, out_ref)


def _post(x2, acts, p2, wout, ng, lng, lnb, wproj, wgate, *, even, tr):
    rows = x2.shape[0]
    full = pl.BlockSpec((tr, D_MODEL), lambda i: (i, 0))
    half = pl.BlockSpec((tr, A_VW), lambda i: (i, 0))
    act_specs = [half] * 4 if even else [full] * 2
    in_specs = ([full] + act_specs + [pl.BlockSpec((tr, PLE_DIM), lambda i: (i, 0)),
                                      _const_spec((D_MODEL, D_MODEL)), _const_spec((1, LANES)),
                                      _const_spec((1, D_MODEL)), _const_spec((1, D_MODEL)),
                                      _const_spec((PLE_DIM, D_MODEL)), _const_spec((D_MODEL, D_MODEL))])
    return pl.pallas_call(
        _post_even_kernel if even else _post_odd_kernel, grid=(rows // tr,),
        in_specs=in_specs, out_specs=full, out_shape=jax.ShapeDtypeStruct((rows, D_MODEL), F32),
        compiler_params=_params(1), name="post_even" if even else "post_odd",
    )(x2, *acts, p2, wout, ng, lng, lnb, wproj, wgate)


def _pack_even_w(w):
    qkv = w[:, 0:1536]
    ga = w[:, 1536:2048]
    ab = w[:, 2048:2056]
    qb = w[:, 2056:2568]
    kb = w[:, 2568:2696]
    vb = w[:, 2696:2824]
    gbt = w[:, 2824:3336]
    ab = jnp.pad(ab, ((0, 0), (0, LANES - 2 * A_HEADS)))
    return jnp.concatenate([qkv, ga, qb, kb, vb, gbt, ab], axis=1).astype(BF16)


def _rot_tables(pos, reps):
    inv = ROPE_THETA ** (-jnp.arange(0, ROT_DIM, 2, dtype=F32) / ROT_DIM)
    ang = pos.astype(F32)[:, None] * inv[None, :]
    cos = jnp.cos(ang)
    sin = jnp.sin(ang)
    n = pos.shape[0]
    half = ROT_DIM // 2
    cos64 = jnp.concatenate([cos, cos, jnp.ones((n, B_HD - ROT_DIM), F32)], axis=1)
    up64 = jnp.concatenate([-sin, jnp.zeros((n, B_HD - half), F32)], axis=1)
    dn64 = jnp.concatenate([jnp.zeros((n, half), F32), sin, jnp.zeros((n, B_HD - ROT_DIM), F32)], axis=1)
    tabs = jnp.stack([jnp.tile(t, (reps, LANES // B_HD)) for t in (cos64, up64, dn64)])
    return tabs


def kernel(x_prompt, x_sample, state_conv_a, state_delta_a, cache_win_k, cache_win_v, state_hgrn_c,
           p_prompt, p_sample, w_in_even, conv_w_a, a_log, dt_bias, norm_a, sinks_b, w_out_even,
           w_in_odd, lb_raw, norm_c, w_out_odd, ln_g, ln_b, w_ple_proj, w_ple_gate):
    n_p, l_p, _ = x_prompt.shape
    n_s, l_s, _ = x_sample.shape
    nb_s = 32
    rot_p = _rot_tables(jnp.arange(l_p), 1)
    rot_s = _rot_tables(PAST_LEN + jnp.arange(l_s), nb_s)
    conv_p, conv_s, delta_p, delta_s = [], [], [], []
    wk_p, wk_s, wv_p, wv_s, hg_p, hg_s = [], [], [], [], [], []
    hp, hs = x_prompt, x_sample
    for layer in range(DEPTH):
        j = layer // 2
        lng = ln_g[layer].reshape(1, D_MODEL)
        lnb = ln_b[layer].reshape(1, D_MODEL)
        wproj = w_ple_proj[layer].astype(BF16)
        wgate = w_ple_gate[layer].astype(BF16)
        hp2 = hp.reshape(n_p * l_p, D_MODEL)
        hs2 = hs.reshape(n_s * l_s, D_MODEL)
        pp2 = p_prompt[layer].reshape(n_p * l_p, PLE_DIM)
        ps2 = p_sample[layer].reshape(n_s * l_s, PLE_DIM)
        if layer % 2 == 0:
            w = _pack_even_w(w_in_even[j])
            misc = jnp.zeros((8, LANES), F32)
            misc = misc.at[0, :A_HEADS].set(a_log[j]).at[1, :A_HEADS].set(dt_bias[j])
            cw = conv_w_a[j]
            sinks = sinks_b[j]
            wout = w_out_even[j].astype(BF16)
            ng = norm_a[j].reshape(1, LANES)
            qkv, gb, ga, qb, kvb, gbt, c1 = _even_in(hp, w, cw, misc, rot_p, None, nb=1, tl=256)
            oa, s1 = _delta(qkv, gb, None, nb=1, tl=256, c=A_CHUNK)
            ob = _swa_prompt(qb, kvb, sinks)
            acts = [t.reshape(n_p * l_p, A_VW) for t in (oa, ga, ob, gbt)]
            hp = _post(hp2, acts, pp2, wout, ng, lng, lnb, wproj, wgate, even=True, tr=512)
            hp = hp.reshape(n_p, l_p, D_MODEL)
            k1 = kvb[:, l_p - WINDOW:, :B_KVW].reshape(n_p, WINDOW, B_KV_HEADS, B_HD)
            v1 = kvb[:, l_p - WINDOW:, B_KVW:].reshape(n_p, WINDOW, B_KV_HEADS, B_HD)
            qkv, gb, ga, qb, kvb, gbt, c2 = _even_in(hs, w, cw, misc, rot_s, state_conv_a[j], nb=nb_s, tl=l_s)
            oa, s2 = _delta(qkv, gb, state_delta_a[j], nb=4, tl=l_s, c=l_s)
            ck = cache_win_k[j].reshape(n_s, WINDOW, B_KVW)
            cv = cache_win_v[j].reshape(n_s, WINDOW, B_KVW)
            ob, k2, v2 = _swa_sample(qb, kvb, ck, cv, sinks, nb=8)
            acts = [t.reshape(n_s * l_s, A_VW) for t in (oa, ga, ob, gbt)]
            hs = _post(hs2, acts, ps2, wout, ng, lng, lnb, wproj, wgate, even=True, tr=512)
            hs = hs.reshape(n_s, l_s, D_MODEL)
            conv_p.append(c1); conv_s.append(c2)
            delta_p.append(s1); delta_s.append(s2)
            wk_p.append(k1); wv_p.append(v1)
            wk_s.append(k2.reshape(n_s, WINDOW, B_KV_HEADS, B_HD))
            wv_s.append(v2.reshape(n_s, WINDOW, B_KV_HEADS, B_HD))
        else:
            w = w_in_odd[j].astype(BF16)
            wout = w_out_odd[j].astype(BF16)
            ng = norm_c[j].reshape(1, LANES)
            q, k, v, lf, g = _odd_in(hp2, w, lb_raw, j, tr=256)
            shp = (n_p, l_p, D_MODEL)
            o, s1 = _hgrn(q.reshape(shp), k.reshape(shp), v.reshape(shp), lf.reshape(shp), None,
                          nb=1, hb=2, tl=256, c=C_CHUNK)
            hp = _post(hp2, [o.reshape(n_p * l_p, D_MODEL), g], pp2, wout, ng, lng, lnb, wproj, wgate,
                       even=False, tr=512)
            hp = hp.reshape(n_p, l_p, D_MODEL)
            q, k, v, lf, g = _odd_in(hs2, w, lb_raw, j, tr=256)
            shp = (n_s, l_s, D_MODEL)
            o, s2 = _hgrn(q.reshape(shp), k.reshape(shp), v.reshape(shp), lf.reshape(shp), state_hgrn_c[j],
                          nb=4, hb=2, tl=l_s, c=l_s)
            hs = _post(hs2, [o.reshape(n_s * l_s, D_MODEL), g], ps2, wout, ng, lng, lnb, wproj, wgate,
                       even=False, tr=512)
            hs = hs.reshape(n_s, l_s, D_MODEL)
            hg_p.append(s1); hg_s.append(s2)
    return (hp, hs, jnp.stack(conv_p), jnp.stack(conv_s), jnp.stack(delta_p), jnp.stack(delta_s),
            jnp.stack(wk_p), jnp.stack(wk_s), jnp.stack(wv_p), jnp.stack(wv_s), jnp.stack(hg_p), jnp.stack(hg_s))
```

```python
import functools
import math

import jax
import jax.numpy as jnp
from jax import lax
from jax.experimental import pallas as pl
from jax.experimental.pallas import tpu as pltpu

F32 = jnp.float32
BF16 = jnp.bfloat16

D_MODEL = 1024
DEPTH = 4
PAST_LEN = 8192
PLE_DIM = 256
NORM_EPS = 1e-6
MASK_NEG = -1e30
F_FLOOR = 1e-30
DEEPNORM_ALPHA = (2 * DEPTH) ** 0.25

A_HEADS = 4
A_DK = 128
A_CONV = 4
A_QK = 512
A_VW = 512
A_CONV_CH = 1536
A_CHUNK = 64

B_HEADS = 8
B_KV_HEADS = 2
B_HD = 64
B_QW = 512
B_KVW = 128
WINDOW = 128
ROT_DIM = 16
ROPE_THETA = 500000.0

C_HEADS = 8
C_DK = 128
C_CHUNK = 64

LANES = 128
VMEM_LIMIT = 56 * 1024 * 1024

E_QKV = 0
E_GA = 1536
E_QB = 2048
E_KB = 2560
E_VB = 2688
E_GB = 2816
E_AB = 3328
E_TOT = 3456

NT = (((1,), (1,)), ((), ()))
TN = (((0,), (0,)), ((), ()))


def _mm(a, b):
    return jnp.dot(a.astype(BF16), b.astype(BF16), preferred_element_type=F32)


def _mm_nt(a, b):
    return lax.dot_general(a.astype(BF16), b.astype(BF16), NT, preferred_element_type=F32)


def _mm_tn(a, b):
    return lax.dot_general(a.astype(BF16), b.astype(BF16), TN, preferred_element_type=F32)


def _sigmoid(x):
    return 1.0 / (1.0 + jnp.exp(-x))


def _silu(x):
    return x * _sigmoid(x)


def _cumsum_rows(x):
    n = x.shape[0]
    row = lax.broadcasted_iota(jnp.int32, x.shape, 0)
    s = 1
    while s < n:
        x = x + jnp.where(row >= s, pltpu.roll(x, s, axis=0), 0.0)
        s *= 2
    return x


def _params(n_grid):
    return pltpu.CompilerParams(dimension_semantics=("arbitrary",) * n_grid,
                                vmem_limit_bytes=VMEM_LIMIT)


def _const_spec(shape):
    nd = len(shape)
    return pl.BlockSpec(shape, lambda *_: (0,) * nd)


def _even_in_body(x_ref, w_ref, cw_ref, misc_ref, rot_ref, buf_ref,
                  qkv_ref, gb_ref, ga_ref, qb_ref, kvb_ref, gbt_ref, conv_ref, xp_ref,
                  *, nb, tl):
    t = pl.program_id(1)
    r = nb * tl
    x = x_ref[...].reshape(r, D_MODEL).astype(BF16)

    z = jnp.dot(x, w_ref[:, E_QKV:E_QKV + A_CONV_CH], preferred_element_type=F32)
    if buf_ref is not None:
        xp_ref[:, 5:8, :] = buf_ref[...]
    else:
        @pl.when(t == 0)
        def _():
            xp_ref[:, 5:8, :] = jnp.zeros((nb, 3, A_CONV_CH), F32)
    xp_ref[:, 8:8 + tl, :] = z.reshape(nb, tl, A_CONV_CH)
    y = xp_ref[:, 5:5 + tl, :] * cw_ref[0:1, :].reshape(1, 1, A_CONV_CH)
    for j in range(1, A_CONV):
        y = y + xp_ref[:, 5 + j:5 + j + tl, :] * cw_ref[j:j + 1, :].reshape(1, 1, A_CONV_CH)
    last = xp_ref[:, tl + 5:tl + 8, :]
    conv_ref[...] = last
    if buf_ref is None:
        xp_ref[:, 5:8, :] = last
    y = _silu(y).reshape(r, A_CONV_CH)
    for h in range(2 * A_HEADS):
        c = y[:, h * A_DK:(h + 1) * A_DK]
        inv = lax.rsqrt(jnp.sum(c * c, axis=-1, keepdims=True) + NORM_EPS)
        if h < A_HEADS:
            inv = inv * (A_DK ** -0.5)
        qkv_ref[:, :, h * A_DK:(h + 1) * A_DK] = (c * inv).reshape(nb, tl, A_DK)
    qkv_ref[:, :, 2 * A_QK:] = y[:, 2 * A_QK:].reshape(nb, tl, A_VW)

    ab = jnp.dot(x, w_ref[:, E_AB:E_AB + LANES], preferred_element_type=F32)
    neg_a = -jnp.exp(misc_ref[0:1, :])
    sp_in = ab + misc_ref[1:2, :]
    softplus = jnp.maximum(sp_in, 0.0) + jnp.log1p(jnp.exp(-jnp.abs(sp_in)))
    lane = lax.broadcasted_iota(jnp.int32, ab.shape, 1)
    gb = jnp.where(lane < A_HEADS, neg_a * softplus, _sigmoid(ab))
    gb_ref[...] = gb.reshape(nb, tl, LANES)

    ga_ref[...] = _silu(jnp.dot(x, w_ref[:, E_GA:E_GA + A_VW],
                                preferred_element_type=F32)).reshape(nb, tl, A_VW)
    gbt_ref[...] = _silu(jnp.dot(x, w_ref[:, E_GB:E_GB + B_QW],
                                 preferred_element_type=F32)).reshape(nb, tl, B_QW)

    cos_t = rot_ref[0]
    s_up = rot_ref[1]
    s_dn = rot_ref[2]
    zq = jnp.dot(x, w_ref[:, E_QB:E_QB + B_QW], preferred_element_type=F32)
    rep = B_QW // LANES
    cos4 = jnp.concatenate([cos_t] * rep, axis=1)
    up4 = jnp.concatenate([s_up] * rep, axis=1)
    dn4 = jnp.concatenate([s_dn] * rep, axis=1)
    half = ROT_DIM // 2
    qrot = zq * cos4 + pltpu.roll(zq, B_QW - half, axis=1) * up4 + pltpu.roll(zq, half, axis=1) * dn4
    qb_ref[...] = qrot.reshape(nb, tl, B_QW)
    zk = jnp.dot(x, w_ref[:, E_KB:E_KB + B_KVW], preferred_element_type=F32)
    krot = zk * cos_t + pltpu.roll(zk, LANES - half, axis=1) * s_up + pltpu.roll(zk, half, axis=1) * s_dn
    kvb_ref[:, :, 0:B_KVW] = krot.reshape(nb, tl, B_KVW)
    zv = jnp.dot(x, w_ref[:, E_VB:E_VB + B_KVW], preferred_element_type=F32)
    kvb_ref[:, :, B_KVW:] = zv.reshape(nb, tl, B_KVW)


def _even_in_kernel_buf(x_ref, w_ref, cw_ref, misc_ref, rot_ref, buf_ref, *rest, nb, tl):
    _even_in_body(x_ref, w_ref, cw_ref, misc_ref, rot_ref, buf_ref, *rest, nb=nb, tl=tl)


def _even_in_kernel_nobuf(x_ref, w_ref, cw_ref, misc_ref, rot_ref, *rest, nb, tl):
    _even_in_body(x_ref, w_ref, cw_ref, misc_ref, rot_ref, None, *rest, nb=nb, tl=tl)


def _even_in(x, w, cw, misc, rot, buf, *, nb, tl):
    n, l, _ = x.shape
    grid = (n // nb, l // tl)
    r = nb * tl
    row_map = lambda i, t: (i, t, 0)
    rot_map = (lambda i, t: (0, t, 0)) if buf is None else (lambda i, t: (0, 0, 0))
    in_specs = [
        pl.BlockSpec((nb, tl, D_MODEL), row_map),
        _const_spec((D_MODEL, E_TOT)),
        _const_spec((A_CONV, A_CONV_CH)),
        _const_spec((8, LANES)),
        pl.BlockSpec((3, r, LANES), rot_map),
    ]
    args = [x, w, cw, misc, rot]
    if buf is not None:
        in_specs.append(pl.BlockSpec((nb, A_CONV - 1, A_CONV_CH), lambda i, t: (i, 0, 0)))
        args.append(buf)
        body = functools.partial(_even_in_kernel_buf, nb=nb, tl=tl)
    else:
        body = functools.partial(_even_in_kernel_nobuf, nb=nb, tl=tl)
    widths = (A_CONV_CH, LANES, A_VW, B_QW, 2 * B_KVW, B_QW)
    out_shape = [jax.ShapeDtypeStruct((n, l, wd), F32) for wd in widths]
    out_specs = [pl.BlockSpec((nb, tl, wd), row_map) for wd in widths]
    out_shape.append(jax.ShapeDtypeStruct((n, A_CONV - 1, A_CONV_CH), F32))
    out_specs.append(pl.BlockSpec((nb, A_CONV - 1, A_CONV_CH), lambda i, t: (i, 0, 0)))
    return pl.pallas_call(
        body, grid=grid, in_specs=in_specs, out_specs=out_specs, out_shape=out_shape,
        scratch_shapes=[pltpu.VMEM((nb, tl + 8, A_CONV_CH), F32)],
        compiler_params=_params(2), name="even_in",
    )(*args)


def _tri_inv_all(a_list, c):
    ri = lax.broadcasted_iota(jnp.int32, (c, LANES), 0)
    ci = lax.broadcasted_iota(jnp.int32, (c, LANES), 1)
    eye = jnp.where(ri == ci, 1.0, 0.0)
    ps = [eye for _ in a_list]
    ms = [-a for a in a_list]
    span = 1
    while span < c:
        mps = [_mm(m[:, :c], jnp.concatenate([m, p], axis=1)) for m, p in zip(ms, ps)]
        ms = [mp[:, :LANES] for mp in mps]
        ps = [p + mp[:, LANES:] for p, mp in zip(ps, mps)]
        span *= 2
    return ps


def _delta_prep_all(qs, ks, vs, g_cols, g_rows, betas, g_lasts, c):
    ri = lax.broadcasted_iota(jnp.int32, (c, LANES), 0)
    ci = lax.broadcasted_iota(jnp.int32, (c, LANES), 1)
    causal = ri >= ci
    strict = ri > ci
    decays = [jnp.where(causal, jnp.exp(jnp.where(causal, gc - gr, 0.0)), 0.0) for gc, gr in zip(g_cols, g_rows)]
    kbs = [k * b for k, b in zip(ks, betas)]
    if c < LANES:
        pad = jnp.zeros((LANES - c, A_DK), F32)
        k_rows = [jnp.concatenate([k, pad], axis=0).astype(BF16) for k in ks]
    else:
        k_rows = [k.astype(BF16) for k in ks]
    a_list = [jnp.where(strict, _mm_nt(kb, kr) * d, 0.0) for kb, kr, d in zip(kbs, k_rows, decays)]
    qks = [(_mm_nt(q, kr) * d)[:, :c].astype(BF16) for q, kr, d in zip(qs, k_rows, decays)]
    t_invs = _tri_inv_all(a_list, c)
    e_gs = [jnp.exp(gc) for gc in g_cols]
    uws = [_mm(t[:, :c], jnp.concatenate([v * b, kb * e], axis=1))
           for t, v, b, kb, e in zip(t_invs, vs, betas, kbs, e_gs)]
    us = [uw[:, :A_DK] for uw in uws]
    wqs = [jnp.concatenate([uw[:, A_DK:], q * e], axis=0).astype(BF16) for uw, q, e in zip(uws, qs, e_gs)]
    k_decs = [(k * jnp.exp(gl - gc)).astype(BF16) for k, gl, gc in zip(ks, g_lasts, g_cols)]
    e_lasts = [jnp.exp(gl) for gl in g_lasts]
    return us, wqs, qks, k_decs, e_lasts


def _delta_body(qkv_ref, gb_ref, s0_ref, o_ref, sout_ref, s_ref, *, nb, tl, c):
    t = pl.program_id(1)
    nt = pl.num_programs(1)

    @pl.when(t == 0)
    def _():
        if s0_ref is None:
            s_ref[...] = jnp.zeros(s_ref.shape, F32)
        else:
            s_ref[...] = s0_ref[...]

    nc = tl // c
    qs, ks, vs, g_cols, g_rows, betas, g_lasts = [], [], [], [], [], [], []
    for ic in range(nc):
        rows = slice(ic * c, (ic + 1) * c)
        for b in range(nb):
            gbc = gb_ref[b, rows, :]
            g_all = _cumsum_rows(gbc)
            if c < LANES:
                g_t = jnp.concatenate([g_all, jnp.zeros((LANES - c, LANES), F32)], axis=0).T
            else:
                g_t = g_all.T
            for h in range(A_HEADS):
                qs.append(qkv_ref[b, rows, h * A_DK:(h + 1) * A_DK])
                ks.append(qkv_ref[b, rows, A_QK + h * A_DK:A_QK + (h + 1) * A_DK])
                vs.append(qkv_ref[b, rows, 2 * A_QK + h * A_DK:2 * A_QK + (h + 1) * A_DK])
                g_cols.append(g_all[:, h:h + 1])
                g_rows.append(g_t[h:h + 1, :])
                betas.append(gbc[:, A_HEADS + h:A_HEADS + h + 1])
                g_lasts.append(g_all[c - 1:c, h:h + 1])
    us, wqs, qks, k_decs, e_lasts = _delta_prep_all(qs, ks, vs, g_cols, g_rows, betas, g_lasts, c)

    nch = nb * A_HEADS
    states = [s_ref[b, h] for b in range(nb) for h in range(A_HEADS)]
    for ic in range(nc):
        sl = slice(ic * nch, (ic + 1) * nch)
        ws_qs = [_mm(wq, s) for wq, s in zip(wqs[sl], states)]
        v_news = [(u - x[:c]).astype(BF16) for u, x in zip(us[sl], ws_qs)]
        outs = [x[c:] + _mm(qk, vn) for x, qk, vn in zip(ws_qs, qks[sl], v_news)]
        states = [s * el + _mm_tn(kd, vn) for s, el, kd, vn in zip(states, e_lasts[sl], k_decs[sl], v_news)]
        for i, o in enumerate(outs):
            b, h = divmod(i, A_HEADS)
            o_ref[b, ic * c:(ic + 1) * c, h * A_DK:(h + 1) * A_DK] = o
    for i, s in enumerate(states):
        b, h = divmod(i, A_HEADS)
        s_ref[b, h] = s

    @pl.when(t == nt - 1)
    def _():
        sout_ref[...] = s_ref[...]


def _delta_kernel_s0(qkv_ref, gb_ref, s0_ref, o_ref, sout_ref, s_ref, *, nb, tl, c):
    _delta_body(qkv_ref, gb_ref, s0_ref, o_ref, sout_ref, s_ref, nb=nb, tl=tl, c=c)


def _delta_kernel_zero(qkv_ref, gb_ref, o_ref, sout_ref, s_ref, *, nb, tl, c):
    _delta_body(qkv_ref, gb_ref, None, o_ref, sout_ref, s_ref, nb=nb, tl=tl, c=c)


def _delta(qkv, gb, s0, *, nb, tl, c):
    n, l, _ = qkv.shape
    grid = (n // nb, l // tl)
    row_map = lambda i, t: (i, t, 0)
    st_map = lambda i, t: (i, 0, 0, 0)
    st_block = (nb, A_HEADS, A_DK, A_DK)
    in_specs = [pl.BlockSpec((nb, tl, A_CONV_CH), row_map), pl.BlockSpec((nb, tl, LANES), row_map)]
    args = [qkv, gb]
    if s0 is not None:
        in_specs.append(pl.BlockSpec(st_block, st_map))
        args.append(s0)
        body = functools.partial(_delta_kernel_s0, nb=nb, tl=tl, c=c)
    else:
        body = functools.partial(_delta_kernel_zero, nb=nb, tl=tl, c=c)
    return pl.pallas_call(
        body, grid=grid, in_specs=in_specs,
        out_specs=[pl.BlockSpec((nb, tl, A_VW), row_map), pl.BlockSpec(st_block, st_map)],
        out_shape=[jax.ShapeDtypeStruct((n, l, A_VW), F32),
                   jax.ShapeDtypeStruct((n, A_HEADS, A_DK, A_DK), F32)],
        scratch_shapes=[pltpu.VMEM(st_block, F32)],
        compiler_params=_params(2), name="delta",
    )(*args)


def _swa_block(q, kx, vx, sink_ref, min_key, o_store):
    nq = q.shape[0]
    nk = 2 * WINDOW
    kxr = pltpu.roll(kx, B_HD, axis=1)
    vxr = pltpu.roll(vx, B_HD, axis=1)
    lo = lax.broadcasted_iota(jnp.int32, (nk, LANES), 1) < B_HD
    ri = lax.broadcasted_iota(jnp.int32, (nq, nk), 0)
    ci = lax.broadcasted_iota(jnp.int32, (nq, nk), 1)
    mask = (ci >= jnp.maximum(ri, min_key)) & (ci <= ri + WINDOW)
    scale = B_HD ** -0.5
    for g in range(B_KV_HEADS):
        if g == 0:
            kbd = jnp.concatenate([jnp.where(lo, kx, 0.0), jnp.where(lo, 0.0, kxr)], axis=0)
            vbd = jnp.concatenate([jnp.where(lo, vx, 0.0), jnp.where(lo, 0.0, vxr)], axis=0)
        else:
            kbd = jnp.concatenate([jnp.where(lo, kxr, 0.0), jnp.where(lo, 0.0, kx)], axis=0)
            vbd = jnp.concatenate([jnp.where(lo, vxr, 0.0), jnp.where(lo, 0.0, vx)], axis=0)
        kbd = kbd.astype(BF16)
        vbd = vbd.astype(BF16)
        for jj in range(2):
            j = 2 * g + jj
            s = _mm_nt(q[:, j * LANES:(j + 1) * LANES], kbd) * scale
            probs = []
            for e in range(2):
                sink = sink_ref[2 * j + e]
                se = jnp.where(mask, s[:, e * nk:(e + 1) * nk], MASK_NEG)
                m = jnp.maximum(jnp.max(se, axis=-1, keepdims=True), sink)
                p = jnp.where(mask, jnp.exp(se - m), 0.0)
                denom = jnp.sum(p, axis=-1, keepdims=True) + jnp.exp(sink - m)
                probs.append(p / denom)
            o_store(j, _mm(jnp.concatenate(probs, axis=1), vbd))


def _swa_prompt_kernel(sink_ref, q_ref, kvc_ref, kvp_ref, o_ref):
    i = pl.program_id(1)
    kvc = kvc_ref[0]
    kvp = kvp_ref[0]
    kx = jnp.concatenate([kvp[:, :B_KVW], kvc[:, :B_KVW]], axis=0)
    vx = jnp.concatenate([kvp[:, B_KVW:], kvc[:, B_KVW:]], axis=0)

    def store(j, val):
        o_ref[0, :, j * LANES:(j + 1) * LANES] = val

    _swa_block(q_ref[0], kx, vx, sink_ref, jnp.where(i > 0, 0, WINDOW), store)


def _swa_prompt(qb, kvb, sinks):
    n, l, _ = qb.shape
    nblk = l // WINDOW
    return pl.pallas_call(
        _swa_prompt_kernel, grid=(n, nblk),
        in_specs=[pl.BlockSpec(memory_space=pltpu.SMEM),
                  pl.BlockSpec((1, WINDOW, B_QW), lambda b, i: (b, i, 0)),
                  pl.BlockSpec((1, WINDOW, 2 * B_KVW), lambda b, i: (b, i, 0)),
                  pl.BlockSpec((1, WINDOW, 2 * B_KVW), lambda b, i: (b, jnp.maximum(i - 1, 0), 0))],
        out_specs=pl.BlockSpec((1, WINDOW, B_QW), lambda b, i: (b, i, 0)),
        out_shape=jax.ShapeDtypeStruct((n, l, B_QW), F32),
        compiler_params=_params(2), name="swa_prompt",
    )(sinks, qb, kvb, kvb)


def _swa_sample_kernel(sink_ref, q_ref, kv_ref, ck_ref, cv_ref, o_ref, nk_ref, nv_ref, *, nb, ls):
    pad = jnp.zeros((WINDOW - ls, B_KVW), F32)
    for b in range(nb):
        kv = kv_ref[b]
        k_new = kv[:, :B_KVW]
        v_new = kv[:, B_KVW:]
        ck = ck_ref[b]
        cv = cv_ref[b]
        kx = jnp.concatenate([ck, k_new, pad], axis=0)
        vx = jnp.concatenate([cv, v_new, pad], axis=0)

        def store(j, val, b=b):
            o_ref[b, :, j * LANES:(j + 1) * LANES] = val

        _swa_block(q_ref[b], kx, vx, sink_ref, 0, store)
        nk_ref[b, 0:WINDOW - ls, :] = ck[ls:, :]
        nk_ref[b, WINDOW - ls:, :] = k_new
        nv_ref[b, 0:WINDOW - ls, :] = cv[ls:, :]
        nv_ref[b, WINDOW - ls:, :] = v_new


def _swa_sample(qb, kvb, ck, cv, sinks, *, nb):
    n, ls, _ = qb.shape
    row_map = lambda i: (i, 0, 0)
    cache_spec = pl.BlockSpec((nb, WINDOW, B_KVW), row_map)
    cache_shape = jax.ShapeDtypeStruct((n, WINDOW, B_KVW), F32)
    return pl.pallas_call(
        functools.partial(_swa_sample_kernel, nb=nb, ls=ls), grid=(n // nb,),
        in_specs=[pl.BlockSpec(memory_space=pltpu.SMEM),
                  pl.BlockSpec((nb, ls, B_QW), row_map),
                  pl.BlockSpec((nb, ls, 2 * B_KVW), row_map),
                  cache_spec, cache_spec],
        out_specs=[pl.BlockSpec((nb, ls, B_QW), row_map), cache_spec, cache_spec],
        out_shape=[jax.ShapeDtypeStruct((n, ls, B_QW), F32), cache_shape, cache_shape],
        compiler_params=_params(1), name="swa_sample",
    )(sinks, qb, kvb, ck, cv)


def _odd_in_kernel(x_ref, w_ref, lbraw_ref, q_ref, k_ref, v_ref, lf_ref, g_ref, *, j):
    x = x_ref[...].astype(BF16)
    raw = lbraw_ref[...]
    e = jnp.exp(raw - jnp.max(raw, axis=0, keepdims=True))
    sm = e / jnp.sum(e, axis=0, keepdims=True)
    lb = jnp.zeros((1, D_MODEL), F32)
    for i in range(1, j + 1):
        lb = lb + sm[i:i + 1, :]
    d = D_MODEL
    q_ref[...] = _silu(jnp.dot(x, w_ref[:, 0:d], preferred_element_type=F32)) * (C_DK ** -0.5)
    zf = jnp.dot(x, w_ref[:, d:2 * d], preferred_element_type=F32)
    fg = lb + (1.0 - lb) * _sigmoid(zf)
    lf_ref[...] = jnp.log(jnp.maximum(fg, F_FLOOR))
    k_ref[...] = (1.0 - lb) * _sigmoid(-zf)
    v_ref[...] = jnp.dot(x, w_ref[:, 2 * d:3 * d], preferred_element_type=F32)
    g_ref[...] = _silu(jnp.dot(x, w_ref[:, 3 * d:4 * d], preferred_element_type=F32))


def _odd_in(x2, w, lb_raw, j, *, tr):
    rows = x2.shape[0]
    n_odd = lb_raw.shape[0]
    row_spec = pl.BlockSpec((tr, D_MODEL), lambda i: (i, 0))
    shp = jax.ShapeDtypeStruct((rows, D_MODEL), F32)
    return pl.pallas_call(
        functools.partial(_odd_in_kernel, j=j), grid=(rows // tr,),
        in_specs=[row_spec, _const_spec((D_MODEL, 4 * D_MODEL)), _const_spec((n_odd, D_MODEL))],
        out_specs=[row_spec] * 5, out_shape=[shp] * 5,
        compiler_params=_params(1), name="odd_in",
    )(x2, w, lb_raw)


def _hgrn_prep_all(qs, ks, vs, lfs, c):
    gs = [_cumsum_rows(lf) for lf in lfs]
    ri = lax.broadcasted_iota(jnp.int32, (c, c), 0)
    ci = lax.broadcasted_iota(jnp.int32, (c, c), 1)
    row = lax.broadcasted_iota(jnp.int32, (c, C_DK), 0)
    diag = ri == ci
    accs = [jnp.where(diag, _mm_nt(q, k), 0.0) for q, k in zip(qs, ks)]
    bs = list(gs)
    h = 1
    while h < c:
        bns = [pltpu.roll(b, c - h, axis=0) for b in bs]
        qhs = qs if h == 1 else [q * jnp.exp(g - b) for q, g, b in zip(qs, gs, bs)]
        khs = [k * jnp.exp(bn - g) for k, bn, g in zip(ks, bns, gs)]
        lvl = ((ri // h) == (ci // h) + 1) & ((ri // (2 * h)) == (ci // (2 * h)))
        accs = [a + jnp.where(lvl, _mm_nt(qh, kh), 0.0) for a, qh, kh in zip(accs, qhs, khs)]
        sel = (row & h) != 0
        bs = [jnp.where(sel, pltpu.roll(b, h, axis=0), b) for b in bs]
        h *= 2
    g_lasts = [g[c - 1:c, :] for g in gs]
    intra = [_mm(a, v) for a, v in zip(accs, vs)]
    q_decs = [(q * jnp.exp(g)).astype(BF16) for q, g in zip(qs, gs)]
    kvs = [_mm_tn(v, k * jnp.exp(gl - g)) for v, k, gl, g in zip(vs, ks, g_lasts, gs)]
    e_lasts = [jnp.exp(gl) for gl in g_lasts]
    return intra, q_decs, kvs, e_lasts


def _hgrn_body(q_ref, k_ref, v_ref, lf_ref, s0_ref, o_ref, sout_ref, st_ref, *, nb, hb, tl, c):
    t = pl.program_id(2)
    nt = pl.num_programs(2)
    nc = tl // c

    @pl.when(t == 0)
    def _():
        if s0_ref is None:
            st_ref[...] = jnp.zeros(st_ref.shape, F32)
        else:
            for b in range(nb):
                for h in range(hb):
                    st_ref[b, h] = s0_ref[b, h].T

    where = [(ic, b, h) for ic in range(nc) for b in range(nb) for h in range(hb)]

    def load(ref):
        return [ref[b, ic * c:(ic + 1) * c, h * C_DK:(h + 1) * C_DK] for ic, b, h in where]

    intra, q_decs, kvs, e_lasts = _hgrn_prep_all(load(q_ref), load(k_ref), load(v_ref), load(lf_ref), c)
    nch = nb * hb
    states = [st_ref[b, h] for b in range(nb) for h in range(hb)]
    for ic in range(nc):
        sl = slice(ic * nch, (ic + 1) * nch)
        outs = [_mm_nt(qd, st) + x for qd, st, x in zip(q_decs[sl], states, intra[sl])]
        states = [st * el + kv for st, el, kv in zip(states, e_lasts[sl], kvs[sl])]
        for i, o in enumerate(outs):
            b, h = divmod(i, hb)
            o_ref[b, ic * c:(ic + 1) * c, h * C_DK:(h + 1) * C_DK] = o
    for i, st in enumerate(states):
        b, h = divmod(i, hb)
        st_ref[b, h] = st

    @pl.when(t == nt - 1)
    def _():
        for b in range(nb):
            for h in range(hb):
                sout_ref[b, h] = st_ref[b, h].T


def _hgrn_kernel_s0(q_ref, k_ref, v_ref, lf_ref, s0_ref, o_ref, sout_ref, st_ref, **kw):
    _hgrn_body(q_ref, k_ref, v_ref, lf_ref, s0_ref, o_ref, sout_ref, st_ref, **kw)


def _hgrn_kernel_zero(q_ref, k_ref, v_ref, lf_ref, o_ref, sout_ref, st_ref, **kw):
    _hgrn_body(q_ref, k_ref, v_ref, lf_ref, None, o_ref, sout_ref, st_ref, **kw)


def _hgrn(q, k, v, lf, s0, *, nb, hb, tl, c):
    n, l, _ = q.shape
    grid = (n // nb, C_HEADS // hb, l // tl)
    row_spec = pl.BlockSpec((nb, tl, hb * C_DK), lambda i, g, t: (i, t, g))
    st_block = (nb, hb, C_DK, C_DK)
    st_spec = pl.BlockSpec(st_block, lambda i, g, t: (i, g, 0, 0))
    in_specs = [row_spec] * 4
    args = [q, k, v, lf]
    kw = dict(nb=nb, hb=hb, tl=tl, c=c)
    if s0 is not None:
        in_specs = in_specs + [st_spec]
        args.append(s0)
        body = functools.partial(_hgrn_kernel_s0, **kw)
    else:
        body = functools.partial(_hgrn_kernel_zero, **kw)
    return pl.pallas_call(
        body, grid=grid, in_specs=in_specs, out_specs=[row_spec, st_spec],
        out_shape=[jax.ShapeDtypeStruct((n, l, D_MODEL), F32),
                   jax.ShapeDtypeStruct((n, C_HEADS, C_DK, C_DK), F32)],
        scratch_shapes=[pltpu.VMEM(st_block, F32)],
        compiler_params=_params(3), name="hgrn",
    )(*args)


def _group_rms(o, g_row, heads):
    outs = []
    for h in range(heads):
        c = o[:, h * LANES:(h + 1) * LANES]
        inv = lax.rsqrt(jnp.mean(c * c, axis=-1, keepdims=True) + NORM_EPS)
        outs.append(c * inv * g_row)
    return jnp.concatenate(outs, axis=1)


def _post_tail(x, y, p_ref, lng_ref, lnb_ref, wproj_ref, wgate_ref, out_ref):
    pre = DEEPNORM_ALPHA * x + y
    mu = jnp.mean(pre, axis=-1, keepdims=True)
    xc = pre - mu
    var = jnp.mean(xc * xc, axis=-1, keepdims=True)
    hn = xc * lax.rsqrt(var + NORM_EPS) * lng_ref[...] + lnb_ref[...]
    gate = _sigmoid(jnp.dot(hn.astype(BF16), wgate_ref[...], preferred_element_type=F32))
    emb = jnp.dot(p_ref[...].astype(BF16), wproj_ref[...], preferred_element_type=F32)
    out_ref[...] = hn + gate * emb


def _post_even_kernel(x_ref, oa_ref, ga_ref, ob_ref, gbt_ref, p_ref, wout_ref, ng_ref, lng_ref, lnb_ref,
                      wproj_ref, wgate_ref, out_ref):
    a = _group_rms(oa_ref[...], ng_ref[...], A_HEADS) * ga_ref[...]
    b = ob_ref[...] * gbt_ref[...]
    y = (jnp.dot(a.astype(BF16), wout_ref[0:A_VW, :], preferred_element_type=F32)
         + jnp.dot(b.astype(BF16), wout_ref[A_VW:, :], preferred_element_type=F32))
    _post_tail(x_ref[...], y, p_ref, lng_ref, lnb_ref, wproj_ref, wgate_ref, out_ref)


def _post_odd_kernel(x_ref, o_ref, g_ref, p_ref, wout_ref, ng_ref, lng_ref, lnb_ref,
                     wproj_ref, wgate_ref, out_ref):
    a = _group_rms(o_ref[...], ng_ref[...], C_HEADS) * g_ref[...]
    y = jnp.dot(a.astype(BF16), wout_ref[...], preferred_element_type=F32)
    _post_tail(x_ref[...], y, p_ref, lng_ref, lnb_ref, wproj_ref, wgate_ref, out_ref)


def _post(x2, acts, p2, wout, ng, lng, lnb, wproj, wgate, *, even, tr):
    rows = x2.shape[0]
    full = pl.BlockSpec((tr, D_MODEL), lambda i: (i, 0))
    half = pl.BlockSpec((tr, A_VW), lambda i: (i, 0))
    act_specs = [half] * 4 if even else [full] * 2
    in_specs = ([full] + act_specs + [pl.BlockSpec((tr, PLE_DIM), lambda i: (i, 0)),
                                      _const_spec((D_MODEL, D_MODEL)), _const_spec((1, LANES)),
                                      _const_spec((1, D_MODEL)), _const_spec((1, D_MODEL)),
                                      _const_spec((PLE_DIM, D_MODEL)), _const_spec((D_MODEL, D_MODEL))])
    return pl.pallas_call(
        _post_even_kernel if even else _post_odd_kernel, grid=(rows // tr,),
        in_specs=in_specs, out_specs=full, out_shape=jax.ShapeDtypeStruct((rows, D_MODEL), F32),
        compiler_params=_params(1), name="post_even" if even else "post_odd",
    )(x2, *acts, p2, wout, ng, lng, lnb, wproj, wgate)


def _pack_even_w(w):
    qkv = w[:, 0:1536]
    ga = w[:, 1536:2048]
    ab = w[:, 2048:2056]
    qb = w[:, 2056:2568]
    kb = w[:, 2568:2696]
    vb = w[:, 2696:2824]
    gbt = w[:, 2824:3336]
    ab = jnp.pad(ab, ((0, 0), (0, LANES - 2 * A_HEADS)))
    return jnp.concatenate([qkv, ga, qb, kb, vb, gbt, ab], axis=1).astype(BF16)


def _rot_tables(pos, reps):
    inv = ROPE_THETA ** (-jnp.arange(0, ROT_DIM, 2, dtype=F32) / ROT_DIM)
    ang = pos.astype(F32)[:, None] * inv[None, :]
    cos = jnp.cos(ang)
    sin = jnp.sin(ang)
    n = pos.shape[0]
    half = ROT_DIM // 2
    cos64 = jnp.concatenate([cos, cos, jnp.ones((n, B_HD - ROT_DIM), F32)], axis=1)
    up64 = jnp.concatenate([-sin, jnp.zeros((n, B_HD - half), F32)], axis=1)
    dn64 = jnp.concatenate([jnp.zeros((n, half), F32), sin, jnp.zeros((n, B_HD - ROT_DIM), F32)], axis=1)
    tabs = jnp.stack([jnp.tile(t, (reps, LANES // B_HD)) for t in (cos64, up64, dn64)])
    return tabs


def kernel(x_prompt, x_sample, state_conv_a, state_delta_a, cache_win_k, cache_win_v, state_hgrn_c,
           p_prompt, p_sample, w_in_even, conv_w_a, a_log, dt_bias, norm_a, sinks_b, w_out_even,
           w_in_odd, lb_raw, norm_c, w_out_odd, ln_g, ln_b, w_ple_proj, w_ple_gate):
    n_p, l_p, _ = x_prompt.shape
    n_s, l_s, _ = x_sample.shape
    nb_s = 32
    rot_p = _rot_tables(jnp.arange(l_p), 1)
    rot_s = _rot_tables(PAST_LEN + jnp.arange(l_s), nb_s)
    conv_p, conv_s, delta_p, delta_s = [], [], [], []
    wk_p, wk_s, wv_p, wv_s, hg_p, hg_s = [], [], [], [], [], []
    hp, hs = x_prompt, x_sample
    for layer in range(DEPTH):
        j = layer // 2
        lng = ln_g[layer].reshape(1, D_MODEL)
        lnb = ln_b[layer].reshape(1, D_MODEL)
        wproj = w_ple_proj[layer].astype(BF16)
        wgate = w_ple_gate[layer].astype(BF16)
        hp2 = hp.reshape(n_p * l_p, D_MODEL)
        hs2 = hs.reshape(n_s * l_s, D_MODEL)
        pp2 = p_prompt[layer].reshape(n_p * l_p, PLE_DIM)
        ps2 = p_sample[layer].reshape(n_s * l_s, PLE_DIM)
        if layer % 2 == 0:
            w = _pack_even_w(w_in_even[j])
            misc = jnp.zeros((8, LANES), F32)
            misc = misc.at[0, :A_HEADS].set(a_log[j]).at[1, :A_HEADS].set(dt_bias[j])
            cw = conv_w_a[j]
            sinks = sinks_b[j]
            wout = w_out_even[j].astype(BF16)
            ng = norm_a[j].reshape(1, LANES)
            qkv, gb, ga, qb, kvb, gbt, c1 = _even_in(hp, w, cw, misc, rot_p, None, nb=1, tl=256)
            oa, s1 = _delta(qkv, gb, None, nb=1, tl=256, c=A_CHUNK)
            ob = _swa_prompt(qb, kvb, sinks)
            acts = [t.reshape(n_p * l_p, A_VW) for t in (oa, ga, ob, gbt)]
            hp = _post(hp2, acts, pp2, wout, ng, lng, lnb, wproj, wgate, even=True, tr=512)
            hp = hp.reshape(n_p, l_p, D_MODEL)
            k1 = kvb[:, l_p - WINDOW:, :B_KVW].reshape(n_p, WINDOW, B_KV_HEADS, B_HD)
            v1 = kvb[:, l_p - WINDOW:, B_KVW:].reshape(n_p, WINDOW, B_KV_HEADS, B_HD)
            qkv, gb, ga, qb, kvb, gbt, c2 = _even_in(hs, w, cw, misc, rot_s, state_conv_a[j], nb=nb_s, tl=l_s)
            oa, s2 = _delta(qkv, gb, state_delta_a[j], nb=8, tl=l_s, c=l_s)
            ck = cache_win_k[j].reshape(n_s, WINDOW, B_KVW)
            cv = cache_win_v[j].reshape(n_s, WINDOW, B_KVW)
            ob, k2, v2 = _swa_sample(qb, kvb, ck, cv, sinks, nb=8)
            acts = [t.reshape(n_s * l_s, A_VW) for t in (oa, ga, ob, gbt)]
            hs = _post(hs2, acts, ps2, wout, ng, lng, lnb, wproj, wgate, even=True, tr=512)
            hs = hs.reshape(n_s, l_s, D_MODEL)
            conv_p.append(c1); conv_s.append(c2)
            delta_p.append(s1); delta_s.append(s2)
            wk_p.append(k1); wv_p.append(v1)
            wk_s.append(k2.reshape(n_s, WINDOW, B_KV_HEADS, B_HD))
            wv_s.append(v2.reshape(n_s, WINDOW, B_KV_HEADS, B_HD))
        else:
            w = w_in_odd[j].astype(BF16)
            wout = w_out_odd[j].astype(BF16)
            ng = norm_c[j].reshape(1, LANES)
            q, k, v, lf, g = _odd_in(hp2, w, lb_raw, j, tr=256)
            shp = (n_p, l_p, D_MODEL)
            o, s1 = _hgrn(q.reshape(shp), k.reshape(shp), v.reshape(shp), lf.reshape(shp), None,
                          nb=1, hb=4, tl=256, c=C_CHUNK)
            hp = _post(hp2, [o.reshape(n_p * l_p, D_MODEL), g], pp2, wout, ng, lng, lnb, wproj, wgate,
                       even=False, tr=512)
            hp = hp.reshape(n_p, l_p, D_MODEL)
            q, k, v, lf, g = _odd_in(hs2, w, lb_raw, j, tr=256)
            shp = (n_s, l_s, D_MODEL)
            o, s2 = _hgrn(q.reshape(shp), k.reshape(shp), v.reshape(shp), lf.reshape(shp), state_hgrn_c[j],
                          nb=4, hb=4, tl=l_s, c=l_s)
            hs = _post(hs2, [o.reshape(n_s * l_s, D_MODEL), g], ps2, wout, ng, lng, lnb, wproj, wgate,
                       even=False, tr=512)
            hs = hs.reshape(n_s, l_s, D_MODEL)
            hg_p.append(s1); hg_s.append(s2)
    return (hp, hs, jnp.stack(conv_p), jnp.stack(conv_s), jnp.stack(delta_p), jnp.stack(delta_s),
            jnp.stack(wk_p), jnp.stack(wk_s), jnp.stack(wv_p), jnp.stack(wv_s), jnp.stack(hg_p), jnp.stack(hg_s))
```

```python
import functools
import math

import jax
import jax.numpy as jnp
from jax import lax
from jax.experimental import pallas as pl
from jax.experimental.pallas import tpu as pltpu

F32 = jnp.float32
BF16 = jnp.bfloat16

D_MODEL = 1024
DEPTH = 4
PAST_LEN = 8192
PLE_DIM = 256
NORM_EPS = 1e-6
MASK_NEG = -1e30
F_FLOOR = 1e-30
DEEPNORM_ALPHA = (2 * DEPTH) ** 0.25

A_HEADS = 4
A_DK = 128
A_CONV = 4
A_QK = 512
A_VW = 512
A_CONV_CH = 1536
A_CHUNK = 64

B_HEADS = 8
B_KV_HEADS = 2
B_HD = 64
B_QW = 512
B_KVW = 128
WINDOW = 128
ROT_DIM = 16
ROPE_THETA = 500000.0

C_HEADS = 8
C_DK = 128
C_CHUNK = 64

LANES = 128
VMEM_LIMIT = 56 * 1024 * 1024

E_QKV = 0
E_GA = 1536
E_QB = 2048
E_KB = 2560
E_VB = 2688
E_GB = 2816
E_AB = 3328
E_TOT = 3456

NT = (((1,), (1,)), ((), ()))
TN = (((0,), (0,)), ((), ()))


def _mm(a, b):
    return jnp.dot(a.astype(BF16), b.astype(BF16), preferred_element_type=F32)


def _mm_nt(a, b):
    return lax.dot_general(a.astype(BF16), b.astype(BF16), NT, preferred_element_type=F32)


def _mm_tn(a, b):
    return lax.dot_general(a.astype(BF16), b.astype(BF16), TN, preferred_element_type=F32)


def _sigmoid(x):
    return 1.0 / (1.0 + jnp.exp(-x))


def _silu(x):
    return x * _sigmoid(x)


def _cumsum_rows(x):
    n = x.shape[0]
    row = lax.broadcasted_iota(jnp.int32, x.shape, 0)
    s = 1
    while s < n:
        x = x + jnp.where(row >= s, pltpu.roll(x, s, axis=0), 0.0)
        s *= 2
    return x


def _params(n_grid):
    return pltpu.CompilerParams(dimension_semantics=("arbitrary",) * n_grid,
                                vmem_limit_bytes=VMEM_LIMIT)


def _const_spec(shape):
    nd = len(shape)
    return pl.BlockSpec(shape, lambda *_: (0,) * nd)


def _even_in_body(x_ref, w_ref, cw_ref, misc_ref, rot_ref, buf_ref,
                  qkv_ref, gb_ref, ga_ref, qb_ref, kvb_ref, gbt_ref, conv_ref, xp_ref,
                  *, nb, tl):
    t = pl.program_id(1)
    r = nb * tl
    x = x_ref[...].reshape(r, D_MODEL).astype(BF16)

    z = jnp.dot(x, w_ref[:, E_QKV:E_QKV + A_CONV_CH], preferred_element_type=F32)
    if buf_ref is not None:
        xp_ref[:, 5:8, :] = buf_ref[...]
    else:
        @pl.when(t == 0)
        def _():
            xp_ref[:, 5:8, :] = jnp.zeros((nb, 3, A_CONV_CH), F32)
    xp_ref[:, 8:8 + tl, :] = z.reshape(nb, tl, A_CONV_CH)
    y = xp_ref[:, 5:5 + tl, :] * cw_ref[0:1, :].reshape(1, 1, A_CONV_CH)
    for j in range(1, A_CONV):
        y = y + xp_ref[:, 5 + j:5 + j + tl, :] * cw_ref[j:j + 1, :].reshape(1, 1, A_CONV_CH)
    last = xp_ref[:, tl + 5:tl + 8, :]
    conv_ref[...] = last
    if buf_ref is None:
        xp_ref[:, 5:8, :] = last
    y = _silu(y).reshape(r, A_CONV_CH)
    for h in range(2 * A_HEADS):
        c = y[:, h * A_DK:(h + 1) * A_DK]
        inv = lax.rsqrt(jnp.sum(c * c, axis=-1, keepdims=True) + NORM_EPS)
        if h < A_HEADS:
            inv = inv * (A_DK ** -0.5)
        qkv_ref[:, :, h * A_DK:(h + 1) * A_DK] = (c * inv).reshape(nb, tl, A_DK)
    qkv_ref[:, :, 2 * A_QK:] = y[:, 2 * A_QK:].reshape(nb, tl, A_VW)

    ab = jnp.dot(x, w_ref[:, E_AB:E_AB + LANES], preferred_element_type=F32)
    neg_a = -jnp.exp(misc_ref[0:1, :])
    sp_in = ab + misc_ref[1:2, :]
    softplus = jnp.maximum(sp_in, 0.0) + jnp.log1p(jnp.exp(-jnp.abs(sp_in)))
    lane = lax.broadcasted_iota(jnp.int32, ab.shape, 1)
    gb = jnp.where(lane < A_HEADS, neg_a * softplus, _sigmoid(ab))
    gb_ref[...] = gb.reshape(nb, tl, LANES)

    ga_ref[...] = _silu(jnp.dot(x, w_ref[:, E_GA:E_GA + A_VW],
                                preferred_element_type=F32)).reshape(nb, tl, A_VW)
    gbt_ref[...] = _silu(jnp.dot(x, w_ref[:, E_GB:E_GB + B_QW],
                                 preferred_element_type=F32)).reshape(nb, tl, B_QW)

    cos_t = rot_ref[0]
    s_up = rot_ref[1]
    s_dn = rot_ref[2]
    zq = jnp.dot(x, w_ref[:, E_QB:E_QB + B_QW], preferred_element_type=F32)
    rep = B_QW // LANES
    cos4 = jnp.concatenate([cos_t] * rep, axis=1)
    up4 = jnp.concatenate([s_up] * rep, axis=1)
    dn4 = jnp.concatenate([s_dn] * rep, axis=1)
    half = ROT_DIM // 2
    qrot = zq * cos4 + pltpu.roll(zq, B_QW - half, axis=1) * up4 + pltpu.roll(zq, half, axis=1) * dn4
    qb_ref[...] = qrot.reshape(nb, tl, B_QW)
    zk = jnp.dot(x, w_ref[:, E_KB:E_KB + B_KVW], preferred_element_type=F32)
    krot = zk * cos_t + pltpu.roll(zk, LANES - half, axis=1) * s_up + pltpu.roll(zk, half, axis=1) * s_dn
    kvb_ref[:, :, 0:B_KVW] = krot.reshape(nb, tl, B_KVW)
    zv = jnp.dot(x, w_ref[:, E_VB:E_VB + B_KVW], preferred_element_type=F32)
    kvb_ref[:, :, B_KVW:] = zv.reshape(nb, tl, B_KVW)


def _even_in_kernel_buf(x_ref, w_ref, cw_ref, misc_ref, rot_ref, buf_ref, *rest, nb, tl):
    _even_in_body(x_ref, w_ref, cw_ref, misc_ref, rot_ref, buf_ref, *rest, nb=nb, tl=tl)


def _even_in_kernel_nobuf(x_ref, w_ref, cw_ref, misc_ref, rot_ref, *rest, nb, tl):
    _even_in_body(x_ref, w_ref, cw_ref, misc_ref, rot_ref, None, *rest, nb=nb, tl=tl)


def _even_in(x, w, cw, misc, rot, buf, j, *, nb, tl):
    n, l, _ = x.shape
    grid = (n // nb, l // tl)
    r = nb * tl
    row_map = lambda i, t: (i, t, 0)
    rot_map = (lambda i, t: (0, t, 0)) if buf is None else (lambda i, t: (0, 0, 0))
    in_specs = [
        pl.BlockSpec((nb, tl, D_MODEL), row_map),
        _const_spec((D_MODEL, E_TOT)),
        _const_spec((A_CONV, A_CONV_CH)),
        _const_spec((8, LANES)),
        pl.BlockSpec((3, r, LANES), rot_map),
    ]
    args = [x, w, cw, misc, rot]
    if buf is not None:
        in_specs.append(pl.BlockSpec((None, nb, A_CONV - 1, A_CONV_CH), lambda i, t: (j, i, 0, 0)))
        args.append(buf)
        body = functools.partial(_even_in_kernel_buf, nb=nb, tl=tl)
    else:
        body = functools.partial(_even_in_kernel_nobuf, nb=nb, tl=tl)
    widths = (A_CONV_CH, LANES, A_VW, B_QW, 2 * B_KVW, B_QW)
    out_shape = [jax.ShapeDtypeStruct((n, l, wd), F32) for wd in widths]
    out_specs = [pl.BlockSpec((nb, tl, wd), row_map) for wd in widths]
    out_shape.append(jax.ShapeDtypeStruct((n, A_CONV - 1, A_CONV_CH), F32))
    out_specs.append(pl.BlockSpec((nb, A_CONV - 1, A_CONV_CH), lambda i, t: (i, 0, 0)))
    return pl.pallas_call(
        body, grid=grid, in_specs=in_specs, out_specs=out_specs, out_shape=out_shape,
        scratch_shapes=[pltpu.VMEM((nb, tl + 8, A_CONV_CH), F32)],
        compiler_params=_params(2), name="even_in",
    )(*args)


def _tri_inv_all(a_list, c):
    ri = lax.broadcasted_iota(jnp.int32, (c, LANES), 0)
    ci = lax.broadcasted_iota(jnp.int32, (c, LANES), 1)
    eye = jnp.where(ri == ci, 1.0, 0.0)
    ps = [eye for _ in a_list]
    ms = [-a for a in a_list]
    span = 1
    while span < c:
        mps = [_mm(m[:, :c], jnp.concatenate([m, p], axis=1)) for m, p in zip(ms, ps)]
        ms = [mp[:, :LANES] for mp in mps]
        ps = [p + mp[:, LANES:] for p, mp in zip(ps, mps)]
        span *= 2
    return ps


def _delta_prep_all(qs, ks, vs, g_cols, g_rows, betas, g_lasts, c):
    ri = lax.broadcasted_iota(jnp.int32, (c, LANES), 0)
    ci = lax.broadcasted_iota(jnp.int32, (c, LANES), 1)
    causal = ri >= ci
    strict = ri > ci
    decays = [jnp.where(causal, jnp.exp(jnp.where(causal, gc - gr, 0.0)), 0.0) for gc, gr in zip(g_cols, g_rows)]
    kbs = [k * b for k, b in zip(ks, betas)]
    if c < LANES:
        pad = jnp.zeros((LANES - c, A_DK), F32)
        k_rows = [jnp.concatenate([k, pad], axis=0).astype(BF16) for k in ks]
    else:
        k_rows = [k.astype(BF16) for k in ks]
    a_list = [jnp.where(strict, _mm_nt(kb, kr) * d, 0.0) for kb, kr, d in zip(kbs, k_rows, decays)]
    qks = [(_mm_nt(q, kr) * d)[:, :c].astype(BF16) for q, kr, d in zip(qs, k_rows, decays)]
    t_invs = _tri_inv_all(a_list, c)
    e_gs = [jnp.exp(gc) for gc in g_cols]
    uws = [_mm(t[:, :c], jnp.concatenate([v * b, kb * e], axis=1))
           for t, v, b, kb, e in zip(t_invs, vs, betas, kbs, e_gs)]
    us = [uw[:, :A_DK] for uw in uws]
    wqs = [jnp.concatenate([uw[:, A_DK:], q * e], axis=0).astype(BF16) for uw, q, e in zip(uws, qs, e_gs)]
    k_decs = [(k * jnp.exp(gl - gc)).astype(BF16) for k, gl, gc in zip(ks, g_lasts, g_cols)]
    e_lasts = [jnp.exp(gl) for gl in g_lasts]
    return us, wqs, qks, k_decs, e_lasts


def _delta_body(qkv_ref, gb_ref, s0_ref, o_ref, sout_ref, s_ref, *, nb, tl, c):
    t = pl.program_id(1)
    nt = pl.num_programs(1)

    @pl.when(t == 0)
    def _():
        if s0_ref is None:
            s_ref[...] = jnp.zeros(s_ref.shape, F32)
        else:
            s_ref[...] = s0_ref[...]

    nc = tl // c
    qs, ks, vs, g_cols, g_rows, betas, g_lasts = [], [], [], [], [], [], []
    for ic in range(nc):
        rows = slice(ic * c, (ic + 1) * c)
        for b in range(nb):
            gbc = gb_ref[b, rows, :]
            g_all = _cumsum_rows(gbc)
            if c < LANES:
                g_t = jnp.concatenate([g_all, jnp.zeros((LANES - c, LANES), F32)], axis=0).T
            else:
                g_t = g_all.T
            for h in range(A_HEADS):
                qs.append(qkv_ref[b, rows, h * A_DK:(h + 1) * A_DK])
                ks.append(qkv_ref[b, rows, A_QK + h * A_DK:A_QK + (h + 1) * A_DK])
                vs.append(qkv_ref[b, rows, 2 * A_QK + h * A_DK:2 * A_QK + (h + 1) * A_DK])
                g_cols.append(g_all[:, h:h + 1])
                g_rows.append(g_t[h:h + 1, :])
                betas.append(gbc[:, A_HEADS + h:A_HEADS + h + 1])
                g_lasts.append(g_all[c - 1:c, h:h + 1])
    us, wqs, qks, k_decs, e_lasts = _delta_prep_all(qs, ks, vs, g_cols, g_rows, betas, g_lasts, c)

    nch = nb * A_HEADS
    states = [s_ref[b, h] for b in range(nb) for h in range(A_HEADS)]
    for ic in range(nc):
        sl = slice(ic * nch, (ic + 1) * nch)
        ws_qs = [_mm(wq, s) for wq, s in zip(wqs[sl], states)]
        v_news = [(u - x[:c]).astype(BF16) for u, x in zip(us[sl], ws_qs)]
        outs = [x[c:] + _mm(qk, vn) for x, qk, vn in zip(ws_qs, qks[sl], v_news)]
        states = [s * el + _mm_tn(kd, vn) for s, el, kd, vn in zip(states, e_lasts[sl], k_decs[sl], v_news)]
        for i, o in enumerate(outs):
            b, h = divmod(i, A_HEADS)
            o_ref[b, ic * c:(ic + 1) * c, h * A_DK:(h + 1) * A_DK] = o
    for i, s in enumerate(states):
        b, h = divmod(i, A_HEADS)
        s_ref[b, h] = s

    @pl.when(t == nt - 1)
    def _():
        sout_ref[...] = s_ref[...]


def _delta_kernel_s0(qkv_ref, gb_ref, s0_ref, o_ref, sout_ref, s_ref, *, nb, tl, c):
    _delta_body(qkv_ref, gb_ref, s0_ref, o_ref, sout_ref, s_ref, nb=nb, tl=tl, c=c)


def _delta_kernel_zero(qkv_ref, gb_ref, o_ref, sout_ref, s_ref, *, nb, tl, c):
    _delta_body(qkv_ref, gb_ref, None, o_ref, sout_ref, s_ref, nb=nb, tl=tl, c=c)


def _delta(qkv, gb, s0, j, *, nb, tl, c):
    n, l, _ = qkv.shape
    grid = (n // nb, l // tl)
    row_map = lambda i, t: (i, t, 0)
    st_map = lambda i, t: (i, 0, 0, 0)
    st_block = (nb, A_HEADS, A_DK, A_DK)
    in_specs = [pl.BlockSpec((nb, tl, A_CONV_CH), row_map), pl.BlockSpec((nb, tl, LANES), row_map)]
    args = [qkv, gb]
    if s0 is not None:
        in_specs.append(pl.BlockSpec((None,) + st_block, lambda i, t: (j, i, 0, 0, 0)))
        args.append(s0)
        body = functools.partial(_delta_kernel_s0, nb=nb, tl=tl, c=c)
    else:
        body = functools.partial(_delta_kernel_zero, nb=nb, tl=tl, c=c)
    return pl.pallas_call(
        body, grid=grid, in_specs=in_specs,
        out_specs=[pl.BlockSpec((nb, tl, A_VW), row_map), pl.BlockSpec(st_block, st_map)],
        out_shape=[jax.ShapeDtypeStruct((n, l, A_VW), F32),
                   jax.ShapeDtypeStruct((n, A_HEADS, A_DK, A_DK), F32)],
        scratch_shapes=[pltpu.VMEM(st_block, F32)],
        compiler_params=_params(2), name="delta",
    )(*args)


def _swa_tasks(qs, kxs, vxs, min_key, sink_ref):
    nq = qs[0].shape[0]
    nk = 2 * WINDOW
    rows = B_HEADS * nq
    lo = lax.broadcasted_iota(jnp.int32, (nq, LANES), 1) < B_HD
    ri = lax.broadcasted_iota(jnp.int32, (rows, nk), 0) % nq
    ci = lax.broadcasted_iota(jnp.int32, (rows, nk), 1)
    mask = (ci >= jnp.maximum(ri, min_key)) & (ci <= ri + WINDOW)
    head = lax.broadcasted_iota(jnp.int32, (rows, 1), 0) // nq
    sink = jnp.zeros((rows, 1), F32)
    for h in range(B_HEADS):
        sink = jnp.where(head == h, sink_ref[h], sink)

    def stack_heads(q):
        parts = []
        for h in range(B_HEADS):
            j, e = divmod(h, 2)
            g = h // (B_HEADS // B_KV_HEADS)
            src = q[:, j * LANES:(j + 1) * LANES]
            if e != g:
                src = pltpu.roll(src, B_HD, axis=1)
            parts.append(jnp.where(lo, src, 0.0) if g == 0 else jnp.where(lo, 0.0, src))
        return jnp.concatenate(parts, axis=0).astype(BF16)

    lhs = [stack_heads(q) for q in qs]
    scale = B_HD ** -0.5
    ss = [jnp.where(mask, _mm_nt(x, kx) * scale, MASK_NEG) for x, kx in zip(lhs, kxs)]
    ms = [jnp.maximum(jnp.max(s, axis=-1, keepdims=True), sink) for s in ss]
    ps = [jnp.where(mask, jnp.exp(s - m), 0.0) for s, m in zip(ss, ms)]
    dens = [jnp.sum(p, axis=-1, keepdims=True) + jnp.exp(sink - m) for p, m in zip(ps, ms)]
    os_ = [_mm(p / d, vx) for p, d, vx in zip(ps, dens, vxs)]
    outs = []
    for o in os_:
        cols = []
        for j in range(B_HEADS // 2):
            a = o[(2 * j) * nq:(2 * j + 1) * nq]
            b = o[(2 * j + 1) * nq:(2 * j + 2) * nq]
            if j < B_HEADS // 4:
                cols.append(jnp.where(lo, a, pltpu.roll(b, B_HD, axis=1)))
            else:
                cols.append(jnp.where(lo, pltpu.roll(a, B_HD, axis=1), b))
        outs.append(jnp.concatenate(cols, axis=1))
    return outs


def _swa_prompt_kernel(sink_ref, q_ref, kvc_ref, kvp_ref, o_ref):
    i = pl.program_id(1)
    kvc = kvc_ref[0]
    kvp = kvp_ref[0]
    kx = jnp.concatenate([kvp[:, :B_KVW], kvc[:, :B_KVW]], axis=0).astype(BF16)
    vx = jnp.concatenate([kvp[:, B_KVW:], kvc[:, B_KVW:]], axis=0).astype(BF16)
    o_ref[0] = _swa_tasks([q_ref[0]], [kx], [vx], jnp.where(i > 0, 0, WINDOW), sink_ref)[0]


def _swa_prompt(qb, kvb, sinks):
    n, l, _ = qb.shape
    nblk = l // WINDOW
    return pl.pallas_call(
        _swa_prompt_kernel, grid=(n, nblk),
        in_specs=[pl.BlockSpec(memory_space=pltpu.SMEM),
                  pl.BlockSpec((1, WINDOW, B_QW), lambda b, i: (b, i, 0)),
                  pl.BlockSpec((1, WINDOW, 2 * B_KVW), lambda b, i: (b, i, 0)),
                  pl.BlockSpec((1, WINDOW, 2 * B_KVW), lambda b, i: (b, jnp.maximum(i - 1, 0), 0))],
        out_specs=pl.BlockSpec((1, WINDOW, B_QW), lambda b, i: (b, i, 0)),
        out_shape=jax.ShapeDtypeStruct((n, l, B_QW), F32),
        compiler_params=_params(2), name="swa_prompt",
    )(sinks, qb, kvb, kvb)


def _swa_sample_kernel(sink_ref, q_ref, kv_ref, ck_ref, cv_ref, o_ref, nk_ref, nv_ref, *, nb, ls):
    pad = jnp.zeros((WINDOW - ls, B_KVW), F32)
    qs, kxs, vxs = [], [], []
    for b in range(nb):
        kv = kv_ref[b]
        k_new = kv[:, :B_KVW]
        v_new = kv[:, B_KVW:]
        ck = ck_ref[b]
        cv = cv_ref[b]
        qs.append(q_ref[b])
        kxs.append(jnp.concatenate([ck, k_new, pad], axis=0).astype(BF16))
        vxs.append(jnp.concatenate([cv, v_new, pad], axis=0).astype(BF16))
        nk_ref[b, 0:WINDOW - ls, :] = ck[ls:, :]
        nk_ref[b, WINDOW - ls:, :] = k_new
        nv_ref[b, 0:WINDOW - ls, :] = cv[ls:, :]
        nv_ref[b, WINDOW - ls:, :] = v_new
    for b, o in enumerate(_swa_tasks(qs, kxs, vxs, 0, sink_ref)):
        o_ref[b] = o


def _swa_sample(qb, kvb, ck, cv, sinks, j, *, nb):
    n, ls, _ = qb.shape
    row_map = lambda i: (i, 0, 0)
    cache_spec = pl.BlockSpec((nb, WINDOW, B_KVW), row_map)
    cache_in = pl.BlockSpec((None, nb, WINDOW, B_KVW), lambda i: (j, i, 0, 0))
    cache_shape = jax.ShapeDtypeStruct((n, WINDOW, B_KVW), F32)
    return pl.pallas_call(
        functools.partial(_swa_sample_kernel, nb=nb, ls=ls), grid=(n // nb,),
        in_specs=[pl.BlockSpec(memory_space=pltpu.SMEM),
                  pl.BlockSpec((nb, ls, B_QW), row_map),
                  pl.BlockSpec((nb, ls, 2 * B_KVW), row_map),
                  cache_in, cache_in],
        out_specs=[pl.BlockSpec((nb, ls, B_QW), row_map), cache_spec, cache_spec],
        out_shape=[jax.ShapeDtypeStruct((n, ls, B_QW), F32), cache_shape, cache_shape],
        compiler_params=_params(1), name="swa_sample",
    )(sinks, qb, kvb, ck, cv)


def _odd_in_kernel(x_ref, w_ref, lbraw_ref, q_ref, k_ref, v_ref, lf_ref, g_ref, *, j):
    x = x_ref[...].astype(BF16)
    raw = lbraw_ref[...]
    e = jnp.exp(raw - jnp.max(raw, axis=0, keepdims=True))
    sm = e / jnp.sum(e, axis=0, keepdims=True)
    lb = jnp.zeros((1, D_MODEL), F32)
    for i in range(1, j + 1):
        lb = lb + sm[i:i + 1, :]
    d = D_MODEL
    q_ref[...] = _silu(jnp.dot(x, w_ref[:, 0:d], preferred_element_type=F32)) * (C_DK ** -0.5)
    zf = jnp.dot(x, w_ref[:, d:2 * d], preferred_element_type=F32)
    fg = lb + (1.0 - lb) * _sigmoid(zf)
    lf_ref[...] = jnp.log(jnp.maximum(fg, F_FLOOR))
    k_ref[...] = (1.0 - lb) * _sigmoid(-zf)
    v_ref[...] = jnp.dot(x, w_ref[:, 2 * d:3 * d], preferred_element_type=F32)
    g_ref[...] = _silu(jnp.dot(x, w_ref[:, 3 * d:4 * d], preferred_element_type=F32))


def _odd_in(x2, w, lb_raw, j, *, tr):
    rows = x2.shape[0]
    n_odd = lb_raw.shape[0]
    row_spec = pl.BlockSpec((tr, D_MODEL), lambda i: (i, 0))
    shp = jax.ShapeDtypeStruct((rows, D_MODEL), F32)
    return pl.pallas_call(
        functools.partial(_odd_in_kernel, j=j), grid=(rows // tr,),
        in_specs=[row_spec, _const_spec((D_MODEL, 4 * D_MODEL)), _const_spec((n_odd, D_MODEL))],
        out_specs=[row_spec] * 5, out_shape=[shp] * 5,
        compiler_params=_params(1), name="odd_in",
    )(x2, w, lb_raw)


def _hgrn_prep_all(qs, ks, vs, lfs, c):
    gs = [_cumsum_rows(lf) for lf in lfs]
    ri = lax.broadcasted_iota(jnp.int32, (c, c), 0)
    ci = lax.broadcasted_iota(jnp.int32, (c, c), 1)
    row = lax.broadcasted_iota(jnp.int32, (c, C_DK), 0)
    diag = ri == ci
    accs = [jnp.where(diag, _mm_nt(q, k), 0.0) for q, k in zip(qs, ks)]
    bs = list(gs)
    h = 1
    while h < c:
        bns = [pltpu.roll(b, c - h, axis=0) for b in bs]
        qhs = qs if h == 1 else [q * jnp.exp(g - b) for q, g, b in zip(qs, gs, bs)]
        khs = [k * jnp.exp(bn - g) for k, bn, g in zip(ks, bns, gs)]
        lvl = ((ri // h) == (ci // h) + 1) & ((ri // (2 * h)) == (ci // (2 * h)))
        accs = [a + jnp.where(lvl, _mm_nt(qh, kh), 0.0) for a, qh, kh in zip(accs, qhs, khs)]
        sel = (row & h) != 0
        bs = [jnp.where(sel, pltpu.roll(b, h, axis=0), b) for b in bs]
        h *= 2
    g_lasts = [g[c - 1:c, :] for g in gs]
    intra = [_mm(a, v) for a, v in zip(accs, vs)]
    q_decs = [(q * jnp.exp(g)).astype(BF16) for q, g in zip(qs, gs)]
    kvs = [_mm_tn(v, k * jnp.exp(gl - g)) for v, k, gl, g in zip(vs, ks, g_lasts, gs)]
    e_lasts = [jnp.exp(gl) for gl in g_lasts]
    return intra, q_decs, kvs, e_lasts


def _hgrn_body(q_ref, k_ref, v_ref, lf_ref, s0_ref, o_ref, sout_ref, st_ref, *, nb, hb, tl, c):
    t = pl.program_id(2)
    nt = pl.num_programs(2)
    nc = tl // c

    @pl.when(t == 0)
    def _():
        if s0_ref is None:
            st_ref[...] = jnp.zeros(st_ref.shape, F32)
        else:
            for b in range(nb):
                for h in range(hb):
                    st_ref[b, h] = s0_ref[b, h].T

    where = [(ic, b, h) for ic in range(nc) for b in range(nb) for h in range(hb)]

    def load(ref):
        return [ref[b, ic * c:(ic + 1) * c, h * C_DK:(h + 1) * C_DK] for ic, b, h in where]

    intra, q_decs, kvs, e_lasts = _hgrn_prep_all(load(q_ref), load(k_ref), load(v_ref), load(lf_ref), c)
    nch = nb * hb
    states = [st_ref[b, h] for b in range(nb) for h in range(hb)]
    for ic in range(nc):
        sl = slice(ic * nch, (ic + 1) * nch)
        outs = [_mm_nt(qd, st) + x for qd, st, x in zip(q_decs[sl], states, intra[sl])]
        states = [st * el + kv for st, el, kv in zip(states, e_lasts[sl], kvs[sl])]
        for i, o in enumerate(outs):
            b, h = divmod(i, hb)
            o_ref[b, ic * c:(ic + 1) * c, h * C_DK:(h + 1) * C_DK] = o
    for i, st in enumerate(states):
        b, h = divmod(i, hb)
        st_ref[b, h] = st

    @pl.when(t == nt - 1)
    def _():
        for b in range(nb):
            for h in range(hb):
                sout_ref[b, h] = st_ref[b, h].T


def _hgrn_kernel_s0(q_ref, k_ref, v_ref, lf_ref, s0_ref, o_ref, sout_ref, st_ref, **kw):
    _hgrn_body(q_ref, k_ref, v_ref, lf_ref, s0_ref, o_ref, sout_ref, st_ref, **kw)


def _hgrn_kernel_zero(q_ref, k_ref, v_ref, lf_ref, o_ref, sout_ref, st_ref, **kw):
    _hgrn_body(q_ref, k_ref, v_ref, lf_ref, None, o_ref, sout_ref, st_ref, **kw)


def _hgrn(q, k, v, lf, s0, j, *, nb, hb, tl, c):
    n, l, _ = q.shape
    grid = (n // nb, C_HEADS // hb, l // tl)
    row_spec = pl.BlockSpec((nb, tl, hb * C_DK), lambda i, g, t: (i, t, g))
    st_block = (nb, hb, C_DK, C_DK)
    st_spec = pl.BlockSpec(st_block, lambda i, g, t: (i, g, 0, 0))
    in_specs = [row_spec] * 4
    args = [q, k, v, lf]
    kw = dict(nb=nb, hb=hb, tl=tl, c=c)
    if s0 is not None:
        in_specs = in_specs + [pl.BlockSpec((None,) + st_block, lambda i, g, t: (j, i, g, 0, 0))]
        args.append(s0)
        body = functools.partial(_hgrn_kernel_s0, **kw)
    else:
        body = functools.partial(_hgrn_kernel_zero, **kw)
    return pl.pallas_call(
        body, grid=grid, in_specs=in_specs, out_specs=[row_spec, st_spec],
        out_shape=[jax.ShapeDtypeStruct((n, l, D_MODEL), F32),
                   jax.ShapeDtypeStruct((n, C_HEADS, C_DK, C_DK), F32)],
        scratch_shapes=[pltpu.VMEM(st_block, F32)],
        compiler_params=_params(3), name="hgrn",
    )(*args)


def _group_rms(o, g_row, heads):
    outs = []
    for h in range(heads):
        c = o[:, h * LANES:(h + 1) * LANES]
        inv = lax.rsqrt(jnp.mean(c * c, axis=-1, keepdims=True) + NORM_EPS)
        outs.append(c * inv * g_row)
    return jnp.concatenate(outs, axis=1)


def _post_tail(x, y, p_ref, lng_ref, lnb_ref, wproj_ref, wgate_ref, out_ref):
    pre = DEEPNORM_ALPHA * x + y
    mu = jnp.mean(pre, axis=-1, keepdims=True)
    xc = pre - mu
    var = jnp.mean(xc * xc, axis=-1, keepdims=True)
    hn = xc * lax.rsqrt(var + NORM_EPS) * lng_ref[...] + lnb_ref[...]
    gate = _sigmoid(jnp.dot(hn.astype(BF16), wgate_ref[...], preferred_element_type=F32))
    emb = jnp.dot(p_ref[...].astype(BF16), wproj_ref[...], preferred_element_type=F32)
    out_ref[...] = hn + gate * emb


def _post_even_kernel(x_ref, oa_ref, ga_ref, ob_ref, gbt_ref, p_ref, wout_ref, ng_ref, lng_ref, lnb_ref,
                      wproj_ref, wgate_ref, out_ref):
    a = _group_rms(oa_ref[...], ng_ref[...], A_HEADS) * ga_ref[...]
    b = ob_ref[...] * gbt_ref[...]
    y = (jnp.dot(a.astype(BF16), wout_ref[0:A_VW, :], preferred_element_type=F32)
         + jnp.dot(b.astype(BF16), wout_ref[A_VW:, :], preferred_element_type=F32))
    _post_tail(x_ref[...], y, p_ref, lng_ref, lnb_ref, wproj_ref, wgate_ref, out_ref)


def _post_odd_kernel(x_ref, o_ref, g_ref, p_ref, wout_ref, ng_ref, lng_ref, lnb_ref,
                     wproj_ref, wgate_ref, out_ref):
    a = _group_rms(o_ref[...], ng_ref[...], C_HEADS) * g_ref[...]
    y = jnp.dot(a.astype(BF16), wout_ref[...], preferred_element_type=F32)
    _post_tail(x_ref[...], y, p_ref, lng_ref, lnb_ref, wproj_ref, wgate_ref, out_ref)


def _post(x2, acts, p3, layer, wout, ng, lng, lnb, wproj, wgate, *, even, tr):
    rows = x2.shape[0]
    full = pl.BlockSpec((tr, D_MODEL), lambda i: (i, 0))
    half = pl.BlockSpec((tr, A_VW), lambda i: (i, 0))
    act_specs = [half] * 4 if even else [full] * 2
    in_specs = ([full] + act_specs + [pl.BlockSpec((None, tr, PLE_DIM), lambda i: (layer, i, 0)),
                                      _const_spec((D_MODEL, D_MODEL)), _const_spec((1, LANES)),
                                      _const_spec((1, D_MODEL)), _const_spec((1, D_MODEL)),
                                      _const_spec((PLE_DIM, D_MODEL)), _const_spec((D_MODEL, D_MODEL))])
    return pl.pallas_call(
        _post_even_kernel if even else _post_odd_kernel, grid=(rows // tr,),
        in_specs=in_specs, out_specs=full, out_shape=jax.ShapeDtypeStruct((rows, D_MODEL), F32),
        compiler_params=_params(1), name="post_even" if even else "post_odd",
    )(x2, *acts, p3, wout, ng, lng, lnb, wproj, wgate)


def _pack_even_w(w):
    qkv = w[:, 0:1536]
    ga = w[:, 1536:2048]
    ab = w[:, 2048:2056]
    qb = w[:, 2056:2568]
    kb = w[:, 2568:2696]
    vb = w[:, 2696:2824]
    gbt = w[:, 2824:3336]
    ab = jnp.pad(ab, ((0, 0), (0, LANES - 2 * A_HEADS)))
    return jnp.concatenate([qkv, ga, qb, kb, vb, gbt, ab], axis=1).astype(BF16)


def _rot_tables(pos, reps):
    inv = ROPE_THETA ** (-jnp.arange(0, ROT_DIM, 2, dtype=F32) / ROT_DIM)
    ang = pos.astype(F32)[:, None] * inv[None, :]
    cos = jnp.cos(ang)
    sin = jnp.sin(ang)
    n = pos.shape[0]
    half = ROT_DIM // 2
    cos64 = jnp.concatenate([cos, cos, jnp.ones((n, B_HD - ROT_DIM), F32)], axis=1)
    up64 = jnp.concatenate([-sin, jnp.zeros((n, B_HD - half), F32)], axis=1)
    dn64 = jnp.concatenate([jnp.zeros((n, half), F32), sin, jnp.zeros((n, B_HD - ROT_DIM), F32)], axis=1)
    tabs = jnp.stack([jnp.tile(t, (reps, LANES // B_HD)) for t in (cos64, up64, dn64)])
    return tabs


def kernel(x_prompt, x_sample, state_conv_a, state_delta_a, cache_win_k, cache_win_v, state_hgrn_c,
           p_prompt, p_sample, w_in_even, conv_w_a, a_log, dt_bias, norm_a, sinks_b, w_out_even,
           w_in_odd, lb_raw, norm_c, w_out_odd, ln_g, ln_b, w_ple_proj, w_ple_gate):
    n_p, l_p, _ = x_prompt.shape
    n_s, l_s, _ = x_sample.shape
    nb_s = 32
    rot_p = _rot_tables(jnp.arange(l_p), 1)
    rot_s = _rot_tables(PAST_LEN + jnp.arange(l_s), nb_s)
    conv_p, conv_s, delta_p, delta_s = [], [], [], []
    wk_p, wk_s, wv_p, wv_s, hg_p, hg_s = [], [], [], [], [], []
    hp, hs = x_prompt, x_sample
    pp3 = p_prompt.reshape(DEPTH, n_p * l_p, PLE_DIM)
    ps3 = p_sample.reshape(DEPTH, n_s * l_s, PLE_DIM)
    ck_all = cache_win_k.reshape(-1, n_s, WINDOW, B_KVW)
    cv_all = cache_win_v.reshape(-1, n_s, WINDOW, B_KVW)
    for layer in range(DEPTH):
        j = layer // 2
        lng = ln_g[layer].reshape(1, D_MODEL)
        lnb = ln_b[layer].reshape(1, D_MODEL)
        wproj = w_ple_proj[layer].astype(BF16)
        wgate = w_ple_gate[layer].astype(BF16)
        hp2 = hp.reshape(n_p * l_p, D_MODEL)
        hs2 = hs.reshape(n_s * l_s, D_MODEL)
        if layer % 2 == 0:
            w = _pack_even_w(w_in_even[j])
            misc = jnp.zeros((8, LANES), F32)
            misc = misc.at[0, :A_HEADS].set(a_log[j]).at[1, :A_HEADS].set(dt_bias[j])
            cw = conv_w_a[j]
            sinks = sinks_b[j]
            wout = w_out_even[j].astype(BF16)
            ng = norm_a[j].reshape(1, LANES)
            qkv, gb, ga, qb, kvb, gbt, c1 = _even_in(hp, w, cw, misc, rot_p, None, j, nb=1, tl=256)
            oa, s1 = _delta(qkv, gb, None, j, nb=1, tl=256, c=A_CHUNK)
            ob = _swa_prompt(qb, kvb, sinks)
            acts = [t.reshape(n_p * l_p, A_VW) for t in (oa, ga, ob, gbt)]
            hp = _post(hp2, acts, pp3, layer, wout, ng, lng, lnb, wproj, wgate, even=True, tr=512)
            hp = hp.reshape(n_p, l_p, D_MODEL)
            k1 = kvb[:, l_p - WINDOW:, :B_KVW].reshape(n_p, WINDOW, B_KV_HEADS, B_HD)
            v1 = kvb[:, l_p - WINDOW:, B_KVW:].reshape(n_p, WINDOW, B_KV_HEADS, B_HD)
            qkv, gb, ga, qb, kvb, gbt, c2 = _even_in(hs, w, cw, misc, rot_s, state_conv_a, j, nb=nb_s, tl=l_s)
            oa, s2 = _delta(qkv, gb, state_delta_a, j, nb=8, tl=l_s, c=l_s)
            ob, k2, v2 = _swa_sample(qb, kvb, ck_all, cv_all, sinks, j, nb=8)
            acts = [t.reshape(n_s * l_s, A_VW) for t in (oa, ga, ob, gbt)]
            hs = _post(hs2, acts, ps3, layer, wout, ng, lng, lnb, wproj, wgate, even=True, tr=512)
            hs = hs.reshape(n_s, l_s, D_MODEL)
            conv_p.append(c1); conv_s.append(c2)
            delta_p.append(s1); delta_s.append(s2)
            wk_p.append(k1); wv_p.append(v1)
            wk_s.append(k2.reshape(n_s, WINDOW, B_KV_HEADS, B_HD))
            wv_s.append(v2.reshape(n_s, WINDOW, B_KV_HEADS, B_HD))
        else:
            w = w_in_odd[j].astype(BF16)
            wout = w_out_odd[j].astype(BF16)
            ng = norm_c[j].reshape(1, LANES)
            q, k, v, lf, g = _odd_in(hp2, w, lb_raw, j, tr=256)
            shp = (n_p, l_p, D_MODEL)
            o, s1 = _hgrn(q.reshape(shp), k.reshape(shp), v.reshape(shp), lf.reshape(shp), None, j,
                          nb=1, hb=4, tl=256, c=C_CHUNK)
            hp = _post(hp2, [o.reshape(n_p * l_p, D_MODEL), g], pp3, layer, wout, ng, lng, lnb, wproj, wgate,
                       even=False, tr=512)
            hp = hp.reshape(n_p, l_p, D_MODEL)
            q, k, v, lf, g = _odd_in(hs2, w, lb_raw, j, tr=256)
            shp = (n_s, l_s, D_MODEL)
            o, s2 = _hgrn(q.reshape(shp), k.reshape(shp), v.reshape(shp), lf.reshape(shp), state_hgrn_c, j,
                          nb=4, hb=4, tl=l_s, c=l_s)
            hs = _post(hs2, [o.reshape(n_s * l_s, D_MODEL), g], ps3, layer, wout, ng, lng, lnb, wproj, wgate,
                       even=False, tr=512)
            hs = hs.reshape(n_s, l_s, D_MODEL)
            hg_p.append(s1); hg_s.append(s2)
    return (hp, hs, jnp.stack(conv_p), jnp.stack(conv_s), jnp.stack(delta_p), jnp.stack(delta_s),
            jnp.stack(wk_p), jnp.stack(wk_s), jnp.stack(wv_p), jnp.stack(wv_s), jnp.stack(hg_p), jnp.stack(hg_s))
```

```python
import functools
import math

import jax
import jax.numpy as jnp
from jax import lax
from jax.experimental import pallas as pl
from jax.experimental.pallas import tpu as pltpu

F32 = jnp.float32
BF16 = jnp.bfloat16

D_MODEL = 1024
DEPTH = 4
PAST_LEN = 8192
PLE_DIM = 256
NORM_EPS = 1e-6
MASK_NEG = -1e30
F_FLOOR = 1e-30
DEEPNORM_ALPHA = (2 * DEPTH) ** 0.25

A_HEADS = 4
A_DK = 128
A_CONV = 4
A_QK = 512
A_VW = 512
A_CONV_CH = 1536
A_CHUNK = 64

B_HEADS = 8
B_KV_HEADS = 2
B_HD = 64
B_QW = 512
B_KVW = 128
WINDOW = 128
ROT_DIM = 16
ROPE_THETA = 500000.0

C_HEADS = 8
C_DK = 128
C_CHUNK = 64

LANES = 128
VMEM_LIMIT = 56 * 1024 * 1024
SUB_ROWS = 128
POST_SUB_ROWS = 512

E_QKV = 0
E_GA = 1536
E_QB = 2048
E_KB = 2560
E_VB = 2688
E_GB = 2816
E_AB = 3328
E_TOT = 3456

NT = (((1,), (1,)), ((), ()))
TN = (((0,), (0,)), ((), ()))


def _mm(a, b):
    return jnp.dot(a.astype(BF16), b.astype(BF16), preferred_element_type=F32)


def _mm_nt(a, b):
    return lax.dot_general(a.astype(BF16), b.astype(BF16), NT, preferred_element_type=F32)


def _mm_tn(a, b):
    return lax.dot_general(a.astype(BF16), b.astype(BF16), TN, preferred_element_type=F32)


def _sigmoid(x):
    return 1.0 / (1.0 + jnp.exp(-x))


def _silu(x):
    return x * _sigmoid(x)


def _cumsum_rows(x):
    n = x.shape[0]
    row = lax.broadcasted_iota(jnp.int32, x.shape, 0)
    s = 1
    while s < n:
        x = x + jnp.where(row >= s, pltpu.roll(x, s, axis=0), 0.0)
        s *= 2
    return x


def _params(n_grid):
    return pltpu.CompilerParams(dimension_semantics=("arbitrary",) * n_grid,
                                vmem_limit_bytes=VMEM_LIMIT)


def _const_spec(shape):
    nd = len(shape)
    return pl.BlockSpec(shape, lambda *_: (0,) * nd)


def _ignore_first_ref(body, _stacked_ref, *refs):
    body(*refs)


def _stacked_spec(block, index_fn, j, n_layers, first):
    if first:
        return pl.BlockSpec((n_layers,) + block, lambda *g: (0,) + index_fn(*g))
    return pl.BlockSpec((None,) + block, lambda *g: (j,) + index_fn(*g))


def _slot_view(ref, slot):
    if slot is None:
        return ref
    j, n_layers = slot
    for i in range(n_layers):
        if i != j:
            ref[i] = jnp.zeros(ref.shape[1:], ref.dtype)
    return ref.at[j]


def _alias_stacked(body, in_specs, args, stacked, *out_indices):
    if stacked is None:
        return body, in_specs, args, {}
    if not isinstance(stacked, (list, tuple)):
        stacked = [stacked]
    for arr in reversed(stacked):
        body = functools.partial(_ignore_first_ref, body)
        in_specs = [pl.BlockSpec(memory_space=pl.ANY)] + list(in_specs)
        args = [arr] + list(args)
    return body, in_specs, args, {i: o for i, o in enumerate(out_indices)}


def _even_in_body(x_ref, w_ref, cw_ref, misc_ref, rot_ref, buf_ref,
                  qkv_ref, gb_ref, ga_ref, qb_ref, kvb_ref, gbt_ref, conv_ref, xp_ref,
                  *, nb, tl):
    t = pl.program_id(1)
    if buf_ref is not None:
        xp_ref[:, 5:8, :] = buf_ref[...]
    else:
        @pl.when(t == 0)
        def _():
            xp_ref[:, 5:8, :] = jnp.zeros((nb, 3, A_CONV_CH), F32)

    neg_a = -jnp.exp(misc_ref[0:1, :])
    dt_bias = misc_ref[1:2, :]
    half = ROT_DIM // 2
    rep = B_QW // LANES

    if nb == 1:
        nbs, ts = 1, min(tl, SUB_ROWS)
    else:
        nbs, ts = max(1, min(nb, SUB_ROWS // tl)), tl
    r = nbs * ts
    for b0 in range(0, nb, nbs):
        for r0 in range(0, tl, ts):
            bs = slice(b0, b0 + nbs)
            rs = slice(r0, r0 + ts)
            flat = slice(b0 * tl + r0, b0 * tl + r0 + r)
            x = x_ref[bs, rs, :].reshape(r, D_MODEL).astype(BF16)

            z = jnp.dot(x, w_ref[:, E_QKV:E_QKV + A_CONV_CH], preferred_element_type=F32)
            xp_ref[bs, 8 + r0:8 + r0 + ts, :] = z.reshape(nbs, ts, A_CONV_CH)
            y = xp_ref[bs, 5 + r0:5 + r0 + ts, :] * cw_ref[0:1, :].reshape(1, 1, A_CONV_CH)
            for j in range(1, A_CONV):
                y = y + xp_ref[bs, 5 + j + r0:5 + j + r0 + ts, :] * cw_ref[j:j + 1, :].reshape(1, 1, A_CONV_CH)
            y = _silu(y).reshape(r, A_CONV_CH)
            for h in range(2 * A_HEADS):
                c = y[:, h * A_DK:(h + 1) * A_DK]
                inv = lax.rsqrt(jnp.sum(c * c, axis=-1, keepdims=True) + NORM_EPS)
                if h < A_HEADS:
                    inv = inv * (A_DK ** -0.5)
                qkv_ref[bs, rs, h * A_DK:(h + 1) * A_DK] = (c * inv).reshape(nbs, ts, A_DK)
            qkv_ref[bs, rs, 2 * A_QK:] = y[:, 2 * A_QK:].reshape(nbs, ts, A_VW)

            ab = jnp.dot(x, w_ref[:, E_AB:E_AB + LANES], preferred_element_type=F32)
            sp_in = ab + dt_bias
            softplus = jnp.maximum(sp_in, 0.0) + jnp.log1p(jnp.exp(-jnp.abs(sp_in)))
            lane = lax.broadcasted_iota(jnp.int32, ab.shape, 1)
            gb = jnp.where(lane < A_HEADS, neg_a * softplus, _sigmoid(ab))
            gb_ref[bs, rs, :] = gb.reshape(nbs, ts, LANES)

            ga_ref[bs, rs, :] = _silu(jnp.dot(x, w_ref[:, E_GA:E_GA + A_VW],
                                              preferred_element_type=F32)).reshape(nbs, ts, A_VW)
            gbt_ref[bs, rs, :] = _silu(jnp.dot(x, w_ref[:, E_GB:E_GB + B_QW],
                                               preferred_element_type=F32)).reshape(nbs, ts, B_QW)

            cos_t = rot_ref[0, flat, :]
            s_up = rot_ref[1, flat, :]
            s_dn = rot_ref[2, flat, :]
            zq = jnp.dot(x, w_ref[:, E_QB:E_QB + B_QW], preferred_element_type=F32)
            cos4 = jnp.concatenate([cos_t] * rep, axis=1)
            up4 = jnp.concatenate([s_up] * rep, axis=1)
            dn4 = jnp.concatenate([s_dn] * rep, axis=1)
            qrot = zq * cos4 + pltpu.roll(zq, B_QW - half, axis=1) * up4 + pltpu.roll(zq, half, axis=1) * dn4
            qb_ref[bs, rs, :] = qrot.reshape(nbs, ts, B_QW)
            zk = jnp.dot(x, w_ref[:, E_KB:E_KB + B_KVW], preferred_element_type=F32)
            krot = zk * cos_t + pltpu.roll(zk, LANES - half, axis=1) * s_up + pltpu.roll(zk, half, axis=1) * s_dn
            kvb_ref[bs, rs, 0:B_KVW] = krot.reshape(nbs, ts, B_KVW)
            zv = jnp.dot(x, w_ref[:, E_VB:E_VB + B_KVW], preferred_element_type=F32)
            kvb_ref[bs, rs, B_KVW:] = zv.reshape(nbs, ts, B_KVW)

    last = xp_ref[:, tl + 5:tl + 8, :]
    conv_ref[...] = last
    if buf_ref is None:
        xp_ref[:, 5:8, :] = last


def _even_in_kernel_buf(x_ref, w_ref, cw_ref, misc_ref, rot_ref, buf_ref, *rest, nb, tl):
    _even_in_body(x_ref, w_ref, cw_ref, misc_ref, rot_ref, buf_ref, *rest, nb=nb, tl=tl)


def _even_in_kernel_nobuf(x_ref, w_ref, cw_ref, misc_ref, rot_ref, *rest, nb, tl):
    _even_in_body(x_ref, w_ref, cw_ref, misc_ref, rot_ref, None, *rest, nb=nb, tl=tl)


def _even_in(x, w, cw, misc, rot, buf, j, *, nb, tl):
    n, l, _ = x.shape
    grid = (n // nb, l // tl)
    r = nb * tl
    row_map = lambda i, t: (i, t, 0)
    rot_map = (lambda i, t: (0, t, 0)) if buf is None else (lambda i, t: (0, 0, 0))
    in_specs = [
        pl.BlockSpec((nb, tl, D_MODEL), row_map),
        _const_spec((D_MODEL, E_TOT)),
        _const_spec((A_CONV, A_CONV_CH)),
        _const_spec((8, LANES)),
        pl.BlockSpec((3, r, LANES), rot_map),
    ]
    args = [x, w, cw, misc, rot]
    if buf is not None:
        in_specs.append(pl.BlockSpec((None, nb, A_CONV - 1, A_CONV_CH), lambda i, t: (j, i, 0, 0)))
        args.append(buf)
        body = functools.partial(_even_in_kernel_buf, nb=nb, tl=tl)
    else:
        body = functools.partial(_even_in_kernel_nobuf, nb=nb, tl=tl)
    widths = (A_CONV_CH, LANES, A_VW, B_QW, 2 * B_KVW, B_QW)
    out_shape = [jax.ShapeDtypeStruct((n, l, wd), F32) for wd in widths]
    out_specs = [pl.BlockSpec((nb, tl, wd), row_map) for wd in widths]
    out_shape.append(jax.ShapeDtypeStruct((n, A_CONV - 1, A_CONV_CH), F32))
    out_specs.append(pl.BlockSpec((nb, A_CONV - 1, A_CONV_CH), lambda i, t: (i, 0, 0)))
    return pl.pallas_call(
        body, grid=grid, in_specs=in_specs, out_specs=out_specs, out_shape=out_shape,
        scratch_shapes=[pltpu.VMEM((nb, tl + 8, A_CONV_CH), F32)],
        compiler_params=_params(2), name="even_in",
    )(*args)


def _tri_inv_all(a_list, c):
    ri = lax.broadcasted_iota(jnp.int32, (c, LANES), 0)
    ci = lax.broadcasted_iota(jnp.int32, (c, LANES), 1)
    eye = jnp.where(ri == ci, 1.0, 0.0)
    ps = [eye for _ in a_list]
    ms = [-a for a in a_list]
    span = 1
    while span < c:
        mps = [_mm(m[:, :c], jnp.concatenate([m, p], axis=1)) for m, p in zip(ms, ps)]
        ms = [mp[:, :LANES] for mp in mps]
        ps = [p + mp[:, LANES:] for p, mp in zip(ps, mps)]
        span *= 2
    return ps


def _delta_prep_all(qs, ks, vs, g_cols, g_rows, betas, g_lasts, c):
    ri = lax.broadcasted_iota(jnp.int32, (c, LANES), 0)
    ci = lax.broadcasted_iota(jnp.int32, (c, LANES), 1)
    causal = ri >= ci
    strict = ri > ci
    decays = [jnp.where(causal, jnp.exp(jnp.where(causal, gc - gr, 0.0)), 0.0) for gc, gr in zip(g_cols, g_rows)]
    kbs = [k * b for k, b in zip(ks, betas)]
    if c < LANES:
        pad = jnp.zeros((LANES - c, A_DK), F32)
        k_rows = [jnp.concatenate([k, pad], axis=0).astype(BF16) for k in ks]
    else:
        k_rows = [k.astype(BF16) for k in ks]
    a_list = [jnp.where(strict, _mm_nt(kb, kr) * d, 0.0) for kb, kr, d in zip(kbs, k_rows, decays)]
    qks = [(_mm_nt(q, kr) * d)[:, :c].astype(BF16) for q, kr, d in zip(qs, k_rows, decays)]
    t_invs = _tri_inv_all(a_list, c)
    e_gs = [jnp.exp(gc) for gc in g_cols]
    uws = [_mm(t[:, :c], jnp.concatenate([v * b, kb * e], axis=1))
           for t, v, b, kb, e in zip(t_invs, vs, betas, kbs, e_gs)]
    us = [uw[:, :A_DK] for uw in uws]
    wqs = [jnp.concatenate([uw[:, A_DK:], q * e], axis=0).astype(BF16) for uw, q, e in zip(uws, qs, e_gs)]
    k_decs = [(k * jnp.exp(gl - gc)).astype(BF16) for k, gl, gc in zip(ks, g_lasts, g_cols)]
    e_lasts = [jnp.exp(gl) for gl in g_lasts]
    return us, wqs, qks, k_decs, e_lasts


def _delta_body(qkv_ref, gb_ref, s0_ref, o_ref, sout_ref, s_ref, *, nb, tl, c, slot):
    t = pl.program_id(1)
    nt = pl.num_programs(1)

    @pl.when(t == 0)
    def _():
        if s0_ref is None:
            s_ref[...] = jnp.zeros(s_ref.shape, F32)
        else:
            s_ref[...] = s0_ref[...]

    nc = tl // c
    qs, ks, vs, g_cols, g_rows, betas, g_lasts = [], [], [], [], [], [], []
    for ic in range(nc):
        rows = slice(ic * c, (ic + 1) * c)
        for b in range(nb):
            gbc = gb_ref[b, rows, :]
            g_all = _cumsum_rows(gbc)
            if c < LANES:
                g_t = jnp.concatenate([g_all, jnp.zeros((LANES - c, LANES), F32)], axis=0).T
            else:
                g_t = g_all.T
            for h in range(A_HEADS):
                qs.append(qkv_ref[b, rows, h * A_DK:(h + 1) * A_DK])
                ks.append(qkv_ref[b, rows, A_QK + h * A_DK:A_QK + (h + 1) * A_DK])
                vs.append(qkv_ref[b, rows, 2 * A_QK + h * A_DK:2 * A_QK + (h + 1) * A_DK])
                g_cols.append(g_all[:, h:h + 1])
                g_rows.append(g_t[h:h + 1, :])
                betas.append(gbc[:, A_HEADS + h:A_HEADS + h + 1])
                g_lasts.append(g_all[c - 1:c, h:h + 1])
    us, wqs, qks, k_decs, e_lasts = _delta_prep_all(qs, ks, vs, g_cols, g_rows, betas, g_lasts, c)

    nch = nb * A_HEADS
    states = [s_ref[b, h] for b in range(nb) for h in range(A_HEADS)]
    for ic in range(nc):
        sl = slice(ic * nch, (ic + 1) * nch)
        ws_qs = [_mm(wq, s) for wq, s in zip(wqs[sl], states)]
        v_news = [(u - x[:c]).astype(BF16) for u, x in zip(us[sl], ws_qs)]
        outs = [x[c:] + _mm(qk, vn) for x, qk, vn in zip(ws_qs, qks[sl], v_news)]
        states = [s * el + _mm_tn(kd, vn) for s, el, kd, vn in zip(states, e_lasts[sl], k_decs[sl], v_news)]
        for i, o in enumerate(outs):
            b, h = divmod(i, A_HEADS)
            o_ref[b, ic * c:(ic + 1) * c, h * A_DK:(h + 1) * A_DK] = o
    for i, s in enumerate(states):
        b, h = divmod(i, A_HEADS)
        s_ref[b, h] = s

    @pl.when(t == nt - 1)
    def _():
        _slot_view(sout_ref, slot)[...] = s_ref[...]


def _delta_kernel_s0(qkv_ref, gb_ref, s0_ref, o_ref, sout_ref, s_ref, **kw):
    _delta_body(qkv_ref, gb_ref, s0_ref, o_ref, sout_ref, s_ref, **kw)


def _delta_kernel_zero(qkv_ref, gb_ref, o_ref, sout_ref, s_ref, **kw):
    _delta_body(qkv_ref, gb_ref, None, o_ref, sout_ref, s_ref, **kw)


def _delta(qkv, gb, s0, j, stacked, n_layers, *, nb, tl, c):
    n, l, _ = qkv.shape
    grid = (n // nb, l // tl)
    row_map = lambda i, t: (i, t, 0)
    st_block = (nb, A_HEADS, A_DK, A_DK)
    st_map = lambda i, t: (i, 0, 0, 0)
    first = stacked is None
    kw = dict(nb=nb, tl=tl, c=c, slot=(j, n_layers) if first else None)
    in_specs = [pl.BlockSpec((nb, tl, A_CONV_CH), row_map), pl.BlockSpec((nb, tl, LANES), row_map)]
    args = [qkv, gb]
    if s0 is not None:
        in_specs.append(_stacked_spec(st_block, st_map, j, n_layers, False))
        args.append(s0)
        body = functools.partial(_delta_kernel_s0, **kw)
    else:
        body = functools.partial(_delta_kernel_zero, **kw)
    body, in_specs, args, aliases = _alias_stacked(body, in_specs, args, stacked, 1)
    return pl.pallas_call(
        body, grid=grid, in_specs=in_specs,
        out_specs=[pl.BlockSpec((nb, tl, A_VW), row_map), _stacked_spec(st_block, st_map, j, n_layers, first)],
        out_shape=[jax.ShapeDtypeStruct((n, l, A_VW), F32),
                   jax.ShapeDtypeStruct((n_layers, n, A_HEADS, A_DK, A_DK), F32)],
        scratch_shapes=[pltpu.VMEM(st_block, F32)], input_output_aliases=aliases,
        compiler_params=_params(2), name="delta",
    )(*args)


def _swa_tasks(qs, kxs, vxs, min_key, sink_ref):
    nq = qs[0].shape[0]
    nk = 2 * WINDOW
    rows = B_HEADS * nq
    lo = lax.broadcasted_iota(jnp.int32, (nq, LANES), 1) < B_HD
    ri = lax.broadcasted_iota(jnp.int32, (rows, nk), 0) % nq
    ci = lax.broadcasted_iota(jnp.int32, (rows, nk), 1)
    mask = (ci >= jnp.maximum(ri, min_key)) & (ci <= ri + WINDOW)
    head = lax.broadcasted_iota(jnp.int32, (rows, 1), 0) // nq
    sink = jnp.zeros((rows, 1), F32)
    for h in range(B_HEADS):
        sink = jnp.where(head == h, sink_ref[h], sink)

    def stack_heads(q):
        parts = []
        for h in range(B_HEADS):
            j, e = divmod(h, 2)
            g = h // (B_HEADS // B_KV_HEADS)
            src = q[:, j * LANES:(j + 1) * LANES]
            if e != g:
                src = pltpu.roll(src, B_HD, axis=1)
            parts.append(jnp.where(lo, src, 0.0) if g == 0 else jnp.where(lo, 0.0, src))
        return jnp.concatenate(parts, axis=0).astype(BF16)

    lhs = [stack_heads(q) for q in qs]
    scale = B_HD ** -0.5
    ss = [jnp.where(mask, _mm_nt(x, kx) * scale, MASK_NEG) for x, kx in zip(lhs, kxs)]
    ms = [jnp.maximum(jnp.max(s, axis=-1, keepdims=True), sink) for s in ss]
    ps = [jnp.where(mask, jnp.exp(s - m), 0.0) for s, m in zip(ss, ms)]
    dens = [jnp.sum(p, axis=-1, keepdims=True) + jnp.exp(sink - m) for p, m in zip(ps, ms)]
    os_ = [_mm(p / d, vx) for p, d, vx in zip(ps, dens, vxs)]
    outs = []
    for o in os_:
        cols = []
        for j in range(B_HEADS // 2):
            a = o[(2 * j) * nq:(2 * j + 1) * nq]
            b = o[(2 * j + 1) * nq:(2 * j + 2) * nq]
            if j < B_HEADS // 4:
                cols.append(jnp.where(lo, a, pltpu.roll(b, B_HD, axis=1)))
            else:
                cols.append(jnp.where(lo, pltpu.roll(a, B_HD, axis=1), b))
        outs.append(jnp.concatenate(cols, axis=1))
    return outs


def _swa_prompt_kernel(sink_ref, q_ref, kvc_ref, kvp_ref, o_ref):
    i = pl.program_id(1)
    kvc = kvc_ref[0]
    kvp = kvp_ref[0]
    kx = jnp.concatenate([kvp[:, :B_KVW], kvc[:, :B_KVW]], axis=0).astype(BF16)
    vx = jnp.concatenate([kvp[:, B_KVW:], kvc[:, B_KVW:]], axis=0).astype(BF16)
    o_ref[0] = _swa_tasks([q_ref[0]], [kx], [vx], jnp.where(i > 0, 0, WINDOW), sink_ref)[0]


def _swa_prompt(qb, kvb, sinks):
    n, l, _ = qb.shape
    nblk = l // WINDOW
    return pl.pallas_call(
        _swa_prompt_kernel, grid=(n, nblk),
        in_specs=[pl.BlockSpec(memory_space=pltpu.SMEM),
                  pl.BlockSpec((1, WINDOW, B_QW), lambda b, i: (b, i, 0)),
                  pl.BlockSpec((1, WINDOW, 2 * B_KVW), lambda b, i: (b, i, 0)),
                  pl.BlockSpec((1, WINDOW, 2 * B_KVW), lambda b, i: (b, jnp.maximum(i - 1, 0), 0))],
        out_specs=pl.BlockSpec((1, WINDOW, B_QW), lambda b, i: (b, i, 0)),
        out_shape=jax.ShapeDtypeStruct((n, l, B_QW), F32),
        compiler_params=_params(2), name="swa_prompt",
    )(sinks, qb, kvb, kvb)


def _swa_sample_kernel(sink_ref, q_ref, kv_ref, ck_ref, cv_ref, o_ref, nk_ref, nv_ref, *, nb, ls, slot):
    nk_ref = _slot_view(nk_ref, slot)
    nv_ref = _slot_view(nv_ref, slot)
    pad = jnp.zeros((WINDOW - ls, B_KVW), F32)
    qs, kxs, vxs = [], [], []
    for b in range(nb):
        kv = kv_ref[b]
        k_new = kv[:, :B_KVW]
        v_new = kv[:, B_KVW:]
        ck = ck_ref[b]
        cv = cv_ref[b]
        qs.append(q_ref[b])
        kxs.append(jnp.concatenate([ck, k_new, pad], axis=0).astype(BF16))
        vxs.append(jnp.concatenate([cv, v_new, pad], axis=0).astype(BF16))
        nk_ref[b, 0:WINDOW - ls, :] = ck[ls:, :]
        nk_ref[b, WINDOW - ls:, :] = k_new
        nv_ref[b, 0:WINDOW - ls, :] = cv[ls:, :]
        nv_ref[b, WINDOW - ls:, :] = v_new
    for b, o in enumerate(_swa_tasks(qs, kxs, vxs, 0, sink_ref)):
        o_ref[b] = o


def _swa_sample(qb, kvb, ck, cv, sinks, j, stacked, *, nb):
    n, ls, _ = qb.shape
    row_map = lambda i: (i, 0, 0)
    n_layers = ck.shape[0]
    first = stacked is None
    cache_block = (nb, WINDOW, B_KVW)
    cache_spec = _stacked_spec(cache_block, row_map, j, n_layers, False)
    cache_out = _stacked_spec(cache_block, row_map, j, n_layers, first)
    cache_shape = jax.ShapeDtypeStruct(ck.shape, F32)
    body = functools.partial(_swa_sample_kernel, nb=nb, ls=ls, slot=(j, n_layers) if first else None)
    in_specs = [pl.BlockSpec(memory_space=pltpu.SMEM),
                pl.BlockSpec((nb, ls, B_QW), row_map),
                pl.BlockSpec((nb, ls, 2 * B_KVW), row_map),
                cache_spec, cache_spec]
    args = [sinks, qb, kvb, ck, cv]
    body, in_specs, args, aliases = _alias_stacked(body, in_specs, args, stacked, 1, 2)
    return pl.pallas_call(
        body, grid=(n // nb,), in_specs=in_specs,
        out_specs=[pl.BlockSpec((nb, ls, B_QW), row_map), cache_out, cache_out],
        out_shape=[jax.ShapeDtypeStruct((n, ls, B_QW), F32), cache_shape, cache_shape],
        input_output_aliases=aliases, compiler_params=_params(1), name="swa_sample",
    )(*args)


def _odd_in_kernel(x_ref, w_ref, lbraw_ref, q_ref, k_ref, v_ref, lf_ref, g_ref, *, j, ts):
    raw = lbraw_ref[...]
    e = jnp.exp(raw - jnp.max(raw, axis=0, keepdims=True))
    sm = e / jnp.sum(e, axis=0, keepdims=True)
    lb = jnp.zeros((1, D_MODEL), F32)
    for i in range(1, j + 1):
        lb = lb + sm[i:i + 1, :]
    d = D_MODEL
    for s in range(x_ref.shape[0] // ts):
        rows = slice(s * ts, (s + 1) * ts)
        x = x_ref[rows, :].astype(BF16)
        q_ref[rows, :] = _silu(jnp.dot(x, w_ref[:, 0:d], preferred_element_type=F32)) * (C_DK ** -0.5)
        zf = jnp.dot(x, w_ref[:, d:2 * d], preferred_element_type=F32)
        fg = lb + (1.0 - lb) * _sigmoid(zf)
        lf_ref[rows, :] = jnp.log(jnp.maximum(fg, F_FLOOR))
        k_ref[rows, :] = (1.0 - lb) * _sigmoid(-zf)
        v_ref[rows, :] = jnp.dot(x, w_ref[:, 2 * d:3 * d], preferred_element_type=F32)
        g_ref[rows, :] = _silu(jnp.dot(x, w_ref[:, 3 * d:4 * d], preferred_element_type=F32))


def _odd_in(x2, w, lb_raw, j, *, tr, ts=SUB_ROWS):
    rows = x2.shape[0]
    n_odd = lb_raw.shape[0]
    row_spec = pl.BlockSpec((tr, D_MODEL), lambda i: (i, 0))
    shp = jax.ShapeDtypeStruct((rows, D_MODEL), F32)
    return pl.pallas_call(
        functools.partial(_odd_in_kernel, j=j, ts=min(ts, tr)), grid=(rows // tr,),
        in_specs=[row_spec, _const_spec((D_MODEL, 4 * D_MODEL)), _const_spec((n_odd, D_MODEL))],
        out_specs=[row_spec] * 5, out_shape=[shp] * 5,
        compiler_params=_params(1), name="odd_in",
    )(x2, w, lb_raw)


def _hgrn_prep_all(qs, ks, vs, lfs, c):
    gs = [_cumsum_rows(lf) for lf in lfs]
    ri = lax.broadcasted_iota(jnp.int32, (c, c), 0)
    ci = lax.broadcasted_iota(jnp.int32, (c, c), 1)
    row = lax.broadcasted_iota(jnp.int32, (c, C_DK), 0)
    diag = ri == ci
    accs = [jnp.where(diag, _mm_nt(q, k), 0.0) for q, k in zip(qs, ks)]
    bs = list(gs)
    h = 1
    while h < c:
        bns = [pltpu.roll(b, c - h, axis=0) for b in bs]
        qhs = qs if h == 1 else [q * jnp.exp(g - b) for q, g, b in zip(qs, gs, bs)]
        khs = [k * jnp.exp(bn - g) for k, bn, g in zip(ks, bns, gs)]
        lvl = ((ri // h) == (ci // h) + 1) & ((ri // (2 * h)) == (ci // (2 * h)))
        accs = [a + jnp.where(lvl, _mm_nt(qh, kh), 0.0) for a, qh, kh in zip(accs, qhs, khs)]
        sel = (row & h) != 0
        bs = [jnp.where(sel, pltpu.roll(b, h, axis=0), b) for b in bs]
        h *= 2
    g_lasts = [g[c - 1:c, :] for g in gs]
    intra = [_mm(a, v) for a, v in zip(accs, vs)]
    q_decs = [(q * jnp.exp(g)).astype(BF16) for q, g in zip(qs, gs)]
    kvs = [_mm_tn(v, k * jnp.exp(gl - g)) for v, k, gl, g in zip(vs, ks, g_lasts, gs)]
    e_lasts = [jnp.exp(gl) for gl in g_lasts]
    return intra, q_decs, kvs, e_lasts


def _hgrn_body(q_ref, k_ref, v_ref, lf_ref, s0_ref, o_ref, sout_ref, st_ref, *, nb, hb, tl, c, slot):
    t = pl.program_id(2)
    nt = pl.num_programs(2)
    nc = tl // c

    @pl.when(t == 0)
    def _():
        if s0_ref is None:
            st_ref[...] = jnp.zeros(st_ref.shape, F32)
        else:
            for b in range(nb):
                for h in range(hb):
                    st_ref[b, h] = s0_ref[b, h].T

    where = [(ic, b, h) for ic in range(nc) for b in range(nb) for h in range(hb)]

    def load(ref):
        return [ref[b, ic * c:(ic + 1) * c, h * C_DK:(h + 1) * C_DK] for ic, b, h in where]

    intra, q_decs, kvs, e_lasts = _hgrn_prep_all(load(q_ref), load(k_ref), load(v_ref), load(lf_ref), c)
    nch = nb * hb
    states = [st_ref[b, h] for b in range(nb) for h in range(hb)]
    for ic in range(nc):
        sl = slice(ic * nch, (ic + 1) * nch)
        outs = [_mm_nt(qd, st) + x for qd, st, x in zip(q_decs[sl], states, intra[sl])]
        states = [st * el + kv for st, el, kv in zip(states, e_lasts[sl], kvs[sl])]
        for i, o in enumerate(outs):
            b, h = divmod(i, hb)
            o_ref[b, ic * c:(ic + 1) * c, h * C_DK:(h + 1) * C_DK] = o
    for i, st in enumerate(states):
        b, h = divmod(i, hb)
        st_ref[b, h] = st

    @pl.when(t == nt - 1)
    def _():
        out = _slot_view(sout_ref, slot)
        for b in range(nb):
            for h in range(hb):
                out[b, h] = st_ref[b, h].T


def _hgrn_kernel_s0(q_ref, k_ref, v_ref, lf_ref, s0_ref, o_ref, sout_ref, st_ref, **kw):
    _hgrn_body(q_ref, k_ref, v_ref, lf_ref, s0_ref, o_ref, sout_ref, st_ref, **kw)


def _hgrn_kernel_zero(q_ref, k_ref, v_ref, lf_ref, o_ref, sout_ref, st_ref, **kw):
    _hgrn_body(q_ref, k_ref, v_ref, lf_ref, None, o_ref, sout_ref, st_ref, **kw)


def _hgrn(q, k, v, lf, s0, j, stacked, n_layers, *, nb, hb, tl, c):
    n, l, _ = q.shape
    grid = (n // nb, C_HEADS // hb, l // tl)
    row_spec = pl.BlockSpec((nb, tl, hb * C_DK), lambda i, g, t: (i, t, g))
    st_block = (nb, hb, C_DK, C_DK)
    st_map = lambda i, g, t: (i, g, 0, 0)
    first = stacked is None
    st_spec = _stacked_spec(st_block, st_map, j, n_layers, first)
    in_specs = [row_spec] * 4
    args = [q, k, v, lf]
    kw = dict(nb=nb, hb=hb, tl=tl, c=c, slot=(j, n_layers) if first else None)
    if s0 is not None:
        in_specs = in_specs + [_stacked_spec(st_block, st_map, j, n_layers, False)]
        args.append(s0)
        body = functools.partial(_hgrn_kernel_s0, **kw)
    else:
        body = functools.partial(_hgrn_kernel_zero, **kw)
    body, in_specs, args, aliases = _alias_stacked(body, in_specs, args, stacked, 1)
    return pl.pallas_call(
        body, grid=grid, in_specs=in_specs, out_specs=[row_spec, st_spec],
        out_shape=[jax.ShapeDtypeStruct((n, l, D_MODEL), F32),
                   jax.ShapeDtypeStruct((n_layers, n, C_HEADS, C_DK, C_DK), F32)],
        scratch_shapes=[pltpu.VMEM(st_block, F32)], input_output_aliases=aliases,
        compiler_params=_params(3), name="hgrn",
    )(*args)


def _group_rms(o, g_row, heads):
    outs = []
    for h in range(heads):
        c = o[:, h * LANES:(h + 1) * LANES]
        inv = lax.rsqrt(jnp.mean(c * c, axis=-1, keepdims=True) + NORM_EPS)
        outs.append(c * inv * g_row)
    return jnp.concatenate(outs, axis=1)


def _post_tail(x, y, p, lng_ref, lnb_ref, wproj_ref, wgate_ref):
    pre = DEEPNORM_ALPHA * x + y
    mu = jnp.mean(pre, axis=-1, keepdims=True)
    xc = pre - mu
    var = jnp.mean(xc * xc, axis=-1, keepdims=True)
    hn = xc * lax.rsqrt(var + NORM_EPS) * lng_ref[...] + lnb_ref[...]
    gate = _sigmoid(jnp.dot(hn.astype(BF16), wgate_ref[...], preferred_element_type=F32))
    emb = jnp.dot(p.astype(BF16), wproj_ref[...], preferred_element_type=F32)
    return hn + gate * emb


def _sub_tiles(n_rows):
    ts = min(n_rows, POST_SUB_ROWS)
    return [slice(r0, r0 + ts) for r0 in range(0, n_rows, ts)]


def _post_even_kernel(x_ref, oa_ref, ga_ref, ob_ref, gbt_ref, p_ref, wout_ref, ng_ref, lng_ref, lnb_ref,
                      wproj_ref, wgate_ref, out_ref):
    for rows in _sub_tiles(x_ref.shape[0]):
        a = _group_rms(oa_ref[rows, :], ng_ref[...], A_HEADS) * ga_ref[rows, :]
        b = ob_ref[rows, :] * gbt_ref[rows, :]
        y = (jnp.dot(a.astype(BF16), wout_ref[0:A_VW, :], preferred_element_type=F32)
             + jnp.dot(b.astype(BF16), wout_ref[A_VW:, :], preferred_element_type=F32))
        out_ref[rows, :] = _post_tail(x_ref[rows, :], y, p_ref[rows, :], lng_ref, lnb_ref, wproj_ref, wgate_ref)


def _post_odd_kernel(x_ref, o_ref, g_ref, p_ref, wout_ref, ng_ref, lng_ref, lnb_ref,
                     wproj_ref, wgate_ref, out_ref):
    for rows in _sub_tiles(x_ref.shape[0]):
        a = _group_rms(o_ref[rows, :], ng_ref[...], C_HEADS) * g_ref[rows, :]
        y = jnp.dot(a.astype(BF16), wout_ref[...], preferred_element_type=F32)
        out_ref[rows, :] = _post_tail(x_ref[rows, :], y, p_ref[rows, :], lng_ref, lnb_ref, wproj_ref, wgate_ref)


def _post(x2, acts, p3, layer, wout, ng, lng, lnb, wproj, wgate, *, even, tr):
    rows = x2.shape[0]
    full = pl.BlockSpec((tr, D_MODEL), lambda i: (i, 0))
    half = pl.BlockSpec((tr, A_VW), lambda i: (i, 0))
    act_specs = [half] * 4 if even else [full] * 2
    in_specs = ([full] + act_specs + [pl.BlockSpec((None, tr, PLE_DIM), lambda i: (layer, i, 0)),
                                      _const_spec((D_MODEL, D_MODEL)), _const_spec((1, LANES)),
                                      _const_spec((1, D_MODEL)), _const_spec((1, D_MODEL)),
                                      _const_spec((PLE_DIM, D_MODEL)), _const_spec((D_MODEL, D_MODEL))])
    return pl.pallas_call(
        _post_even_kernel if even else _post_odd_kernel, grid=(rows // tr,),
        in_specs=in_specs, out_specs=full, out_shape=jax.ShapeDtypeStruct((rows, D_MODEL), F32),
        compiler_params=_params(1), name="post_even" if even else "post_odd",
    )(x2, *acts, p3, wout, ng, lng, lnb, wproj, wgate)


def _pack_even_w(w):
    qkv = w[:, 0:1536]
    ga = w[:, 1536:2048]
    ab = w[:, 2048:2056]
    qb = w[:, 2056:2568]
    kb = w[:, 2568:2696]
    vb = w[:, 2696:2824]
    gbt = w[:, 2824:3336]
    ab = jnp.pad(ab, ((0, 0), (0, LANES - 2 * A_HEADS)))
    return jnp.concatenate([qkv, ga, qb, kb, vb, gbt, ab], axis=1).astype(BF16)


def _rot_tables(pos, reps):
    inv = ROPE_THETA ** (-jnp.arange(0, ROT_DIM, 2, dtype=F32) / ROT_DIM)
    ang = pos.astype(F32)[:, None] * inv[None, :]
    cos = jnp.cos(ang)
    sin = jnp.sin(ang)
    n = pos.shape[0]
    half = ROT_DIM // 2
    cos64 = jnp.concatenate([cos, cos, jnp.ones((n, B_HD - ROT_DIM), F32)], axis=1)
    up64 = jnp.concatenate([-sin, jnp.zeros((n, B_HD - half), F32)], axis=1)
    dn64 = jnp.concatenate([jnp.zeros((n, half), F32), sin, jnp.zeros((n, B_HD - ROT_DIM), F32)], axis=1)
    tabs = jnp.stack([jnp.tile(t, (reps, LANES // B_HD)) for t in (cos64, up64, dn64)])
    return tabs


def kernel(x_prompt, x_sample, state_conv_a, state_delta_a, cache_win_k, cache_win_v, state_hgrn_c,
           p_prompt, p_sample, w_in_even, conv_w_a, a_log, dt_bias, norm_a, sinks_b, w_out_even,
           w_in_odd, lb_raw, norm_c, w_out_odd, ln_g, ln_b, w_ple_proj, w_ple_gate):
    n_p, l_p, _ = x_prompt.shape
    n_s, l_s, _ = x_sample.shape
    nb_s = 32
    rot_p = _rot_tables(jnp.arange(l_p), 1)
    rot_s = _rot_tables(PAST_LEN + jnp.arange(l_s), nb_s)
    n_even, n_odd = w_in_even.shape[0], w_in_odd.shape[0]
    conv_p, conv_s, wk_p, wv_p = [], [], [], []
    delta_p = delta_s = hg_p = hg_s = win_s = None
    hp, hs = x_prompt, x_sample
    pp3 = p_prompt.reshape(DEPTH, n_p * l_p, PLE_DIM)
    ps3 = p_sample.reshape(DEPTH, n_s * l_s, PLE_DIM)
    ck_all = cache_win_k.reshape(-1, n_s, WINDOW, B_KVW)
    cv_all = cache_win_v.reshape(-1, n_s, WINDOW, B_KVW)
    for layer in range(DEPTH):
        j = layer // 2
        lng = ln_g[layer].reshape(1, D_MODEL)
        lnb = ln_b[layer].reshape(1, D_MODEL)
        wproj = w_ple_proj[layer].astype(BF16)
        wgate = w_ple_gate[layer].astype(BF16)
        hp2 = hp.reshape(n_p * l_p, D_MODEL)
        hs2 = hs.reshape(n_s * l_s, D_MODEL)
        if layer % 2 == 0:
            w = _pack_even_w(w_in_even[j])
            misc = jnp.zeros((8, LANES), F32)
            misc = misc.at[0, :A_HEADS].set(a_log[j]).at[1, :A_HEADS].set(dt_bias[j])
            cw = conv_w_a[j]
            sinks = sinks_b[j]
            wout = w_out_even[j].astype(BF16)
            ng = norm_a[j].reshape(1, LANES)
            qkv, gb, ga, qb, kvb, gbt, c1 = _even_in(hp, w, cw, misc, rot_p, None, j, nb=1, tl=512)
            oa, delta_p = _delta(qkv, gb, None, j, delta_p, n_even, nb=1, tl=256, c=A_CHUNK)
            ob = _swa_prompt(qb, kvb, sinks)
            acts = [t.reshape(n_p * l_p, A_VW) for t in (oa, ga, ob, gbt)]
            hp = _post(hp2, acts, pp3, layer, wout, ng, lng, lnb, wproj, wgate, even=True, tr=512)
            hp = hp.reshape(n_p, l_p, D_MODEL)
            k1 = kvb[:, l_p - WINDOW:, :B_KVW].reshape(n_p, WINDOW, B_KV_HEADS, B_HD)
            v1 = kvb[:, l_p - WINDOW:, B_KVW:].reshape(n_p, WINDOW, B_KV_HEADS, B_HD)
            qkv, gb, ga, qb, kvb, gbt, c2 = _even_in(hs, w, cw, misc, rot_s, state_conv_a, j, nb=nb_s, tl=l_s)
            oa, delta_s = _delta(qkv, gb, state_delta_a, j, delta_s, n_even, nb=8, tl=l_s, c=l_s)
            ob, *win_s = _swa_sample(qb, kvb, ck_all, cv_all, sinks, j, win_s, nb=8)
            acts = [t.reshape(n_s * l_s, A_VW) for t in (oa, ga, ob, gbt)]
            hs = _post(hs2, acts, ps3, layer, wout, ng, lng, lnb, wproj, wgate, even=True, tr=512)
            hs = hs.reshape(n_s, l_s, D_MODEL)
            conv_p.append(c1); conv_s.append(c2)
            wk_p.append(k1); wv_p.append(v1)
        else:
            w = w_in_odd[j].astype(BF16)
            wout = w_out_odd[j].astype(BF16)
            ng = norm_c[j].reshape(1, LANES)
            q, k, v, lf, g = _odd_in(hp2, w, lb_raw, j, tr=512)
            shp = (n_p, l_p, D_MODEL)
            o, hg_p = _hgrn(q.reshape(shp), k.reshape(shp), v.reshape(shp), lf.reshape(shp), None, j,
                            hg_p, n_odd, nb=1, hb=4, tl=256, c=C_CHUNK)
            hp = _post(hp2, [o.reshape(n_p * l_p, D_MODEL), g], pp3, layer, wout, ng, lng, lnb, wproj, wgate,
                       even=False, tr=512)
            hp = hp.reshape(n_p, l_p, D_MODEL)
            q, k, v, lf, g = _odd_in(hs2, w, lb_raw, j, tr=256)
            shp = (n_s, l_s, D_MODEL)
            o, hg_s = _hgrn(q.reshape(shp), k.reshape(shp), v.reshape(shp), lf.reshape(shp), state_hgrn_c, j,
                            hg_s, n_odd, nb=4, hb=4, tl=l_s, c=l_s)
            hs = _post(hs2, [o.reshape(n_s * l_s, D_MODEL), g], ps3, layer, wout, ng, lng, lnb, wproj, wgate,
                       even=False, tr=512)
            hs = hs.reshape(n_s, l_s, D_MODEL)
    wk_s, wv_s = (t.reshape(n_even, n_s, WINDOW, B_KV_HEADS, B_HD) for t in win_s)
    return (hp, hs, jnp.stack(conv_p), jnp.stack(conv_s), delta_p, delta_s,
            jnp.stack(wk_p), wk_s, jnp.stack(wv_p), wv_s, hg_p, hg_s)
```

```python
import functools
import math

import jax
import jax.numpy as jnp
from jax import lax
from jax.experimental import pallas as pl
from jax.experimental.pallas import tpu as pltpu

F32 = jnp.float32
BF16 = jnp.bfloat16

D_MODEL = 1024
DEPTH = 4
PAST_LEN = 8192
PLE_DIM = 256
NORM_EPS = 1e-6
MASK_NEG = -1e30
F_FLOOR = 1e-30
DEEPNORM_ALPHA = (2 * DEPTH) ** 0.25

A_HEADS = 4
A_DK = 128
A_CONV = 4
A_QK = 512
A_VW = 512
A_CONV_CH = 1536
A_CHUNK = 64

B_HEADS = 8
B_KV_HEADS = 2
B_HD = 64
B_QW = 512
B_KVW = 128
WINDOW = 128
ROT_DIM = 16
ROPE_THETA = 500000.0

C_HEADS = 8
C_DK = 128
C_CHUNK = 64

LANES = 128
SUBLANES = 8
LOG2_E = math.log2(math.e)
VMEM_LIMIT = 56 * 1024 * 1024
SUB_ROWS = 128
POST_SUB_ROWS = 512

E_QKV = 0
E_GA = 1536
E_QB = 2048
E_KB = 2560
E_VB = 2688
E_GB = 2816
E_AB = 3328
E_TOT = 3456

NT = (((1,), (1,)), ((), ()))
TN = (((0,), (0,)), ((), ()))


def _mm(a, b):
    return jnp.dot(a.astype(BF16), b.astype(BF16), preferred_element_type=F32)


def _mm_nt(a, b):
    return lax.dot_general(a.astype(BF16), b.astype(BF16), NT, preferred_element_type=F32)


def _mm_tn(a, b):
    return lax.dot_general(a.astype(BF16), b.astype(BF16), TN, preferred_element_type=F32)


def _sigmoid(x):
    return 1.0 / (1.0 + jnp.exp(-x))


def _silu(x):
    return x * _sigmoid(x)


def _cumsum_rows(x):
    n = x.shape[0]
    row = lax.broadcasted_iota(jnp.int32, x.shape, 0)
    s = 1
    while s < n:
        x = x + jnp.where(row >= s, pltpu.roll(x, s, axis=0), 0.0)
        s *= 2
    return x


def _params(n_grid):
    return pltpu.CompilerParams(dimension_semantics=("arbitrary",) * n_grid,
                                vmem_limit_bytes=VMEM_LIMIT)


def _const_spec(shape):
    nd = len(shape)
    return pl.BlockSpec(shape, lambda *_: (0,) * nd)


def _ignore_first_ref(body, _stacked_ref, *refs):
    body(*refs)


def _stacked_spec(block, index_fn, j, n_layers, first):
    if first:
        return pl.BlockSpec((n_layers,) + block, lambda *g: (0,) + index_fn(*g))
    return pl.BlockSpec((None,) + block, lambda *g: (j,) + index_fn(*g))


def _slot_view(ref, slot):
    if slot is None:
        return ref
    j, n_layers = slot
    for i in range(n_layers):
        if i != j:
            ref[i] = jnp.zeros(ref.shape[1:], ref.dtype)
    return ref.at[j]


def _alias_stacked(body, in_specs, args, stacked, *out_indices):
    if stacked is None:
        return body, in_specs, args, {}
    if not isinstance(stacked, (list, tuple)):
        stacked = [stacked]
    for arr in reversed(stacked):
        body = functools.partial(_ignore_first_ref, body)
        in_specs = [pl.BlockSpec(memory_space=pl.ANY)] + list(in_specs)
        args = [arr] + list(args)
    return body, in_specs, args, {i: o for i, o in enumerate(out_indices)}


def _even_in_body(x_ref, w_ref, cw_ref, misc_ref, rot_ref, buf_ref,
                  qkv_ref, gb_ref, ga_ref, qb_ref, kvb_ref, gbt_ref, conv_ref, xp_ref,
                  *, nb, tl):
    t = pl.program_id(1)
    if buf_ref is not None:
        xp_ref[:, 5:8, :] = buf_ref[...]
    else:
        @pl.when(t == 0)
        def _():
            xp_ref[:, 5:8, :] = jnp.zeros((nb, 3, A_CONV_CH), F32)

    neg_a = -jnp.exp(misc_ref[0:1, :])
    dt_bias = misc_ref[1:2, :]
    half = ROT_DIM // 2
    rep = B_QW // LANES

    if nb == 1:
        nbs, ts = 1, min(tl, SUB_ROWS)
    else:
        nbs, ts = max(1, min(nb, SUB_ROWS // tl)), tl
    r = nbs * ts
    for b0 in range(0, nb, nbs):
        for r0 in range(0, tl, ts):
            bs = slice(b0, b0 + nbs)
            rs = slice(r0, r0 + ts)
            flat = slice(b0 * tl + r0, b0 * tl + r0 + r)
            x = x_ref[bs, rs, :].reshape(r, D_MODEL).astype(BF16)

            z = jnp.dot(x, w_ref[:, E_QKV:E_QKV + A_CONV_CH], preferred_element_type=F32)
            xp_ref[bs, 8 + r0:8 + r0 + ts, :] = z.reshape(nbs, ts, A_CONV_CH)
            y = xp_ref[bs, 5 + r0:5 + r0 + ts, :] * cw_ref[0:1, :].reshape(1, 1, A_CONV_CH)
            for j in range(1, A_CONV):
                y = y + xp_ref[bs, 5 + j + r0:5 + j + r0 + ts, :] * cw_ref[j:j + 1, :].reshape(1, 1, A_CONV_CH)
            y = _silu(y).reshape(r, A_CONV_CH)
            for h in range(2 * A_HEADS):
                c = y[:, h * A_DK:(h + 1) * A_DK]
                inv = lax.rsqrt(jnp.sum(c * c, axis=-1, keepdims=True) + NORM_EPS)
                if h < A_HEADS:
                    inv = inv * (A_DK ** -0.5)
                qkv_ref[bs, rs, h * A_DK:(h + 1) * A_DK] = (c * inv).reshape(nbs, ts, A_DK)
            qkv_ref[bs, rs, 2 * A_QK:] = y[:, 2 * A_QK:].reshape(nbs, ts, A_VW)

            ab = jnp.dot(x, w_ref[:, E_AB:E_AB + LANES], preferred_element_type=F32)
            sp_in = ab + dt_bias
            softplus = jnp.maximum(sp_in, 0.0) + jnp.log1p(jnp.exp(-jnp.abs(sp_in)))
            lane = lax.broadcasted_iota(jnp.int32, ab.shape, 1)
            gb = jnp.where(lane < A_HEADS, neg_a * softplus, _sigmoid(ab))
            gb_ref[bs, rs, :] = gb.reshape(nbs, ts, LANES)

            ga_ref[bs, rs, :] = _silu(jnp.dot(x, w_ref[:, E_GA:E_GA + A_VW],
                                              preferred_element_type=F32)).reshape(nbs, ts, A_VW)
            gbt_ref[bs, rs, :] = _silu(jnp.dot(x, w_ref[:, E_GB:E_GB + B_QW],
                                               preferred_element_type=F32)).reshape(nbs, ts, B_QW)

            cos_t = rot_ref[0, flat, :]
            s_up = rot_ref[1, flat, :]
            s_dn = rot_ref[2, flat, :]
            zq = jnp.dot(x, w_ref[:, E_QB:E_QB + B_QW], preferred_element_type=F32)
            cos4 = jnp.concatenate([rot_ref[3, flat, :]] * rep, axis=1)
            up4 = jnp.concatenate([rot_ref[4, flat, :]] * rep, axis=1)
            dn4 = jnp.concatenate([rot_ref[5, flat, :]] * rep, axis=1)
            qrot = zq * cos4 + pltpu.roll(zq, B_QW - half, axis=1) * up4 + pltpu.roll(zq, half, axis=1) * dn4
            qb_ref[bs, rs, :] = qrot.reshape(nbs, ts, B_QW)
            zk = jnp.dot(x, w_ref[:, E_KB:E_KB + B_KVW], preferred_element_type=F32)
            krot = zk * cos_t + pltpu.roll(zk, LANES - half, axis=1) * s_up + pltpu.roll(zk, half, axis=1) * s_dn
            kvb_ref[bs, rs, 0:B_KVW] = krot.reshape(nbs, ts, B_KVW)
            zv = jnp.dot(x, w_ref[:, E_VB:E_VB + B_KVW], preferred_element_type=F32)
            kvb_ref[bs, rs, B_KVW:] = zv.reshape(nbs, ts, B_KVW)

    last = xp_ref[:, tl + 5:tl + 8, :]
    conv_ref[...] = last
    if buf_ref is None:
        xp_ref[:, 5:8, :] = last


def _even_in_kernel_buf(x_ref, w_ref, cw_ref, misc_ref, rot_ref, buf_ref, *rest, nb, tl):
    _even_in_body(x_ref, w_ref, cw_ref, misc_ref, rot_ref, buf_ref, *rest, nb=nb, tl=tl)


def _even_in_kernel_nobuf(x_ref, w_ref, cw_ref, misc_ref, rot_ref, *rest, nb, tl):
    _even_in_body(x_ref, w_ref, cw_ref, misc_ref, rot_ref, None, *rest, nb=nb, tl=tl)


def _even_in(x, w, cw, misc, rot, buf, j, *, nb, tl):
    n, l, _ = x.shape
    grid = (n // nb, l // tl)
    r = nb * tl
    row_map = lambda i, t: (i, t, 0)
    rot_map = (lambda i, t: (0, t, 0)) if buf is None else (lambda i, t: (0, 0, 0))
    in_specs = [
        pl.BlockSpec((nb, tl, D_MODEL), row_map),
        _const_spec((D_MODEL, E_TOT)),
        _const_spec((A_CONV, A_CONV_CH)),
        _const_spec((8, LANES)),
        pl.BlockSpec((6, r, LANES), rot_map),
    ]
    args = [x, w, cw, misc, rot]
    if buf is not None:
        in_specs.append(pl.BlockSpec((None, nb, A_CONV - 1, A_CONV_CH), lambda i, t: (j, i, 0, 0)))
        args.append(buf)
        body = functools.partial(_even_in_kernel_buf, nb=nb, tl=tl)
    else:
        body = functools.partial(_even_in_kernel_nobuf, nb=nb, tl=tl)
    widths = (A_CONV_CH, LANES, A_VW, B_QW, 2 * B_KVW, B_QW)
    out_shape = [jax.ShapeDtypeStruct((n, l, wd), F32) for wd in widths]
    out_specs = [pl.BlockSpec((nb, tl, wd), row_map) for wd in widths]
    out_shape.append(jax.ShapeDtypeStruct((n, A_CONV - 1, A_CONV_CH), F32))
    out_specs.append(pl.BlockSpec((nb, A_CONV - 1, A_CONV_CH), lambda i, t: (i, 0, 0)))
    return pl.pallas_call(
        body, grid=grid, in_specs=in_specs, out_specs=out_specs, out_shape=out_shape,
        scratch_shapes=[pltpu.VMEM((nb, tl + 8, A_CONV_CH), F32)],
        compiler_params=_params(2), name="even_in",
    )(*args)


def _tri_inv_all(a_list, c):
    ri = lax.broadcasted_iota(jnp.int32, (c, LANES), 0)
    ci = lax.broadcasted_iota(jnp.int32, (c, LANES), 1)
    eye = jnp.where(ri == ci, 1.0, 0.0)
    ps = [eye for _ in a_list]
    ms = [-a for a in a_list]
    span = 1
    while span < c:
        mps = [_mm(m[:, :c], jnp.concatenate([m, p], axis=1)) for m, p in zip(ms, ps)]
        ms = [mp[:, :LANES] for mp in mps]
        ps = [p + mp[:, LANES:] for p, mp in zip(ps, mps)]
        span *= 2
    return ps


def _delta_prep_all(qs, ks, vs, g_cols, g_rows, betas, g_lasts, c):
    ri = lax.broadcasted_iota(jnp.int32, (c, LANES), 0)
    ci = lax.broadcasted_iota(jnp.int32, (c, LANES), 1)
    causal = ri >= ci
    strict = ri > ci
    decays = [jnp.where(causal, jnp.exp(jnp.where(causal, gc - gr, 0.0)), 0.0) for gc, gr in zip(g_cols, g_rows)]
    kbs = [k * b for k, b in zip(ks, betas)]
    if c < LANES:
        pad = jnp.zeros((LANES - c, A_DK), F32)
        k_rows = [jnp.concatenate([k, pad], axis=0).astype(BF16) for k in ks]
    else:
        k_rows = [k.astype(BF16) for k in ks]
    a_list = [jnp.where(strict, _mm_nt(kb, kr) * d, 0.0) for kb, kr, d in zip(kbs, k_rows, decays)]
    qks = [(_mm_nt(q, kr) * d)[:, :c].astype(BF16) for q, kr, d in zip(qs, k_rows, decays)]
    t_invs = _tri_inv_all(a_list, c)
    e_gs = [jnp.exp(gc) for gc in g_cols]
    uws = [_mm(t[:, :c], jnp.concatenate([v * b, kb * e], axis=1))
           for t, v, b, kb, e in zip(t_invs, vs, betas, kbs, e_gs)]
    us = [uw[:, :A_DK] for uw in uws]
    wqs = [jnp.concatenate([uw[:, A_DK:], q * e], axis=0).astype(BF16) for uw, q, e in zip(uws, qs, e_gs)]
    k_decs = [(k * jnp.exp(gl - gc)).astype(BF16) for k, gl, gc in zip(ks, g_lasts, g_cols)]
    e_lasts = [jnp.exp(gl) for gl in g_lasts]
    return us, wqs, qks, k_decs, e_lasts


def _delta_body(qkv_ref, gb_ref, s0_ref, o_ref, sout_ref, s_ref, *, nb, tl, c, slot):
    t = pl.program_id(1)
    nt = pl.num_programs(1)

    @pl.when(t == 0)
    def _():
        if s0_ref is None:
            s_ref[...] = jnp.zeros(s_ref.shape, F32)
        else:
            s_ref[...] = s0_ref[...]

    nc = tl // c
    qs, ks, vs, g_cols, g_rows, betas, g_lasts = [], [], [], [], [], [], []
    for ic in range(nc):
        rows = slice(ic * c, (ic + 1) * c)
        for b in range(nb):
            gbc = gb_ref[b, rows, :]
            g_all = _cumsum_rows(gbc)
            if c < LANES:
                g_t = jnp.concatenate([g_all, jnp.zeros((LANES - c, LANES), F32)], axis=0).T
            else:
                g_t = g_all.T
            for h in range(A_HEADS):
                qs.append(qkv_ref[b, rows, h * A_DK:(h + 1) * A_DK])
                ks.append(qkv_ref[b, rows, A_QK + h * A_DK:A_QK + (h + 1) * A_DK])
                vs.append(qkv_ref[b, rows, 2 * A_QK + h * A_DK:2 * A_QK + (h + 1) * A_DK])
                g_cols.append(g_all[:, h:h + 1])
                g_rows.append(g_t[h:h + 1, :])
                betas.append(gbc[:, A_HEADS + h:A_HEADS + h + 1])
                g_lasts.append(g_all[c - 1:c, h:h + 1])
    us, wqs, qks, k_decs, e_lasts = _delta_prep_all(qs, ks, vs, g_cols, g_rows, betas, g_lasts, c)

    nch = nb * A_HEADS
    states = [s_ref[b, h] for b in range(nb) for h in range(A_HEADS)]
    for ic in range(nc):
        sl = slice(ic * nch, (ic + 1) * nch)
        ws_qs = [_mm(wq, s) for wq, s in zip(wqs[sl], states)]
        v_news = [(u - x[:c]).astype(BF16) for u, x in zip(us[sl], ws_qs)]
        outs = [x[c:] + _mm(qk, vn) for x, qk, vn in zip(ws_qs, qks[sl], v_news)]
        states = [s * el + _mm_tn(kd, vn) for s, el, kd, vn in zip(states, e_lasts[sl], k_decs[sl], v_news)]
        for i, o in enumerate(outs):
            b, h = divmod(i, A_HEADS)
            o_ref[b, ic * c:(ic + 1) * c, h * A_DK:(h + 1) * A_DK] = o
    for i, s in enumerate(states):
        b, h = divmod(i, A_HEADS)
        s_ref[b, h] = s

    @pl.when(t == nt - 1)
    def _():
        _slot_view(sout_ref, slot)[...] = s_ref[...]


def _delta_kernel_s0(qkv_ref, gb_ref, s0_ref, o_ref, sout_ref, s_ref, **kw):
    _delta_body(qkv_ref, gb_ref, s0_ref, o_ref, sout_ref, s_ref, **kw)


def _delta_kernel_zero(qkv_ref, gb_ref, o_ref, sout_ref, s_ref, **kw):
    _delta_body(qkv_ref, gb_ref, None, o_ref, sout_ref, s_ref, **kw)


def _delta(qkv, gb, s0, j, stacked, n_layers, *, nb, tl, c):
    n, l, _ = qkv.shape
    grid = (n // nb, l // tl)
    row_map = lambda i, t: (i, t, 0)
    st_block = (nb, A_HEADS, A_DK, A_DK)
    st_map = lambda i, t: (i, 0, 0, 0)
    first = stacked is None
    kw = dict(nb=nb, tl=tl, c=c, slot=(j, n_layers) if first else None)
    in_specs = [pl.BlockSpec((nb, tl, A_CONV_CH), row_map), pl.BlockSpec((nb, tl, LANES), row_map)]
    args = [qkv, gb]
    if s0 is not None:
        in_specs.append(_stacked_spec(st_block, st_map, j, n_layers, False))
        args.append(s0)
        body = functools.partial(_delta_kernel_s0, **kw)
    else:
        body = functools.partial(_delta_kernel_zero, **kw)
    body, in_specs, args, aliases = _alias_stacked(body, in_specs, args, stacked, 1)
    return pl.pallas_call(
        body, grid=grid, in_specs=in_specs,
        out_specs=[pl.BlockSpec((nb, tl, A_VW), row_map), _stacked_spec(st_block, st_map, j, n_layers, first)],
        out_shape=[jax.ShapeDtypeStruct((n, l, A_VW), F32),
                   jax.ShapeDtypeStruct((n_layers, n, A_HEADS, A_DK, A_DK), F32)],
        scratch_shapes=[pltpu.VMEM(st_block, F32)], input_output_aliases=aliases,
        compiler_params=_params(2), name="delta",
    )(*args)


def _swa_tasks(qs, kxs, vxs, first_min_key, sink_ref):
    nq = qs[0].shape[0]
    nk = 2 * WINDOW
    rows = B_HEADS * nq
    lo = lax.broadcasted_iota(jnp.int32, (nq, LANES), 1) < B_HD
    ri = lax.broadcasted_iota(jnp.int32, (rows, nk), 0) % nq
    ci = lax.broadcasted_iota(jnp.int32, (rows, nk), 1)
    band = (ci >= ri) & (ci <= ri + WINDOW)
    masks = [band & (ci >= first_min_key)] + [band] * (len(qs) - 1)
    head = lax.broadcasted_iota(jnp.int32, (rows, 1), 0) // nq
    sink = jnp.zeros((rows, 1), F32)
    for h in range(B_HEADS):
        sink = jnp.where(head == h, sink_ref[h], sink)

    def stack_heads(q):
        parts = []
        for h in range(B_HEADS):
            j, e = divmod(h, 2)
            g = h // (B_HEADS // B_KV_HEADS)
            src = q[:, j * LANES:(j + 1) * LANES]
            if e != g:
                src = pltpu.roll(src, B_HD, axis=1)
            parts.append(jnp.where(lo, src, 0.0) if g == 0 else jnp.where(lo, 0.0, src))
        return jnp.concatenate(parts, axis=0).astype(BF16)

    lhs = [stack_heads(q) for q in qs]
    ss = [jnp.where(mask, _mm_nt(x, kx), MASK_NEG) for mask, x, kx in zip(masks, lhs, kxs)]
    ms = [jnp.maximum(jnp.max(s, axis=-1, keepdims=True), sink) for s in ss]
    ps = [jnp.where(mask, jnp.exp(s - m), 0.0) for mask, s, m in zip(masks, ss, ms)]
    dens = [jnp.sum(p, axis=-1, keepdims=True) + jnp.exp(sink - m) for p, m in zip(ps, ms)]
    os_ = [_mm(p, vx) / d for p, d, vx in zip(ps, dens, vxs)]
    outs = []
    for o in os_:
        cols = []
        for j in range(B_HEADS // 2):
            a = o[(2 * j) * nq:(2 * j + 1) * nq]
            b = o[(2 * j + 1) * nq:(2 * j + 2) * nq]
            if j < B_HEADS // 4:
                cols.append(jnp.where(lo, a, pltpu.roll(b, B_HD, axis=1)))
            else:
                cols.append(jnp.where(lo, pltpu.roll(a, B_HD, axis=1), b))
        outs.append(jnp.concatenate(cols, axis=1))
    return outs


def _swa_prompt_kernel(sink_ref, q_ref, kvc_ref, kvp_ref, o_ref, *, nblk):
    i = pl.program_id(1)
    kv = [kvp_ref[0].astype(BF16)] + [kvc_ref[0, t * WINDOW:(t + 1) * WINDOW, :].astype(BF16) for t in range(nblk)]
    qs = [q_ref[0, t * WINDOW:(t + 1) * WINDOW, :] for t in range(nblk)]
    kxs = [jnp.concatenate([kv[t][:, :B_KVW], kv[t + 1][:, :B_KVW]], axis=0) for t in range(nblk)]
    vxs = [jnp.concatenate([kv[t][:, B_KVW:], kv[t + 1][:, B_KVW:]], axis=0) for t in range(nblk)]
    outs = _swa_tasks(qs, kxs, vxs, jnp.where(i > 0, 0, WINDOW), sink_ref)
    for t, o in enumerate(outs):
        o_ref[0, t * WINDOW:(t + 1) * WINDOW, :] = o


def _swa_prompt(qb, kvb, sinks, *, nblk):
    n, l, _ = qb.shape
    tq = nblk * WINDOW
    return pl.pallas_call(
        functools.partial(_swa_prompt_kernel, nblk=nblk), grid=(n, l // tq),
        in_specs=[pl.BlockSpec(memory_space=pltpu.SMEM),
                  pl.BlockSpec((1, tq, B_QW), lambda b, i: (b, i, 0)),
                  pl.BlockSpec((1, tq, 2 * B_KVW), lambda b, i: (b, i, 0)),
                  pl.BlockSpec((1, WINDOW, 2 * B_KVW), lambda b, i: (b, jnp.maximum(i * nblk - 1, 0), 0))],
        out_specs=pl.BlockSpec((1, tq, B_QW), lambda b, i: (b, i, 0)),
        out_shape=jax.ShapeDtypeStruct((n, l, B_QW), F32),
        compiler_params=_params(2), name="swa_prompt",
    )(sinks, qb, kvb, kvb)


def _swa_sample_kernel(sink_ref, q_ref, kv_ref, ck_ref, cv_ref, o_ref, nk_ref, nv_ref, *, nb, ls, slot):
    nk_ref = _slot_view(nk_ref, slot)
    nv_ref = _slot_view(nv_ref, slot)
    pad = jnp.zeros((WINDOW - ls, B_KVW), F32)
    qs, kxs, vxs = [], [], []
    for b in range(nb):
        kv = kv_ref[b]
        k_new = kv[:, :B_KVW]
        v_new = kv[:, B_KVW:]
        ck = ck_ref[b]
        cv = cv_ref[b]
        qs.append(q_ref[b])
        kxs.append(jnp.concatenate([ck, k_new, pad], axis=0).astype(BF16))
        vxs.append(jnp.concatenate([cv, v_new, pad], axis=0).astype(BF16))
        nk_ref[b, 0:WINDOW - ls, :] = ck[ls:, :]
        nk_ref[b, WINDOW - ls:, :] = k_new
        nv_ref[b, 0:WINDOW - ls, :] = cv[ls:, :]
        nv_ref[b, WINDOW - ls:, :] = v_new
    for b, o in enumerate(_swa_tasks(qs, kxs, vxs, 0, sink_ref)):
        o_ref[b] = o


def _swa_sample(qb, kvb, ck, cv, sinks, j, stacked, *, nb):
    n, ls, _ = qb.shape
    row_map = lambda i: (i, 0, 0)
    n_layers = ck.shape[0]
    first = stacked is None
    cache_block = (nb, WINDOW, B_KVW)
    cache_spec = _stacked_spec(cache_block, row_map, j, n_layers, False)
    cache_out = _stacked_spec(cache_block, row_map, j, n_layers, first)
    cache_shape = jax.ShapeDtypeStruct(ck.shape, F32)
    body = functools.partial(_swa_sample_kernel, nb=nb, ls=ls, slot=(j, n_layers) if first else None)
    in_specs = [pl.BlockSpec(memory_space=pltpu.SMEM),
                pl.BlockSpec((nb, ls, B_QW), row_map),
                pl.BlockSpec((nb, ls, 2 * B_KVW), row_map),
                cache_spec, cache_spec]
    args = [sinks, qb, kvb, ck, cv]
    body, in_specs, args, aliases = _alias_stacked(body, in_specs, args, stacked, 1, 2)
    return pl.pallas_call(
        body, grid=(n // nb,), in_specs=in_specs,
        out_specs=[pl.BlockSpec((nb, ls, B_QW), row_map), cache_out, cache_out],
        out_shape=[jax.ShapeDtypeStruct((n, ls, B_QW), F32), cache_shape, cache_shape],
        input_output_aliases=aliases, compiler_params=_params(1), name="swa_sample",
    )(*args)


def _odd_in_kernel(x_ref, w_ref, lbraw_ref, q_ref, k_ref, v_ref, lf_ref, g_ref, *, j, ts):
    raw = lbraw_ref[...]
    e = jnp.exp(raw - jnp.max(raw, axis=0, keepdims=True))
    sm = e / jnp.sum(e, axis=0, keepdims=True)
    lb = jnp.zeros((1, D_MODEL), F32)
    for i in range(1, j + 1):
        lb = lb + sm[i:i + 1, :]
    d = D_MODEL
    for s in range(x_ref.shape[0] // ts):
        rows = slice(s * ts, (s + 1) * ts)
        x = x_ref[rows, :].astype(BF16)
        q_ref[rows, :] = _silu(jnp.dot(x, w_ref[:, 0:d], preferred_element_type=F32)) * (C_DK ** -0.5)
        zf = jnp.dot(x, w_ref[:, d:2 * d], preferred_element_type=F32)
        fg = lb + (1.0 - lb) * _sigmoid(zf)
        lf_ref[rows, :] = jnp.log(jnp.maximum(fg, F_FLOOR))
        k_ref[rows, :] = (1.0 - lb) * _sigmoid(-zf)
        v_ref[rows, :] = jnp.dot(x, w_ref[:, 2 * d:3 * d], preferred_element_type=F32)
        g_ref[rows, :] = _silu(jnp.dot(x, w_ref[:, 3 * d:4 * d], preferred_element_type=F32))


def _odd_in(x2, w, lb_raw, j, *, tr, ts=SUB_ROWS):
    rows = x2.shape[0]
    n_odd = lb_raw.shape[0]
    row_spec = pl.BlockSpec((tr, D_MODEL), lambda i: (i, 0))
    shp = jax.ShapeDtypeStruct((rows, D_MODEL), F32)
    return pl.pallas_call(
        functools.partial(_odd_in_kernel, j=j, ts=min(ts, tr)), grid=(rows // tr,),
        in_specs=[row_spec, _const_spec((D_MODEL, 4 * D_MODEL)), _const_spec((n_odd, D_MODEL))],
        out_specs=[row_spec] * 5, out_shape=[shp] * 5,
        compiler_params=_params(1), name="odd_in",
    )(x2, w, lb_raw)


def _hgrn_prep_all(qs, ks, vs, lfs, c, state_t):
    ng = c // SUBLANES
    shape3 = (ng, SUBLANES, C_DK)
    sub = lax.broadcasted_iota(jnp.int32, shape3, 1)

    def grouped(x):
        return x.reshape(shape3)

    def flat(x):
        return x.reshape(c, C_DK)

    gs = []
    for lf in lfs:
        x = grouped(lf * LOG2_E)
        s = 1
        while s < SUBLANES:
            x = x + jnp.where(sub >= s, pltpu.roll(x, s, axis=1), 0.0)
            s *= 2
        if ng > 1:
            tot = x[:, SUBLANES - 1:SUBLANES, :]
            run = jnp.zeros((1, 1, C_DK), F32)
            offs = []
            for i in range(ng):
                offs.append(run)
                run = run + tot[i:i + 1]
            x = x + jnp.concatenate(offs, axis=0)
        gs.append(x)
    rbs = [jnp.broadcast_to(g[:, 0:1, :], shape3) for g in gs]

    def block_start(rb, m):
        x = rb.reshape(ng // m, m, SUBLANES, C_DK)[:, 0:1]
        return jnp.broadcast_to(x, (ng // m, m, SUBLANES, C_DK)).reshape(shape3)

    ri = lax.broadcasted_iota(jnp.int32, (c, c), 0)
    ci = lax.broadcasted_iota(jnp.int32, (c, c), 1)
    q3s = [grouped(q) for q in qs]
    k3s = [grouped(k) for k in ks]
    accs = None
    bs = list(gs)
    h = 1
    while h < c:
        if h < SUBLANES:
            bns = [pltpu.roll(b, SUBLANES - h, axis=1) for b in bs]
        else:
            m = h // SUBLANES
            bs = [block_start(rb, m) for rb in rbs]
            bns = [jnp.concatenate([b[m:], g[ng - m:]], axis=0) for b, g in zip(bs, gs)]
        qhs = q3s if h == 1 else [q * jnp.exp2(g - b) for q, g, b in zip(q3s, gs, bs)]
        khs = [k * jnp.exp2(bn - g) for k, bn, g in zip(k3s, bns, gs)]
        lvl = ((ri // h) == (ci // h) + 1) & ((ri // (2 * h)) == (ci // (2 * h)))
        if h == 1 and ng > 1:
            gap = jnp.zeros((LANES - c, C_DK), F32)
            both = [_mm_nt(flat(q), jnp.concatenate([flat(k), gap, flat(kh)], axis=0))
                    for q, k, kh in zip(q3s, k3s, khs)]
            accs = [jnp.where(ri == ci, x[:, :c], jnp.where(lvl, x[:, LANES:], 0.0)) for x in both]
        elif h == 1:
            accs = [jnp.where(ri == ci, _mm_nt(flat(q), flat(k)), jnp.where(lvl, _mm_nt(flat(q), flat(kh)), 0.0))
                    for q, k, kh in zip(q3s, k3s, khs)]
        else:
            accs = [jnp.where(lvl, _mm_nt(flat(qh), flat(kh)), a) for a, qh, kh in zip(accs, qhs, khs)]
        if 2 * h < SUBLANES:
            sel = (sub & h) != 0
            bs = [jnp.where(sel, pltpu.roll(b, h, axis=1), b) for b in bs]
        h *= 2
    if accs is None:
        accs = [jnp.where(ri == ci, _mm_nt(q, k), 0.0) for q, k in zip(qs, ks)]
    g_lasts = [g[ng - 1, SUBLANES - 1:SUBLANES, :] for g in gs]
    intra = [_mm(a, v) for a, v in zip(accs, vs)]
    q_decs = [flat(q * jnp.exp2(g)).astype(BF16) for q, g in zip(q3s, gs)]
    k_decs = [flat(k * jnp.exp2(gl - g)) for k, gl, g in zip(k3s, g_lasts, gs)]
    e_lasts = [jnp.exp2(gl) for gl in g_lasts]
    if state_t:
        kvs = [_mm_tn(v, kd) for v, kd in zip(vs, k_decs)]
    else:
        kvs = [_mm_tn(kd, v) for v, kd in zip(vs, k_decs)]
        e_lasts = [jnp.broadcast_to(el, (C_DK, C_DK)).T for el in e_lasts]
    return intra, q_decs, kvs, e_lasts


def _hgrn_body(q_ref, k_ref, v_ref, lf_ref, s0_ref, o_ref, sout_ref, st_ref, *, nb, hb, tl, c, slot):
    t = pl.program_id(2)
    nt = pl.num_programs(2)
    nc = tl // c
    state_t = s0_ref is None

    @pl.when(t == 0)
    def _():
        if s0_ref is None:
            st_ref[...] = jnp.zeros(st_ref.shape, F32)
        else:
            st_ref[...] = s0_ref[...]

    where = [(ic, b, h) for ic in range(nc) for b in range(nb) for h in range(hb)]

    def load(ref):
        return [ref[b, ic * c:(ic + 1) * c, h * C_DK:(h + 1) * C_DK] for ic, b, h in where]

    intra, q_decs, kvs, e_lasts = _hgrn_prep_all(load(q_ref), load(k_ref), load(v_ref), load(lf_ref), c, state_t)
    nch = nb * hb
    states = [st_ref[b, h] for b in range(nb) for h in range(hb)]
    read_state = _mm_nt if state_t else _mm
    for ic in range(nc):
        sl = slice(ic * nch, (ic + 1) * nch)
        outs = [read_state(qd, st) + x for qd, st, x in zip(q_decs[sl], states, intra[sl])]
        states = [st * el + kv for st, el, kv in zip(states, e_lasts[sl], kvs[sl])]
        for i, o in enumerate(outs):
            b, h = divmod(i, hb)
            o_ref[b, ic * c:(ic + 1) * c, h * C_DK:(h + 1) * C_DK] = o
    for i, st in enumerate(states):
        b, h = divmod(i, hb)
        st_ref[b, h] = st

    @pl.when(t == nt - 1)
    def _():
        out = _slot_view(sout_ref, slot)
        for b in range(nb):
            for h in range(hb):
                out[b, h] = st_ref[b, h].T if state_t else st_ref[b, h]


def _hgrn_kernel_s0(q_ref, k_ref, v_ref, lf_ref, s0_ref, o_ref, sout_ref, st_ref, **kw):
    _hgrn_body(q_ref, k_ref, v_ref, lf_ref, s0_ref, o_ref, sout_ref, st_ref, **kw)


def _hgrn_kernel_zero(q_ref, k_ref, v_ref, lf_ref, o_ref, sout_ref, st_ref, **kw):
    _hgrn_body(q_ref, k_ref, v_ref, lf_ref, None, o_ref, sout_ref, st_ref, **kw)


def _hgrn(q, k, v, lf, s0, j, stacked, n_layers, *, nb, hb, tl, c):
    n, l, _ = q.shape
    grid = (n // nb, C_HEADS // hb, l // tl)
    row_spec = pl.BlockSpec((nb, tl, hb * C_DK), lambda i, g, t: (i, t, g))
    st_block = (nb, hb, C_DK, C_DK)
    st_map = lambda i, g, t: (i, g, 0, 0)
    first = stacked is None
    st_spec = _stacked_spec(st_block, st_map, j, n_layers, first)
    in_specs = [row_spec] * 4
    args = [q, k, v, lf]
    kw = dict(nb=nb, hb=hb, tl=tl, c=c, slot=(j, n_layers) if first else None)
    if s0 is not None:
        in_specs = in_specs + [_stacked_spec(st_block, st_map, j, n_layers, False)]
        args.append(s0)
        body = functools.partial(_hgrn_kernel_s0, **kw)
    else:
        body = functools.partial(_hgrn_kernel_zero, **kw)
    body, in_specs, args, aliases = _alias_stacked(body, in_specs, args, stacked, 1)
    return pl.pallas_call(
        body, grid=grid, in_specs=in_specs, out_specs=[row_spec, st_spec],
        out_shape=[jax.ShapeDtypeStruct((n, l, D_MODEL), F32),
                   jax.ShapeDtypeStruct((n_layers, n, C_HEADS, C_DK, C_DK), F32)],
        scratch_shapes=[pltpu.VMEM(st_block, F32)], input_output_aliases=aliases,
        compiler_params=_params(3), name="hgrn",
    )(*args)


def _group_rms(o, g_row, heads):
    outs = []
    for h in range(heads):
        c = o[:, h * LANES:(h + 1) * LANES]
        inv = lax.rsqrt(jnp.mean(c * c, axis=-1, keepdims=True) + NORM_EPS)
        outs.append(c * inv * g_row)
    return jnp.concatenate(outs, axis=1)


def _post_tail(x, y, p, lng_ref, lnb_ref, wproj_ref, wgate_ref):
    pre = DEEPNORM_ALPHA * x + y
    mu = jnp.mean(pre, axis=-1, keepdims=True)
    xc = pre - mu
    var = jnp.mean(xc * xc, axis=-1, keepdims=True)
    hn = xc * lax.rsqrt(var + NORM_EPS) * lng_ref[...] + lnb_ref[...]
    gate = _sigmoid(jnp.dot(hn.astype(BF16), wgate_ref[...], preferred_element_type=F32))
    emb = jnp.dot(p.astype(BF16), wproj_ref[...], preferred_element_type=F32)
    return hn + gate * emb


def _sub_tiles(n_rows):
    ts = min(n_rows, POST_SUB_ROWS)
    return [slice(r0, r0 + ts) for r0 in range(0, n_rows, ts)]


def _post_even_kernel(x_ref, oa_ref, ga_ref, ob_ref, gbt_ref, p_ref, wout_ref, ng_ref, lng_ref, lnb_ref,
                      wproj_ref, wgate_ref, out_ref):
    for rows in _sub_tiles(x_ref.shape[0]):
        a = _group_rms(oa_ref[rows, :], ng_ref[...], A_HEADS) * ga_ref[rows, :]
        b = ob_ref[rows, :] * gbt_ref[rows, :]
        y = (jnp.dot(a.astype(BF16), wout_ref[0:A_VW, :], preferred_element_type=F32)
             + jnp.dot(b.astype(BF16), wout_ref[A_VW:, :], preferred_element_type=F32))
        out_ref[rows, :] = _post_tail(x_ref[rows, :], y, p_ref[rows, :], lng_ref, lnb_ref, wproj_ref, wgate_ref)


def _post_odd_kernel(x_ref, o_ref, g_ref, p_ref, wout_ref, ng_ref, lng_ref, lnb_ref,
                     wproj_ref, wgate_ref, out_ref):
    for rows in _sub_tiles(x_ref.shape[0]):
        a = _group_rms(o_ref[rows, :], ng_ref[...], C_HEADS) * g_ref[rows, :]
        y = jnp.dot(a.astype(BF16), wout_ref[...], preferred_element_type=F32)
        out_ref[rows, :] = _post_tail(x_ref[rows, :], y, p_ref[rows, :], lng_ref, lnb_ref, wproj_ref, wgate_ref)


def _post(x2, acts, p3, layer, wout, ng, lng, lnb, wproj, wgate, *, even, tr):
    rows = x2.shape[0]
    full = pl.BlockSpec((tr, D_MODEL), lambda i: (i, 0))
    half = pl.BlockSpec((tr, A_VW), lambda i: (i, 0))
    act_specs = [half] * 4 if even else [full] * 2
    in_specs = ([full] + act_specs + [pl.BlockSpec((None, tr, PLE_DIM), lambda i: (layer, i, 0)),
                                      _const_spec((D_MODEL, D_MODEL)), _const_spec((1, LANES)),
                                      _const_spec((1, D_MODEL)), _const_spec((1, D_MODEL)),
                                      _const_spec((PLE_DIM, D_MODEL)), _const_spec((D_MODEL, D_MODEL))])
    return pl.pallas_call(
        _post_even_kernel if even else _post_odd_kernel, grid=(rows // tr,),
        in_specs=in_specs, out_specs=full, out_shape=jax.ShapeDtypeStruct((rows, D_MODEL), F32),
        compiler_params=_params(1), name="post_even" if even else "post_odd",
    )(x2, *acts, p3, wout, ng, lng, lnb, wproj, wgate)


def _pack_even_w(w):
    qkv = w[:, 0:1536]
    ga = w[:, 1536:2048]
    ab = w[:, 2048:2056]
    qb = w[:, 2056:2568]
    kb = w[:, 2568:2696]
    vb = w[:, 2696:2824]
    gbt = w[:, 2824:3336]
    ab = jnp.pad(ab, ((0, 0), (0, LANES - 2 * A_HEADS)))
    return jnp.concatenate([qkv, ga, qb, kb, vb, gbt, ab], axis=1).astype(BF16)


def _rot_tables(pos, reps):
    inv = ROPE_THETA ** (-jnp.arange(0, ROT_DIM, 2, dtype=F32) / ROT_DIM)
    ang = pos.astype(F32)[:, None] * inv[None, :]
    cos = jnp.cos(ang)
    sin = jnp.sin(ang)
    n = pos.shape[0]
    half = ROT_DIM // 2
    cos64 = jnp.concatenate([cos, cos, jnp.ones((n, B_HD - ROT_DIM), F32)], axis=1)
    up64 = jnp.concatenate([-sin, jnp.zeros((n, B_HD - half), F32)], axis=1)
    dn64 = jnp.concatenate([jnp.zeros((n, half), F32), sin, jnp.zeros((n, B_HD - ROT_DIM), F32)], axis=1)
    tabs = [jnp.tile(t, (reps, LANES // B_HD)) for t in (cos64, up64, dn64)]
    return jnp.stack(tabs + [t * (B_HD ** -0.5) for t in tabs])


def kernel(x_prompt, x_sample, state_conv_a, state_delta_a, cache_win_k, cache_win_v, state_hgrn_c,
           p_prompt, p_sample, w_in_even, conv_w_a, a_log, dt_bias, norm_a, sinks_b, w_out_even,
           w_in_odd, lb_raw, norm_c, w_out_odd, ln_g, ln_b, w_ple_proj, w_ple_gate):
    n_p, l_p, _ = x_prompt.shape
    n_s, l_s, _ = x_sample.shape
    nb_s = 32
    rot_p = _rot_tables(jnp.arange(l_p), 1)
    rot_s = _rot_tables(PAST_LEN + jnp.arange(l_s), nb_s)
    n_even, n_odd = w_in_even.shape[0], w_in_odd.shape[0]
    conv_p, conv_s, wk_p, wv_p = [], [], [], []
    delta_p = delta_s = hg_p = hg_s = win_s = None
    hp, hs = x_prompt, x_sample
    pp3 = p_prompt.reshape(DEPTH, n_p * l_p, PLE_DIM)
    ps3 = p_sample.reshape(DEPTH, n_s * l_s, PLE_DIM)
    ck_all = cache_win_k.reshape(-1, n_s, WINDOW, B_KVW)
    cv_all = cache_win_v.reshape(-1, n_s, WINDOW, B_KVW)
    for layer in range(DEPTH):
        j = layer // 2
        lng = ln_g[layer].reshape(1, D_MODEL)
        lnb = ln_b[layer].reshape(1, D_MODEL)
        wproj = w_ple_proj[layer].astype(BF16)
        wgate = w_ple_gate[layer].astype(BF16)
        hp2 = hp.reshape(n_p * l_p, D_MODEL)
        hs2 = hs.reshape(n_s * l_s, D_MODEL)
        if layer % 2 == 0:
            w = _pack_even_w(w_in_even[j])
            misc = jnp.zeros((8, LANES), F32)
            misc = misc.at[0, :A_HEADS].set(a_log[j]).at[1, :A_HEADS].set(dt_bias[j])
            cw = conv_w_a[j]
            sinks = sinks_b[j]
            wout = w_out_even[j].astype(BF16)
            ng = norm_a[j].reshape(1, LANES)
            qkv, gb, ga, qb, kvb, gbt, c1 = _even_in(hp, w, cw, misc, rot_p, None, j, nb=1, tl=512)
            oa, delta_p = _delta(qkv, gb, None, j, delta_p, n_even, nb=1, tl=512, c=A_CHUNK)
            ob = _swa_prompt(qb, kvb, sinks, nblk=4)
            acts = [t.reshape(n_p * l_p, A_VW) for t in (oa, ga, ob, gbt)]
            hp = _post(hp2, acts, pp3, layer, wout, ng, lng, lnb, wproj, wgate, even=True, tr=512)
            hp = hp.reshape(n_p, l_p, D_MODEL)
            k1 = kvb[:, l_p - WINDOW:, :B_KVW].reshape(n_p, WINDOW, B_KV_HEADS, B_HD)
            v1 = kvb[:, l_p - WINDOW:, B_KVW:].reshape(n_p, WINDOW, B_KV_HEADS, B_HD)
            qkv, gb, ga, qb, kvb, gbt, c2 = _even_in(hs, w, cw, misc, rot_s, state_conv_a, j, nb=nb_s, tl=l_s)
            oa, delta_s = _delta(qkv, gb, state_delta_a, j, delta_s, n_even, nb=8, tl=l_s, c=l_s)
            ob, *win_s = _swa_sample(qb, kvb, ck_all, cv_all, sinks, j, win_s, nb=8)
            acts = [t.reshape(n_s * l_s, A_VW) for t in (oa, ga, ob, gbt)]
            hs = _post(hs2, acts, ps3, layer, wout, ng, lng, lnb, wproj, wgate, even=True, tr=512)
            hs = hs.reshape(n_s, l_s, D_MODEL)
            conv_p.append(c1); conv_s.append(c2)
            wk_p.append(k1); wv_p.append(v1)
        else:
            w = w_in_odd[j].astype(BF16)
            wout = w_out_odd[j].astype(BF16)
            ng = norm_c[j].reshape(1, LANES)
            q, k, v, lf, g = _odd_in(hp2, w, lb_raw, j, tr=512)
            shp = (n_p, l_p, D_MODEL)
            o, hg_p = _hgrn(q.reshape(shp), k.reshape(shp), v.reshape(shp), lf.reshape(shp), None, j,
                            hg_p, n_odd, nb=1, hb=8, tl=256, c=C_CHUNK)
            hp = _post(hp2, [o.reshape(n_p * l_p, D_MODEL), g], pp3, layer, wout, ng, lng, lnb, wproj, wgate,
                       even=False, tr=512)
            hp = hp.reshape(n_p, l_p, D_MODEL)
            q, k, v, lf, g = _odd_in(hs2, w, lb_raw, j, tr=256)
            shp = (n_s, l_s, D_MODEL)
            o, hg_s = _hgrn(q.reshape(shp), k.reshape(shp), v.reshape(shp), lf.reshape(shp), state_hgrn_c, j,
                            hg_s, n_odd, nb=4, hb=8, tl=l_s, c=l_s)
            hs = _post(hs2, [o.reshape(n_s * l_s, D_MODEL), g], ps3, layer, wout, ng, lng, lnb, wproj, wgate,
                       even=False, tr=512)
            hs = hs.reshape(n_s, l_s, D_MODEL)
    wk_s, wv_s = (t.reshape(n_even, n_s, WINDOW, B_KV_HEADS, B_HD) for t in win_s)
    return (hp, hs, jnp.stack(conv_p), jnp.stack(conv_s), delta_p, delta_s,
            jnp.stack(wk_p), wk_s, jnp.stack(wv_p), wv_s, hg_p, hg_s)
```

```python
import functools
import math

import jax
import jax.numpy as jnp
from jax import lax
from jax.experimental import pallas as pl
from jax.experimental.pallas import tpu as pltpu

F32 = jnp.float32
BF16 = jnp.bfloat16

D_MODEL = 1024
DEPTH = 4
PAST_LEN = 8192
PLE_DIM = 256
NORM_EPS = 1e-6
MASK_NEG = -1e30
F_FLOOR = 1e-30
DEEPNORM_ALPHA = (2 * DEPTH) ** 0.25

A_HEADS = 4
A_DK = 128
A_CONV = 4
A_QK = 512
A_VW = 512
A_CONV_CH = 1536
A_CHUNK = 64

B_HEADS = 8
B_KV_HEADS = 2
B_HD = 64
B_QW = 512
B_KVW = 128
WINDOW = 128
ROT_DIM = 16
ROPE_THETA = 500000.0

C_HEADS = 8
C_DK = 128
C_CHUNK = 64

LANES = 128
SUBLANES = 8
LOG2_E = math.log2(math.e)
VMEM_LIMIT = 56 * 1024 * 1024
SUB_ROWS = 128
POST_SUB_ROWS = 512

E_QKV = 0
E_GA = 1536
E_QB = 2048
E_KB = 2560
E_VB = 2688
E_GB = 2816
E_AB = 3328
E_TOT = 3456

NT = (((1,), (1,)), ((), ()))
TN = (((0,), (0,)), ((), ()))


def _mm(a, b):
    return jnp.dot(a.astype(BF16), b.astype(BF16), preferred_element_type=F32)


def _mm_nt(a, b):
    return lax.dot_general(a.astype(BF16), b.astype(BF16), NT, preferred_element_type=F32)


def _mm_tn(a, b):
    return lax.dot_general(a.astype(BF16), b.astype(BF16), TN, preferred_element_type=F32)


def _sigmoid(x):
    return 1.0 / (1.0 + jnp.exp(-x))


def _silu(x):
    return x * _sigmoid(x)


def _cumsum_rows(x):
    n = x.shape[0]
    row = lax.broadcasted_iota(jnp.int32, x.shape, 0)
    s = 1
    while s < n:
        x = x + jnp.where(row >= s, pltpu.roll(x, s, axis=0), 0.0)
        s *= 2
    return x


def _params(n_grid):
    return pltpu.CompilerParams(dimension_semantics=("arbitrary",) * n_grid,
                                vmem_limit_bytes=VMEM_LIMIT)


def _const_spec(shape):
    nd = len(shape)
    return pl.BlockSpec(shape, lambda *_: (0,) * nd)


def _ignore_first_ref(body, _stacked_ref, *refs):
    body(*refs)


def _stacked_spec(block, index_fn, j, n_layers, first):
    if first:
        return pl.BlockSpec((n_layers,) + block, lambda *g: (0,) + index_fn(*g))
    return pl.BlockSpec((None,) + block, lambda *g: (j,) + index_fn(*g))


def _slot_view(ref, slot):
    if slot is None:
        return ref
    j, n_layers = slot
    for i in range(n_layers):
        if i != j:
            ref[i] = jnp.zeros(ref.shape[1:], ref.dtype)
    return ref.at[j]


def _alias_stacked(body, in_specs, args, stacked, *out_indices):
    if stacked is None:
        return body, in_specs, args, {}
    if not isinstance(stacked, (list, tuple)):
        stacked = [stacked]
    for arr in reversed(stacked):
        body = functools.partial(_ignore_first_ref, body)
        in_specs = [pl.BlockSpec(memory_space=pl.ANY)] + list(in_specs)
        args = [arr] + list(args)
    return body, in_specs, args, {i: o for i, o in enumerate(out_indices)}


def _even_in_body(x_ref, w_ref, cw_ref, misc_ref, rot_ref, buf_ref,
                  qkv_ref, gb_ref, ga_ref, qb_ref, kvb_ref, gbt_ref, conv_ref, xp_ref,
                  *, nb, tl):
    t = pl.program_id(1)
    if buf_ref is not None:
        xp_ref[:, 5:8, :] = buf_ref[...]
    else:
        @pl.when(t == 0)
        def _():
            xp_ref[:, 5:8, :] = jnp.zeros((nb, 3, A_CONV_CH), F32)

    neg_a = -jnp.exp(misc_ref[0:1, :])
    dt_bias = misc_ref[1:2, :]
    half = ROT_DIM // 2
    rep = B_QW // LANES

    if nb == 1:
        nbs, ts = 1, min(tl, SUB_ROWS)
    else:
        nbs, ts = max(1, min(nb, SUB_ROWS // tl)), tl
    r = nbs * ts
    for b0 in range(0, nb, nbs):
        for r0 in range(0, tl, ts):
            bs = slice(b0, b0 + nbs)
            rs = slice(r0, r0 + ts)
            flat = slice(b0 * tl + r0, b0 * tl + r0 + r)
            x = x_ref[bs, rs, :].reshape(r, D_MODEL).astype(BF16)

            cw_blk = 2 * A_DK
            for c0 in range(0, A_CONV_CH, cw_blk):
                cols = slice(c0, c0 + cw_blk)
                z = jnp.dot(x, w_ref[:, E_QKV + c0:E_QKV + c0 + cw_blk], preferred_element_type=F32)
                xp_ref[bs, 8 + r0:8 + r0 + ts, cols] = z.reshape(nbs, ts, cw_blk)
                y = xp_ref[bs, 5 + r0:5 + r0 + ts, cols] * cw_ref[0:1, cols].reshape(1, 1, cw_blk)
                for j in range(1, A_CONV):
                    y = y + xp_ref[bs, 5 + j + r0:5 + j + r0 + ts, cols] * cw_ref[j:j + 1, cols].reshape(1, 1, cw_blk)
                y = _silu(y).reshape(r, cw_blk)
                for h0 in range(0, cw_blk, A_DK):
                    c = y[:, h0:h0 + A_DK]
                    if c0 + h0 < 2 * A_QK:
                        inv = lax.rsqrt(jnp.sum(c * c, axis=-1, keepdims=True) + NORM_EPS)
                        if c0 + h0 < A_QK:
                            inv = inv * (A_DK ** -0.5)
                        c = c * inv
                    qkv_ref[bs, rs, c0 + h0:c0 + h0 + A_DK] = c.reshape(nbs, ts, A_DK)

            ab = jnp.dot(x, w_ref[:, E_AB:E_AB + LANES], preferred_element_type=F32)
            sp_in = ab + dt_bias
            softplus = jnp.maximum(sp_in, 0.0) + jnp.log1p(jnp.exp(-jnp.abs(sp_in)))
            lane = lax.broadcasted_iota(jnp.int32, ab.shape, 1)
            gb = jnp.where(lane < A_HEADS, neg_a * softplus, _sigmoid(ab))
            gb_ref[bs, rs, :] = gb.reshape(nbs, ts, LANES)

            ga_ref[bs, rs, :] = _silu(jnp.dot(x, w_ref[:, E_GA:E_GA + A_VW],
                                              preferred_element_type=F32)).reshape(nbs, ts, A_VW)
            gbt_ref[bs, rs, :] = _silu(jnp.dot(x, w_ref[:, E_GB:E_GB + B_QW],
                                               preferred_element_type=F32)).reshape(nbs, ts, B_QW)

            cos_t = rot_ref[0, flat, :]
            s_up = rot_ref[1, flat, :]
            s_dn = rot_ref[2, flat, :]
            zq = jnp.dot(x, w_ref[:, E_QB:E_QB + B_QW], preferred_element_type=F32)
            cos4 = jnp.concatenate([rot_ref[3, flat, :]] * rep, axis=1)
            up4 = jnp.concatenate([rot_ref[4, flat, :]] * rep, axis=1)
            dn4 = jnp.concatenate([rot_ref[5, flat, :]] * rep, axis=1)
            qrot = zq * cos4 + pltpu.roll(zq, B_QW - half, axis=1) * up4 + pltpu.roll(zq, half, axis=1) * dn4
            qb_ref[bs, rs, :] = qrot.reshape(nbs, ts, B_QW)
            zk = jnp.dot(x, w_ref[:, E_KB:E_KB + B_KVW], preferred_element_type=F32)
            krot = zk * cos_t + pltpu.roll(zk, LANES - half, axis=1) * s_up + pltpu.roll(zk, half, axis=1) * s_dn
            kvb_ref[bs, rs, 0:B_KVW] = krot.reshape(nbs, ts, B_KVW)
            zv = jnp.dot(x, w_ref[:, E_VB:E_VB + B_KVW], preferred_element_type=F32)
            kvb_ref[bs, rs, B_KVW:] = zv.reshape(nbs, ts, B_KVW)

    last = xp_ref[:, tl + 5:tl + 8, :]
    conv_ref[...] = last
    if buf_ref is None:
        xp_ref[:, 5:8, :] = last


def _even_in_kernel_buf(x_ref, w_ref, cw_ref, misc_ref, rot_ref, buf_ref, *rest, nb, tl):
    _even_in_body(x_ref, w_ref, cw_ref, misc_ref, rot_ref, buf_ref, *rest, nb=nb, tl=tl)


def _even_in_kernel_nobuf(x_ref, w_ref, cw_ref, misc_ref, rot_ref, *rest, nb, tl):
    _even_in_body(x_ref, w_ref, cw_ref, misc_ref, rot_ref, None, *rest, nb=nb, tl=tl)


def _even_in(x, w, cw, misc, rot, buf, j, *, nb, tl):
    n, l, _ = x.shape
    grid = (n // nb, l // tl)
    r = nb * tl
    row_map = lambda i, t: (i, t, 0)
    rot_map = (lambda i, t: (0, t, 0)) if buf is None else (lambda i, t: (0, 0, 0))
    in_specs = [
        pl.BlockSpec((nb, tl, D_MODEL), row_map),
        _const_spec((D_MODEL, E_TOT)),
        _const_spec((A_CONV, A_CONV_CH)),
        _const_spec((8, LANES)),
        pl.BlockSpec((6, r, LANES), rot_map),
    ]
    args = [x, w, cw, misc, rot]
    if buf is not None:
        in_specs.append(pl.BlockSpec((None, nb, A_CONV - 1, A_CONV_CH), lambda i, t: (j, i, 0, 0)))
        args.append(buf)
        body = functools.partial(_even_in_kernel_buf, nb=nb, tl=tl)
    else:
        body = functools.partial(_even_in_kernel_nobuf, nb=nb, tl=tl)
    widths = (A_CONV_CH, LANES, A_VW, B_QW, 2 * B_KVW, B_QW)
    out_shape = [jax.ShapeDtypeStruct((n, l, wd), F32) for wd in widths]
    out_specs = [pl.BlockSpec((nb, tl, wd), row_map) for wd in widths]
    out_shape.append(jax.ShapeDtypeStruct((n, A_CONV - 1, A_CONV_CH), F32))
    out_specs.append(pl.BlockSpec((nb, A_CONV - 1, A_CONV_CH), lambda i, t: (i, 0, 0)))
    return pl.pallas_call(
        body, grid=grid, in_specs=in_specs, out_specs=out_specs, out_shape=out_shape,
        scratch_shapes=[pltpu.VMEM((nb, tl + 8, A_CONV_CH), F32)],
        compiler_params=_params(2), name="even_in",
    )(*args)


def _tri_inv_all(a_list, c):
    ri = lax.broadcasted_iota(jnp.int32, (c, LANES), 0)
    ci = lax.broadcasted_iota(jnp.int32, (c, LANES), 1)
    eye = jnp.where(ri == ci, 1.0, 0.0)
    ps = [eye for _ in a_list]
    ms = [-a for a in a_list]
    span = 1
    while span < c:
        mps = [_mm(m[:, :c], jnp.concatenate([m, p], axis=1)) for m, p in zip(ms, ps)]
        ms = [mp[:, :LANES] for mp in mps]
        ps = [p + mp[:, LANES:] for p, mp in zip(ps, mps)]
        span *= 2
    return ps


def _delta_prep_all(qs, ks, vs, g_cols, g_rows, betas, g_lasts, c):
    ri = lax.broadcasted_iota(jnp.int32, (c, LANES), 0)
    ci = lax.broadcasted_iota(jnp.int32, (c, LANES), 1)
    causal = ri >= ci
    strict = ri > ci
    decays = [jnp.where(causal, jnp.exp(jnp.where(causal, gc - gr, 0.0)), 0.0) for gc, gr in zip(g_cols, g_rows)]
    kbs = [k * b for k, b in zip(ks, betas)]
    if c < LANES:
        pad = jnp.zeros((LANES - c, A_DK), F32)
        k_rows = [jnp.concatenate([k, pad], axis=0).astype(BF16) for k in ks]
    else:
        k_rows = [k.astype(BF16) for k in ks]
    a_list = [jnp.where(strict, _mm_nt(kb, kr) * d, 0.0) for kb, kr, d in zip(kbs, k_rows, decays)]
    qks = [(_mm_nt(q, kr) * d)[:, :c].astype(BF16) for q, kr, d in zip(qs, k_rows, decays)]
    t_invs = _tri_inv_all(a_list, c)
    e_gs = [jnp.exp(gc) for gc in g_cols]
    uws = [_mm(t[:, :c], jnp.concatenate([v * b, kb * e], axis=1))
           for t, v, b, kb, e in zip(t_invs, vs, betas, kbs, e_gs)]
    us = [uw[:, :A_DK] for uw in uws]
    wqs = [jnp.concatenate([uw[:, A_DK:], q * e], axis=0).astype(BF16) for uw, q, e in zip(uws, qs, e_gs)]
    k_decs = [(k * jnp.exp(gl - gc)).astype(BF16) for k, gl, gc in zip(ks, g_lasts, g_cols)]
    e_lasts = [jnp.exp(gl) for gl in g_lasts]
    return us, wqs, qks, k_decs, e_lasts


def _delta_body(qkv_ref, gb_ref, s0_ref, o_ref, sout_ref, s_ref, *, nb, tl, c, slot):
    t = pl.program_id(1)
    nt = pl.num_programs(1)

    @pl.when(t == 0)
    def _():
        if s0_ref is None:
            s_ref[...] = jnp.zeros(s_ref.shape, F32)
        else:
            s_ref[...] = s0_ref[...]

    nc = tl // c
    qs, ks, vs, g_cols, g_rows, betas, g_lasts = [], [], [], [], [], [], []
    for ic in range(nc):
        rows = slice(ic * c, (ic + 1) * c)
        for b in range(nb):
            gbc = gb_ref[b, rows, :]
            g_all = _cumsum_rows(gbc)
            if c < LANES:
                g_t = jnp.concatenate([g_all, jnp.zeros((LANES - c, LANES), F32)], axis=0).T
            else:
                g_t = g_all.T
            for h in range(A_HEADS):
                qs.append(qkv_ref[b, rows, h * A_DK:(h + 1) * A_DK])
                ks.append(qkv_ref[b, rows, A_QK + h * A_DK:A_QK + (h + 1) * A_DK])
                vs.append(qkv_ref[b, rows, 2 * A_QK + h * A_DK:2 * A_QK + (h + 1) * A_DK])
                g_cols.append(g_all[:, h:h + 1])
                g_rows.append(g_t[h:h + 1, :])
                betas.append(gbc[:, A_HEADS + h:A_HEADS + h + 1])
                g_lasts.append(g_all[c - 1:c, h:h + 1])
    us, wqs, qks, k_decs, e_lasts = _delta_prep_all(qs, ks, vs, g_cols, g_rows, betas, g_lasts, c)

    nch = nb * A_HEADS
    states = [s_ref[b, h] for b in range(nb) for h in range(A_HEADS)]
    for ic in range(nc):
        sl = slice(ic * nch, (ic + 1) * nch)
        ws_qs = [_mm(wq, s) for wq, s in zip(wqs[sl], states)]
        v_news = [(u - x[:c]).astype(BF16) for u, x in zip(us[sl], ws_qs)]
        outs = [x[c:] + _mm(qk, vn) for x, qk, vn in zip(ws_qs, qks[sl], v_news)]
        states = [s * el + _mm_tn(kd, vn) for s, el, kd, vn in zip(states, e_lasts[sl], k_decs[sl], v_news)]
        for i, o in enumerate(outs):
            b, h = divmod(i, A_HEADS)
            o_ref[b, ic * c:(ic + 1) * c, h * A_DK:(h + 1) * A_DK] = o
    for i, s in enumerate(states):
        b, h = divmod(i, A_HEADS)
        s_ref[b, h] = s

    @pl.when(t == nt - 1)
    def _():
        _slot_view(sout_ref, slot)[...] = s_ref[...]


def _delta_kernel_s0(qkv_ref, gb_ref, s0_ref, o_ref, sout_ref, s_ref, **kw):
    _delta_body(qkv_ref, gb_ref, s0_ref, o_ref, sout_ref, s_ref, **kw)


def _delta_kernel_zero(qkv_ref, gb_ref, o_ref, sout_ref, s_ref, **kw):
    _delta_body(qkv_ref, gb_ref, None, o_ref, sout_ref, s_ref, **kw)


def _delta(qkv, gb, s0, j, stacked, n_layers, *, nb, tl, c):
    n, l, _ = qkv.shape
    grid = (n // nb, l // tl)
    row_map = lambda i, t: (i, t, 0)
    st_block = (nb, A_HEADS, A_DK, A_DK)
    st_map = lambda i, t: (i, 0, 0, 0)
    first = stacked is None
    kw = dict(nb=nb, tl=tl, c=c, slot=(j, n_layers) if first else None)
    in_specs = [pl.BlockSpec((nb, tl, A_CONV_CH), row_map), pl.BlockSpec((nb, tl, LANES), row_map)]
    args = [qkv, gb]
    if s0 is not None:
        in_specs.append(_stacked_spec(st_block, st_map, j, n_layers, False))
        args.append(s0)
        body = functools.partial(_delta_kernel_s0, **kw)
    else:
        body = functools.partial(_delta_kernel_zero, **kw)
    body, in_specs, args, aliases = _alias_stacked(body, in_specs, args, stacked, 1)
    return pl.pallas_call(
        body, grid=grid, in_specs=in_specs,
        out_specs=[pl.BlockSpec((nb, tl, A_VW), row_map), _stacked_spec(st_block, st_map, j, n_layers, first)],
        out_shape=[jax.ShapeDtypeStruct((n, l, A_VW), F32),
                   jax.ShapeDtypeStruct((n_layers, n, A_HEADS, A_DK, A_DK), F32)],
        scratch_shapes=[pltpu.VMEM(st_block, F32)], input_output_aliases=aliases,
        compiler_params=_params(2), name="delta",
    )(*args)


def _swa_tasks(qs, kxs, vxs, first_min_key, sink_ref):
    nq = qs[0].shape[0]
    nk = 2 * WINDOW
    rows = B_HEADS * nq
    lo = lax.broadcasted_iota(jnp.int32, (nq, LANES), 1) < B_HD
    ri = lax.broadcasted_iota(jnp.int32, (rows, nk), 0) % nq
    ci = lax.broadcasted_iota(jnp.int32, (rows, nk), 1)
    band = (ci >= ri) & (ci <= ri + WINDOW)
    masks = [band & (ci >= first_min_key)] + [band] * (len(qs) - 1)
    head = lax.broadcasted_iota(jnp.int32, (rows, 1), 0) // nq
    sink = jnp.zeros((rows, 1), F32)
    for h in range(B_HEADS):
        sink = jnp.where(head == h, sink_ref[h], sink)

    def stack_heads(q):
        parts = []
        for h in range(B_HEADS):
            j, e = divmod(h, 2)
            g = h // (B_HEADS // B_KV_HEADS)
            src = q[:, j * LANES:(j + 1) * LANES]
            if e != g:
                src = pltpu.roll(src, B_HD, axis=1)
            parts.append(jnp.where(lo, src, 0.0) if g == 0 else jnp.where(lo, 0.0, src))
        return jnp.concatenate(parts, axis=0).astype(BF16)

    lhs = [stack_heads(q) for q in qs]
    ss = [jnp.where(mask, _mm_nt(x, kx), MASK_NEG) for mask, x, kx in zip(masks, lhs, kxs)]
    ms = [jnp.maximum(jnp.max(s, axis=-1, keepdims=True), sink) for s in ss]
    ps = [jnp.where(mask, jnp.exp(s - m), 0.0) for mask, s, m in zip(masks, ss, ms)]
    dens = [jnp.sum(p, axis=-1, keepdims=True) + jnp.exp(sink - m) for p, m in zip(ps, ms)]
    os_ = [_mm(p, vx) / d for p, d, vx in zip(ps, dens, vxs)]
    outs = []
    for o in os_:
        cols = []
        for j in range(B_HEADS // 2):
            a = o[(2 * j) * nq:(2 * j + 1) * nq]
            b = o[(2 * j + 1) * nq:(2 * j + 2) * nq]
            if j < B_HEADS // 4:
                cols.append(jnp.where(lo, a, pltpu.roll(b, B_HD, axis=1)))
            else:
                cols.append(jnp.where(lo, pltpu.roll(a, B_HD, axis=1), b))
        outs.append(jnp.concatenate(cols, axis=1))
    return outs


def _swa_prompt_kernel(sink_ref, q_ref, kvc_ref, kvp_ref, o_ref, *, nblk):
    i = pl.program_id(1)
    kv = [kvp_ref[0].astype(BF16)] + [kvc_ref[0, t * WINDOW:(t + 1) * WINDOW, :].astype(BF16) for t in range(nblk)]
    qs = [q_ref[0, t * WINDOW:(t + 1) * WINDOW, :] for t in range(nblk)]
    kxs = [jnp.concatenate([kv[t][:, :B_KVW], kv[t + 1][:, :B_KVW]], axis=0) for t in range(nblk)]
    vxs = [jnp.concatenate([kv[t][:, B_KVW:], kv[t + 1][:, B_KVW:]], axis=0) for t in range(nblk)]
    outs = _swa_tasks(qs, kxs, vxs, jnp.where(i > 0, 0, WINDOW), sink_ref)
    for t, o in enumerate(outs):
        o_ref[0, t * WINDOW:(t + 1) * WINDOW, :] = o


def _swa_prompt(qb, kvb, sinks, *, nblk):
    n, l, _ = qb.shape
    tq = nblk * WINDOW
    return pl.pallas_call(
        functools.partial(_swa_prompt_kernel, nblk=nblk), grid=(n, l // tq),
        in_specs=[pl.BlockSpec(memory_space=pltpu.SMEM),
                  pl.BlockSpec((1, tq, B_QW), lambda b, i: (b, i, 0)),
                  pl.BlockSpec((1, tq, 2 * B_KVW), lambda b, i: (b, i, 0)),
                  pl.BlockSpec((1, WINDOW, 2 * B_KVW), lambda b, i: (b, jnp.maximum(i * nblk - 1, 0), 0))],
        out_specs=pl.BlockSpec((1, tq, B_QW), lambda b, i: (b, i, 0)),
        out_shape=jax.ShapeDtypeStruct((n, l, B_QW), F32),
        compiler_params=_params(2), name="swa_prompt",
    )(sinks, qb, kvb, kvb)


def _swa_sample_kernel(sink_ref, q_ref, kv_ref, ck_ref, cv_ref, o_ref, nk_ref, nv_ref, *, nb, ls, slot):
    nk_ref = _slot_view(nk_ref, slot)
    nv_ref = _slot_view(nv_ref, slot)
    pad = jnp.zeros((WINDOW - ls, B_KVW), F32)
    qs, kxs, vxs = [], [], []
    for b in range(nb):
        kv = kv_ref[b]
        k_new = kv[:, :B_KVW]
        v_new = kv[:, B_KVW:]
        ck = ck_ref[b]
        cv = cv_ref[b]
        qs.append(q_ref[b])
        kxs.append(jnp.concatenate([ck, k_new, pad], axis=0).astype(BF16))
        vxs.append(jnp.concatenate([cv, v_new, pad], axis=0).astype(BF16))
        nk_ref[b, 0:WINDOW - ls, :] = ck[ls:, :]
        nk_ref[b, WINDOW - ls:, :] = k_new
        nv_ref[b, 0:WINDOW - ls, :] = cv[ls:, :]
        nv_ref[b, WINDOW - ls:, :] = v_new
    for b, o in enumerate(_swa_tasks(qs, kxs, vxs, 0, sink_ref)):
        o_ref[b] = o


def _swa_sample(qb, kvb, ck, cv, sinks, j, stacked, *, nb):
    n, ls, _ = qb.shape
    row_map = lambda i: (i, 0, 0)
    n_layers = ck.shape[0]
    first = stacked is None
    cache_block = (nb, WINDOW, B_KVW)
    cache_spec = _stacked_spec(cache_block, row_map, j, n_layers, False)
    cache_out = _stacked_spec(cache_block, row_map, j, n_layers, first)
    cache_shape = jax.ShapeDtypeStruct(ck.shape, F32)
    body = functools.partial(_swa_sample_kernel, nb=nb, ls=ls, slot=(j, n_layers) if first else None)
    in_specs = [pl.BlockSpec(memory_space=pltpu.SMEM),
                pl.BlockSpec((nb, ls, B_QW), row_map),
                pl.BlockSpec((nb, ls, 2 * B_KVW), row_map),
                cache_spec, cache_spec]
    args = [sinks, qb, kvb, ck, cv]
    body, in_specs, args, aliases = _alias_stacked(body, in_specs, args, stacked, 1, 2)
    return pl.pallas_call(
        body, grid=(n // nb,), in_specs=in_specs,
        out_specs=[pl.BlockSpec((nb, ls, B_QW), row_map), cache_out, cache_out],
        out_shape=[jax.ShapeDtypeStruct((n, ls, B_QW), F32), cache_shape, cache_shape],
        input_output_aliases=aliases, compiler_params=_params(1), name="swa_sample",
    )(*args)


def _lower_bound(lbraw_ref, j):
    raw = lbraw_ref[...]
    e = jnp.exp(raw - jnp.max(raw, axis=0, keepdims=True))
    sm = e / jnp.sum(e, axis=0, keepdims=True)
    lb = jnp.zeros((1, D_MODEL), F32)
    for i in range(1, j + 1):
        lb = lb + sm[i:i + 1, :]
    return lb


def _odd_proj(x, w_ref, lb):
    d = D_MODEL

    def proj(c0):
        return jnp.dot(x, w_ref[:, c0:c0 + d], preferred_element_type=F32)

    q = _silu(proj(0)) * (C_DK ** -0.5)
    zf = proj(d)
    fg = lb + (1.0 - lb) * _sigmoid(zf)
    lf = jnp.log(jnp.maximum(fg, F_FLOOR))
    k = (1.0 - lb) * _sigmoid(-zf)
    v = proj(2 * d)
    g = _silu(proj(3 * d))
    return q, k, v, lf, g


def _odd_in_kernel(x_ref, w_ref, lbraw_ref, q_ref, k_ref, v_ref, lf_ref, g_ref, *, j, ts):
    lb = _lower_bound(lbraw_ref, j)
    for s in range(x_ref.shape[0] // ts):
        rows = slice(s * ts, (s + 1) * ts)
        q, k, v, lf, g = _odd_proj(x_ref[rows, :].astype(BF16), w_ref, lb)
        q_ref[rows, :] = q
        k_ref[rows, :] = k
        v_ref[rows, :] = v
        lf_ref[rows, :] = lf
        g_ref[rows, :] = g


def _odd_in(x2, w, lb_raw, j, *, tr, ts=SUB_ROWS):
    rows = x2.shape[0]
    n_odd = lb_raw.shape[0]
    row_spec = pl.BlockSpec((tr, D_MODEL), lambda i: (i, 0))
    shp = jax.ShapeDtypeStruct((rows, D_MODEL), F32)
    return pl.pallas_call(
        functools.partial(_odd_in_kernel, j=j, ts=min(ts, tr)), grid=(rows // tr,),
        in_specs=[row_spec, _const_spec(w.shape), _const_spec((n_odd, D_MODEL))],
        out_specs=[row_spec] * 5, out_shape=[shp] * 5,
        compiler_params=_params(1), name="odd_in",
    )(x2, w, lb_raw)


def _hgrn_prep_all(qs, ks, vs, lfs, c, state_t):
    ng = c // SUBLANES
    shape3 = (ng, SUBLANES, C_DK)
    sub = lax.broadcasted_iota(jnp.int32, shape3, 1)

    def grouped(x):
        return x.reshape(shape3)

    def flat(x):
        return x.reshape(c, C_DK)

    gs = []
    for lf in lfs:
        x = grouped(lf * LOG2_E)
        s = 1
        while s < SUBLANES:
            x = x + jnp.where(sub >= s, pltpu.roll(x, s, axis=1), 0.0)
            s *= 2
        if ng > 1:
            tot = x[:, SUBLANES - 1:SUBLANES, :]
            run = jnp.zeros((1, 1, C_DK), F32)
            offs = []
            for i in range(ng):
                offs.append(run)
                run = run + tot[i:i + 1]
            x = x + jnp.concatenate(offs, axis=0)
        gs.append(x)
    rbs = [jnp.broadcast_to(g[:, 0:1, :], shape3) for g in gs]

    def block_start(rb, m):
        x = rb.reshape(ng // m, m, SUBLANES, C_DK)[:, 0:1]
        return jnp.broadcast_to(x, (ng // m, m, SUBLANES, C_DK)).reshape(shape3)

    ri = lax.broadcasted_iota(jnp.int32, (c, c), 0)
    ci = lax.broadcasted_iota(jnp.int32, (c, c), 1)
    q3s = [grouped(q) for q in qs]
    k3s = [grouped(k) for k in ks]
    accs = None
    bs = list(gs)
    h = 1
    while h < c:
        if h < SUBLANES:
            bns = [pltpu.roll(b, SUBLANES - h, axis=1) for b in bs]
        else:
            m = h // SUBLANES
            bs = [block_start(rb, m) for rb in rbs]
            bns = [jnp.concatenate([b[m:], g[ng - m:]], axis=0) for b, g in zip(bs, gs)]
        qhs = q3s if h == 1 else [q * jnp.exp2(g - b) for q, g, b in zip(q3s, gs, bs)]
        khs = [k * jnp.exp2(bn - g) for k, bn, g in zip(k3s, bns, gs)]
        lvl = ((ri // h) == (ci // h) + 1) & ((ri // (2 * h)) == (ci // (2 * h)))
        if h == 1 and ng > 1:
            gap = jnp.zeros((LANES - c, C_DK), F32)
            both = [_mm_nt(flat(q), jnp.concatenate([flat(k), gap, flat(kh)], axis=0))
                    for q, k, kh in zip(q3s, k3s, khs)]
            accs = [jnp.where(ri == ci, x[:, :c], jnp.where(lvl, x[:, LANES:], 0.0)) for x in both]
        elif h == 1:
            accs = [jnp.where(ri == ci, _mm_nt(flat(q), flat(k)), jnp.where(lvl, _mm_nt(flat(q), flat(kh)), 0.0))
                    for q, k, kh in zip(q3s, k3s, khs)]
        else:
            accs = [jnp.where(lvl, _mm_nt(flat(qh), flat(kh)), a) for a, qh, kh in zip(accs, qhs, khs)]
        if 2 * h < SUBLANES:
            sel = (sub & h) != 0
            bs = [jnp.where(sel, pltpu.roll(b, h, axis=1), b) for b in bs]
        h *= 2
    if accs is None:
        accs = [jnp.where(ri == ci, _mm_nt(q, k), 0.0) for q, k in zip(qs, ks)]
    g_lasts = [g[ng - 1, SUBLANES - 1:SUBLANES, :] for g in gs]
    intra = [_mm(a, v) for a, v in zip(accs, vs)]
    q_decs = [flat(q * jnp.exp2(g)).astype(BF16) for q, g in zip(q3s, gs)]
    k_decs = [flat(k * jnp.exp2(gl - g)) for k, gl, g in zip(k3s, g_lasts, gs)]
    e_lasts = [jnp.exp2(gl) for gl in g_lasts]
    if state_t:
        kvs = [_mm_tn(v, kd) for v, kd in zip(vs, k_decs)]
    else:
        kvs = [_mm_tn(kd, v) for v, kd in zip(vs, k_decs)]
        e_lasts = [jnp.broadcast_to(el, (C_DK, C_DK)).T for el in e_lasts]
    return intra, q_decs, kvs, e_lasts


def _hgrn_steps(qs, ks, vs, lfs, states, c, state_t):
    intra, q_decs, kvs, e_lasts = _hgrn_prep_all(qs, ks, vs, lfs, c, state_t)
    nch = len(states)
    read_state = _mm_nt if state_t else _mm
    outs = []
    for ic in range(len(qs) // nch):
        sl = slice(ic * nch, (ic + 1) * nch)
        outs += [read_state(qd, st) + x for qd, st, x in zip(q_decs[sl], states, intra[sl])]
        states = [st * el + kv for st, el, kv in zip(states, e_lasts[sl], kvs[sl])]
    return outs, states


def _hgrn_body(q_ref, k_ref, v_ref, lf_ref, s0_ref, o_ref, sout_ref, st_ref, *, nb, hb, tl, c, slot):
    t = pl.program_id(2)
    nt = pl.num_programs(2)
    nc = tl // c
    state_t = s0_ref is None

    @pl.when(t == 0)
    def _():
        if s0_ref is None:
            st_ref[...] = jnp.zeros(st_ref.shape, F32)
        else:
            st_ref[...] = s0_ref[...]

    chains = [(b, h) for b in range(nb) for h in range(hb)]
    where = [(ic, b, h) for ic in range(nc) for b, h in chains]

    def load(ref):
        return [ref[b, ic * c:(ic + 1) * c, h * C_DK:(h + 1) * C_DK] for ic, b, h in where]

    states = [st_ref[b, h] for b, h in chains]
    outs, states = _hgrn_steps(load(q_ref), load(k_ref), load(v_ref), load(lf_ref), states, c, state_t)
    for (ic, b, h), o in zip(where, outs):
        o_ref[b, ic * c:(ic + 1) * c, h * C_DK:(h + 1) * C_DK] = o
    for (b, h), st in zip(chains, states):
        st_ref[b, h] = st

    @pl.when(t == nt - 1)
    def _():
        out = _slot_view(sout_ref, slot)
        for b in range(nb):
            for h in range(hb):
                out[b, h] = st_ref[b, h].T if state_t else st_ref[b, h]


def _hgrn_kernel_s0(q_ref, k_ref, v_ref, lf_ref, s0_ref, o_ref, sout_ref, st_ref, **kw):
    _hgrn_body(q_ref, k_ref, v_ref, lf_ref, s0_ref, o_ref, sout_ref, st_ref, **kw)


def _hgrn_kernel_zero(q_ref, k_ref, v_ref, lf_ref, o_ref, sout_ref, st_ref, **kw):
    _hgrn_body(q_ref, k_ref, v_ref, lf_ref, None, o_ref, sout_ref, st_ref, **kw)


def _hgrn(q, k, v, lf, s0, j, stacked, n_layers, *, nb, hb, tl, c):
    n, l, _ = q.shape
    grid = (n // nb, C_HEADS // hb, l // tl)
    row_spec = pl.BlockSpec((nb, tl, hb * C_DK), lambda i, g, t: (i, t, g))
    st_block = (nb, hb, C_DK, C_DK)
    st_map = lambda i, g, t: (i, g, 0, 0)
    first = stacked is None
    st_spec = _stacked_spec(st_block, st_map, j, n_layers, first)
    in_specs = [row_spec] * 4
    args = [q, k, v, lf]
    kw = dict(nb=nb, hb=hb, tl=tl, c=c, slot=(j, n_layers) if first else None)
    if s0 is not None:
        in_specs = in_specs + [_stacked_spec(st_block, st_map, j, n_layers, False)]
        args.append(s0)
        body = functools.partial(_hgrn_kernel_s0, **kw)
    else:
        body = functools.partial(_hgrn_kernel_zero, **kw)
    body, in_specs, args, aliases = _alias_stacked(body, in_specs, args, stacked, 1)
    return pl.pallas_call(
        body, grid=grid, in_specs=in_specs, out_specs=[row_spec, st_spec],
        out_shape=[jax.ShapeDtypeStruct((n, l, D_MODEL), F32),
                   jax.ShapeDtypeStruct((n_layers, n, C_HEADS, C_DK, C_DK), F32)],
        scratch_shapes=[pltpu.VMEM(st_block, F32)], input_output_aliases=aliases,
        compiler_params=_params(3), name="hgrn",
    )(*args)


def _odd_fused_kernel(x_ref, w_ref, lbraw_ref, o_ref, g_ref, sout_ref, st_ref, *, j, tl, c, ts, slot):
    t = pl.program_id(1)
    nt = pl.num_programs(1)

    @pl.when(t == 0)
    def _():
        st_ref[...] = jnp.zeros(st_ref.shape, F32)

    lb = _lower_bound(lbraw_ref, j)
    heads = range(C_HEADS)

    def project(s):
        rows = slice(s * ts, (s + 1) * ts)
        q, k, v, lf, g = _odd_proj(x_ref[0, rows, :].astype(BF16), w_ref, lb)
        g_ref[0, rows, :] = g
        return q, k, v, lf

    def mix(s, qkvl, states):
        where = [(ic, h) for ic in range(ts // c) for h in heads]
        qs, ks, vs, lfs = ([a[ic * c:(ic + 1) * c, h * C_DK:(h + 1) * C_DK] for ic, h in where] for a in qkvl)
        outs, states = _hgrn_steps(qs, ks, vs, lfs, states, c, True)
        for (ic, h), o in zip(where, outs):
            o_ref[0, s * ts + ic * c:s * ts + (ic + 1) * c, h * C_DK:(h + 1) * C_DK] = o
        return states

    states = [st_ref[h] for h in heads]
    n_sub = tl // ts
    pending = project(0)
    for s in range(n_sub):
        following = project(s + 1) if s + 1 < n_sub else None
        states = mix(s, pending, states)
        pending = following
    for h in heads:
        st_ref[h] = states[h]

    @pl.when(t == nt - 1)
    def _():
        out = _slot_view(sout_ref, slot)
        for h in heads:
            out[0, h] = st_ref[h].T


def _odd_fused(x, w, lb_raw, j, stacked, n_layers, *, tl, c, ts=SUB_ROWS):
    n, l, _ = x.shape
    row_spec = pl.BlockSpec((1, tl, D_MODEL), lambda i, t: (i, t, 0))
    st_block = (1, C_HEADS, C_DK, C_DK)
    first = stacked is None
    body = functools.partial(_odd_fused_kernel, j=j, tl=tl, c=c, ts=ts, slot=(j, n_layers) if first else None)
    in_specs = [row_spec, _const_spec(w.shape), _const_spec(lb_raw.shape)]
    body, in_specs, args, aliases = _alias_stacked(body, in_specs, [x, w, lb_raw], stacked, 2)
    shp = jax.ShapeDtypeStruct((n, l, D_MODEL), F32)
    return pl.pallas_call(
        body, grid=(n, l // tl), in_specs=in_specs,
        out_specs=[row_spec, row_spec, _stacked_spec(st_block, lambda i, t: (i, 0, 0, 0), j, n_layers, first)],
        out_shape=[shp, shp, jax.ShapeDtypeStruct((n_layers, n, C_HEADS, C_DK, C_DK), F32)],
        scratch_shapes=[pltpu.VMEM((C_HEADS, C_DK, C_DK), F32)], input_output_aliases=aliases,
        compiler_params=_params(2), name="odd_fused",
    )(*args)


def _group_rms(o, g_row, heads):
    outs = []
    for h in range(heads):
        c = o[:, h * LANES:(h + 1) * LANES]
        inv = lax.rsqrt(jnp.mean(c * c, axis=-1, keepdims=True) + NORM_EPS)
        outs.append(c * inv * g_row)
    return jnp.concatenate(outs, axis=1)


def _post_tail(x, y, p, lng_ref, lnb_ref, wproj_ref, wgate_ref):
    pre = DEEPNORM_ALPHA * x + y
    mu = jnp.mean(pre, axis=-1, keepdims=True)
    xc = pre - mu
    var = jnp.mean(xc * xc, axis=-1, keepdims=True)
    hn = xc * lax.rsqrt(var + NORM_EPS) * lng_ref[...] + lnb_ref[...]
    gate = _sigmoid(jnp.dot(hn.astype(BF16), wgate_ref[:, 0:D_MODEL], preferred_element_type=F32))
    emb = jnp.dot(p.astype(BF16), wproj_ref[:, 0:D_MODEL], preferred_element_type=F32)
    return hn + gate * emb


def _sub_tiles(n_rows):
    ts = min(n_rows, POST_SUB_ROWS)
    return [slice(r0, r0 + ts) for r0 in range(0, n_rows, ts)]


def _post_even_kernel(x_ref, oa_ref, ga_ref, ob_ref, gbt_ref, p_ref, wout_ref, ng_ref, lng_ref, lnb_ref,
                      wproj_ref, wgate_ref, out_ref):
    for rows in _sub_tiles(x_ref.shape[0]):
        a = _group_rms(oa_ref[rows, :], ng_ref[...], A_HEADS) * ga_ref[rows, :]
        b = ob_ref[rows, :] * gbt_ref[rows, :]
        y = (jnp.dot(a.astype(BF16), wout_ref[0:A_VW, 0:D_MODEL], preferred_element_type=F32)
             + jnp.dot(b.astype(BF16), wout_ref[A_VW:, 0:D_MODEL], preferred_element_type=F32))
        out_ref[rows, :] = _post_tail(x_ref[rows, :], y, p_ref[rows, :], lng_ref, lnb_ref, wproj_ref, wgate_ref)


def _post_odd_kernel(x_ref, o_ref, g_ref, p_ref, wout_ref, ng_ref, lng_ref, lnb_ref,
                     wproj_ref, wgate_ref, out_ref):
    for rows in _sub_tiles(x_ref.shape[0]):
        a = _group_rms(o_ref[rows, :], ng_ref[...], C_HEADS) * g_ref[rows, :]
        y = jnp.dot(a.astype(BF16), wout_ref[:, 0:D_MODEL], preferred_element_type=F32)
        out_ref[rows, :] = _post_tail(x_ref[rows, :], y, p_ref[rows, :], lng_ref, lnb_ref, wproj_ref, wgate_ref)


def _post(x2, acts, p3, layer, wout, ng, lng, lnb, wproj, wgate, *, even, tr):
    rows = x2.shape[0]
    full = pl.BlockSpec((tr, D_MODEL), lambda i: (i, 0))
    half = pl.BlockSpec((tr, A_VW), lambda i: (i, 0))
    act_specs = [half] * 4 if even else [full] * 2
    in_specs = ([full] + act_specs + [pl.BlockSpec((None, tr, PLE_DIM), lambda i: (layer, i, 0)),
                                      _const_spec(wout.shape), _const_spec((1, LANES)),
                                      _const_spec((1, D_MODEL)), _const_spec((1, D_MODEL)),
                                      _const_spec(wproj.shape), _const_spec(wgate.shape)])
    return pl.pallas_call(
        _post_even_kernel if even else _post_odd_kernel, grid=(rows // tr,),
        in_specs=in_specs, out_specs=full, out_shape=jax.ShapeDtypeStruct((rows, D_MODEL), F32),
        compiler_params=_params(1), name="post_even" if even else "post_odd",
    )(x2, *acts, p3, wout, ng, lng, lnb, wproj, wgate)


def _pad_lanes(w):
    return jnp.pad(w, ((0, 0), (0, LANES)))


def _pack_even_w(w):
    qkv = w[:, 0:1536]
    ga = w[:, 1536:2048]
    ab = w[:, 2048:2056]
    qb = w[:, 2056:2568]
    kb = w[:, 2568:2696]
    vb = w[:, 2696:2824]
    gbt = w[:, 2824:3336]
    ab = jnp.pad(ab, ((0, 0), (0, LANES - 2 * A_HEADS)))
    return jnp.concatenate([qkv, ga, qb, kb, vb, gbt, ab], axis=1).astype(BF16)


def _rot_tables(pos, reps):
    inv = ROPE_THETA ** (-jnp.arange(0, ROT_DIM, 2, dtype=F32) / ROT_DIM)
    ang = pos.astype(F32)[:, None] * inv[None, :]
    cos = jnp.cos(ang)
    sin = jnp.sin(ang)
    n = pos.shape[0]
    half = ROT_DIM // 2
    cos64 = jnp.concatenate([cos, cos, jnp.ones((n, B_HD - ROT_DIM), F32)], axis=1)
    up64 = jnp.concatenate([-sin, jnp.zeros((n, B_HD - half), F32)], axis=1)
    dn64 = jnp.concatenate([jnp.zeros((n, half), F32), sin, jnp.zeros((n, B_HD - ROT_DIM), F32)], axis=1)
    tabs = [jnp.tile(t, (reps, LANES // B_HD)) for t in (cos64, up64, dn64)]
    return jnp.stack(tabs + [t * (B_HD ** -0.5) for t in tabs])


def kernel(x_prompt, x_sample, state_conv_a, state_delta_a, cache_win_k, cache_win_v, state_hgrn_c,
           p_prompt, p_sample, w_in_even, conv_w_a, a_log, dt_bias, norm_a, sinks_b, w_out_even,
           w_in_odd, lb_raw, norm_c, w_out_odd, ln_g, ln_b, w_ple_proj, w_ple_gate):
    n_p, l_p, _ = x_prompt.shape
    n_s, l_s, _ = x_sample.shape
    nb_s = 32
    rot_p = _rot_tables(jnp.arange(l_p), 1)
    rot_s = _rot_tables(PAST_LEN + jnp.arange(l_s), nb_s)
    n_even, n_odd = w_in_even.shape[0], w_in_odd.shape[0]
    conv_p, conv_s, wk_p, wv_p = [], [], [], []
    delta_p = delta_s = hg_p = hg_s = win_s = None
    hp, hs = x_prompt, x_sample
    pp3 = p_prompt.reshape(DEPTH, n_p * l_p, PLE_DIM)
    ps3 = p_sample.reshape(DEPTH, n_s * l_s, PLE_DIM)
    ck_all = cache_win_k.reshape(-1, n_s, WINDOW, B_KVW)
    cv_all = cache_win_v.reshape(-1, n_s, WINDOW, B_KVW)
    for layer in range(DEPTH):
        j = layer // 2
        lng = ln_g[layer].reshape(1, D_MODEL)
        lnb = ln_b[layer].reshape(1, D_MODEL)
        wproj = _pad_lanes(w_ple_proj[layer]).astype(BF16)
        wgate = _pad_lanes(w_ple_gate[layer]).astype(BF16)
        hp2 = hp.reshape(n_p * l_p, D_MODEL)
        hs2 = hs.reshape(n_s * l_s, D_MODEL)
        if layer % 2 == 0:
            w = _pack_even_w(w_in_even[j])
            misc = jnp.zeros((8, LANES), F32)
            misc = misc.at[0, :A_HEADS].set(a_log[j]).at[1, :A_HEADS].set(dt_bias[j])
            cw = conv_w_a[j]
            sinks = sinks_b[j]
            wout = _pad_lanes(w_out_even[j]).astype(BF16)
            ng = norm_a[j].reshape(1, LANES)
            qkv, gb, ga, qb, kvb, gbt, c1 = _even_in(hp, w, cw, misc, rot_p, None, j, nb=1, tl=512)
            oa, delta_p = _delta(qkv, gb, None, j, delta_p, n_even, nb=1, tl=512, c=A_CHUNK)
            ob = _swa_prompt(qb, kvb, sinks, nblk=4)
            acts = [t.reshape(n_p * l_p, A_VW) for t in (oa, ga, ob, gbt)]
            hp = _post(hp2, acts, pp3, layer, wout, ng, lng, lnb, wproj, wgate, even=True, tr=512)
            hp = hp.reshape(n_p, l_p, D_MODEL)
            k1 = kvb[:, l_p - WINDOW:, :B_KVW].reshape(n_p, WINDOW, B_KV_HEADS, B_HD)
            v1 = kvb[:, l_p - WINDOW:, B_KVW:].reshape(n_p, WINDOW, B_KV_HEADS, B_HD)
            qkv, gb, ga, qb, kvb, gbt, c2 = _even_in(hs, w, cw, misc, rot_s, state_conv_a, j, nb=nb_s, tl=l_s)
            oa, delta_s = _delta(qkv, gb, state_delta_a, j, delta_s, n_even, nb=8, tl=l_s, c=l_s)
            ob, *win_s = _swa_sample(qb, kvb, ck_all, cv_all, sinks, j, win_s, nb=8)
            acts = [t.reshape(n_s * l_s, A_VW) for t in (oa, ga, ob, gbt)]
            hs = _post(hs2, acts, ps3, layer, wout, ng, lng, lnb, wproj, wgate, even=True, tr=512)
            hs = hs.reshape(n_s, l_s, D_MODEL)
            conv_p.append(c1); conv_s.append(c2)
            wk_p.append(k1); wv_p.append(v1)
        else:
            w = _pad_lanes(w_in_odd[j]).astype(BF16)
            wout = _pad_lanes(w_out_odd[j]).astype(BF16)
            ng = norm_c[j].reshape(1, LANES)
            o, g, hg_p = _odd_fused(hp, w, lb_raw, j, hg_p, n_odd, tl=512, c=C_CHUNK)
            acts = [t.reshape(n_p * l_p, D_MODEL) for t in (o, g)]
            hp = _post(hp2, acts, pp3, layer, wout, ng, lng, lnb, wproj, wgate, even=False, tr=512)
            hp = hp.reshape(n_p, l_p, D_MODEL)
            q, k, v, lf, g = _odd_in(hs2, w, lb_raw, j, tr=256)
            shp = (n_s, l_s, D_MODEL)
            o, hg_s = _hgrn(q.reshape(shp), k.reshape(shp), v.reshape(shp), lf.reshape(shp), state_hgrn_c, j,
                            hg_s, n_odd, nb=4, hb=8, tl=l_s, c=l_s)
            hs = _post(hs2, [o.reshape(n_s * l_s, D_MODEL), g], ps3, layer, wout, ng, lng, lnb, wproj, wgate,
                       even=False, tr=512)
            hs = hs.reshape(n_s, l_s, D_MODEL)
    wk_s, wv_s = (t.reshape(n_even, n_s, WINDOW, B_KV_HEADS, B_HD) for t in win_s)
    return (hp, hs, jnp.stack(conv_p), jnp.stack(conv_s), delta_p, delta_s,
            jnp.stack(wk_p), wk_s, jnp.stack(wv_p), wv_s, hg_p, hg_s)
```

```python
import functools
import math

import jax
import jax.numpy as jnp
from jax import lax
from jax.experimental import pallas as pl
from jax.experimental.pallas import tpu as pltpu

F32 = jnp.float32
BF16 = jnp.bfloat16

D_MODEL = 1024
DEPTH = 4
PAST_LEN = 8192
PLE_DIM = 256
NORM_EPS = 1e-6
MASK_NEG = -1e30
F_FLOOR = 1e-30
DEEPNORM_ALPHA = (2 * DEPTH) ** 0.25

A_HEADS = 4
A_DK = 128
A_CONV = 4
A_QK = 512
A_VW = 512
A_CONV_CH = 1536
A_CHUNK = 64

B_HEADS = 8
B_KV_HEADS = 2
B_HD = 64
B_QW = 512
B_KVW = 128
WINDOW = 128
ROT_DIM = 16
ROPE_THETA = 500000.0

C_HEADS = 8
C_DK = 128
C_CHUNK = 64

LANES = 128
SUBLANES = 8
LOG2_E = math.log2(math.e)
VMEM_LIMIT = 56 * 1024 * 1024
SUB_ROWS = 128
POST_SUB_ROWS = 512

E_QKV = 0
E_GA = 1536
E_QB = 2048
E_KB = 2560
E_VB = 2688
E_GB = 2816
E_AB = 3328
E_TOT = 3456

NT = (((1,), (1,)), ((), ()))
TN = (((0,), (0,)), ((), ()))


def _mm(a, b):
    return jnp.dot(a.astype(BF16), b.astype(BF16), preferred_element_type=F32)


def _mm_nt(a, b):
    return lax.dot_general(a.astype(BF16), b.astype(BF16), NT, preferred_element_type=F32)


def _mm_tn(a, b):
    return lax.dot_general(a.astype(BF16), b.astype(BF16), TN, preferred_element_type=F32)


def _sigmoid(x):
    return 1.0 / (1.0 + jnp.exp(-x))


def _silu(x):
    return x * _sigmoid(x)


def _cumsum_rows(x):
    n = x.shape[0]
    row = lax.broadcasted_iota(jnp.int32, x.shape, 0)
    s = 1
    while s < n:
        x = x + jnp.where(row >= s, pltpu.roll(x, s, axis=0), 0.0)
        s *= 2
    return x


def _params(n_grid):
    return pltpu.CompilerParams(dimension_semantics=("arbitrary",) * n_grid,
                                vmem_limit_bytes=VMEM_LIMIT)


def _const_spec(shape):
    nd = len(shape)
    return pl.BlockSpec(shape, lambda *_: (0,) * nd)


def _layer_spec(stacked, layer):
    nd = stacked.ndim
    return pl.BlockSpec((None,) + stacked.shape[1:], lambda *_: (layer,) + (0,) * (nd - 1))


def _ignore_first_ref(body, _stacked_ref, *refs):
    body(*refs)


def _stacked_spec(block, index_fn, j, n_layers, first):
    if first:
        return pl.BlockSpec((n_layers,) + block, lambda *g: (0,) + index_fn(*g))
    return pl.BlockSpec((None,) + block, lambda *g: (j,) + index_fn(*g))


def _slot_view(ref, slot):
    if slot is None:
        return ref
    j, n_layers = slot
    for i in range(n_layers):
        if i != j:
            ref[i] = jnp.zeros(ref.shape[1:], ref.dtype)
    return ref.at[j]


def _alias_stacked(body, in_specs, args, stacked, *out_indices):
    if stacked is None:
        return body, in_specs, args, {}
    if not isinstance(stacked, (list, tuple)):
        stacked = [stacked]
    for arr in reversed(stacked):
        body = functools.partial(_ignore_first_ref, body)
        in_specs = [pl.BlockSpec(memory_space=pl.ANY)] + list(in_specs)
        args = [arr] + list(args)
    return body, in_specs, args, {i: o for i, o in enumerate(out_indices)}


def _even_in_body(x_ref, w_ref, cw_ref, misc_ref, rot_ref, buf_ref,
                  qkv_ref, gb_ref, ga_ref, qb_ref, kvb_ref, gbt_ref, conv_ref, xp_ref,
                  *, nb, tl):
    t = pl.program_id(1)
    if buf_ref is not None:
        xp_ref[:, 5:8, :] = buf_ref[...]
    else:
        @pl.when(t == 0)
        def _():
            xp_ref[:, 5:8, :] = jnp.zeros((nb, 3, A_CONV_CH), F32)

    neg_a = -jnp.exp(misc_ref[0:1, :])
    dt_bias = misc_ref[1:2, :]
    half = ROT_DIM // 2
    rep = B_QW // LANES

    if nb == 1:
        nbs, ts = 1, min(tl, SUB_ROWS)
    else:
        nbs, ts = max(1, min(nb, SUB_ROWS // tl)), tl
    r = nbs * ts
    for b0 in range(0, nb, nbs):
        for r0 in range(0, tl, ts):
            bs = slice(b0, b0 + nbs)
            rs = slice(r0, r0 + ts)
            flat = slice(b0 * tl + r0, b0 * tl + r0 + r)
            x = x_ref[bs, rs, :].reshape(r, D_MODEL).astype(BF16)

            cw_blk = 2 * A_DK
            for c0 in range(0, A_CONV_CH, cw_blk):
                cols = slice(c0, c0 + cw_blk)
                z = jnp.dot(x, w_ref[:, E_QKV + c0:E_QKV + c0 + cw_blk], preferred_element_type=F32)
                xp_ref[bs, 8 + r0:8 + r0 + ts, cols] = z.reshape(nbs, ts, cw_blk)
                y = xp_ref[bs, 5 + r0:5 + r0 + ts, cols] * cw_ref[0:1, cols].reshape(1, 1, cw_blk)
                for j in range(1, A_CONV):
                    y = y + xp_ref[bs, 5 + j + r0:5 + j + r0 + ts, cols] * cw_ref[j:j + 1, cols].reshape(1, 1, cw_blk)
                y = _silu(y).reshape(r, cw_blk)
                for h0 in range(0, cw_blk, A_DK):
                    c = y[:, h0:h0 + A_DK]
                    if c0 + h0 < 2 * A_QK:
                        inv = lax.rsqrt(jnp.sum(c * c, axis=-1, keepdims=True) + NORM_EPS)
                        if c0 + h0 < A_QK:
                            inv = inv * (A_DK ** -0.5)
                        c = c * inv
                    qkv_ref[bs, rs, c0 + h0:c0 + h0 + A_DK] = c.reshape(nbs, ts, A_DK)

            ab = jnp.dot(x, w_ref[:, E_AB:E_AB + LANES], preferred_element_type=F32)
            sp_in = ab + dt_bias
            softplus = jnp.maximum(sp_in, 0.0) + jnp.log1p(jnp.exp(-jnp.abs(sp_in)))
            lane = lax.broadcasted_iota(jnp.int32, ab.shape, 1)
            gb = jnp.where(lane < A_HEADS, neg_a * softplus, _sigmoid(ab))
            gb_ref[bs, rs, :] = gb.reshape(nbs, ts, LANES)

            ga_ref[bs, rs, :] = _silu(jnp.dot(x, w_ref[:, E_GA:E_GA + A_VW],
                                              preferred_element_type=F32)).reshape(nbs, ts, A_VW)
            gbt_ref[bs, rs, :] = _silu(jnp.dot(x, w_ref[:, E_GB:E_GB + B_QW],
                                               preferred_element_type=F32)).reshape(nbs, ts, B_QW)

            cos_t = rot_ref[0, flat, :]
            s_up = rot_ref[1, flat, :]
            s_dn = rot_ref[2, flat, :]
            zq = jnp.dot(x, w_ref[:, E_QB:E_QB + B_QW], preferred_element_type=F32)
            cos4 = jnp.concatenate([rot_ref[3, flat, :]] * rep, axis=1)
            up4 = jnp.concatenate([rot_ref[4, flat, :]] * rep, axis=1)
            dn4 = jnp.concatenate([rot_ref[5, flat, :]] * rep, axis=1)
            qrot = zq * cos4 + pltpu.roll(zq, B_QW - half, axis=1) * up4 + pltpu.roll(zq, half, axis=1) * dn4
            qb_ref[bs, rs, :] = qrot.reshape(nbs, ts, B_QW)
            zk = jnp.dot(x, w_ref[:, E_KB:E_KB + B_KVW], preferred_element_type=F32)
            krot = zk * cos_t + pltpu.roll(zk, LANES - half, axis=1) * s_up + pltpu.roll(zk, half, axis=1) * s_dn
            kvb_ref[bs, rs, 0:B_KVW] = krot.reshape(nbs, ts, B_KVW)
            zv = jnp.dot(x, w_ref[:, E_VB:E_VB + B_KVW], preferred_element_type=F32)
            kvb_ref[bs, rs, B_KVW:] = zv.reshape(nbs, ts, B_KVW)

    last = xp_ref[:, tl + 5:tl + 8, :]
    conv_ref[...] = last
    if buf_ref is None:
        xp_ref[:, 5:8, :] = last


def _even_in_kernel_buf(x_ref, w_ref, cw_ref, misc_ref, rot_ref, buf_ref, *rest, nb, tl):
    _even_in_body(x_ref, w_ref, cw_ref, misc_ref, rot_ref, buf_ref, *rest, nb=nb, tl=tl)


def _even_in_kernel_nobuf(x_ref, w_ref, cw_ref, misc_ref, rot_ref, *rest, nb, tl):
    _even_in_body(x_ref, w_ref, cw_ref, misc_ref, rot_ref, None, *rest, nb=nb, tl=tl)


def _even_in(x, w, cw, misc, rot, buf, j, *, nb, tl):
    n, l, _ = x.shape
    grid = (n // nb, l // tl)
    r = nb * tl
    row_map = lambda i, t: (i, t, 0)
    rot_map = (lambda i, t: (0, t, 0)) if buf is None else (lambda i, t: (0, 0, 0))
    in_specs = [
        pl.BlockSpec((nb, tl, D_MODEL), row_map),
        _layer_spec(w, j),
        _const_spec((A_CONV, A_CONV_CH)),
        _const_spec((8, LANES)),
        pl.BlockSpec((6, r, LANES), rot_map),
    ]
    args = [x, w, cw, misc, rot]
    if buf is not None:
        in_specs.append(pl.BlockSpec((None, nb, A_CONV - 1, A_CONV_CH), lambda i, t: (j, i, 0, 0)))
        args.append(buf)
        body = functools.partial(_even_in_kernel_buf, nb=nb, tl=tl)
    else:
        body = functools.partial(_even_in_kernel_nobuf, nb=nb, tl=tl)
    widths = (A_CONV_CH, LANES, A_VW, B_QW, 2 * B_KVW, B_QW)
    out_shape = [jax.ShapeDtypeStruct((n, l, wd), F32) for wd in widths]
    out_specs = [pl.BlockSpec((nb, tl, wd), row_map) for wd in widths]
    out_shape.append(jax.ShapeDtypeStruct((n, A_CONV - 1, A_CONV_CH), F32))
    out_specs.append(pl.BlockSpec((nb, A_CONV - 1, A_CONV_CH), lambda i, t: (i, 0, 0)))
    return pl.pallas_call(
        body, grid=grid, in_specs=in_specs, out_specs=out_specs, out_shape=out_shape,
        scratch_shapes=[pltpu.VMEM((nb, tl + 8, A_CONV_CH), F32)],
        compiler_params=_params(2), name="even_in",
    )(*args)


def _tri_inv_all(a_list, c):
    ri = lax.broadcasted_iota(jnp.int32, (c, LANES), 0)
    ci = lax.broadcasted_iota(jnp.int32, (c, LANES), 1)
    eye = jnp.where(ri == ci, 1.0, 0.0)
    ps = [eye for _ in a_list]
    ms = [-a for a in a_list]
    span = 1
    while span < c:
        mps = [_mm(m[:, :c], jnp.concatenate([m, p], axis=1)) for m, p in zip(ms, ps)]
        ms = [mp[:, :LANES] for mp in mps]
        ps = [p + mp[:, LANES:] for p, mp in zip(ps, mps)]
        span *= 2
    return ps


def _delta_prep_all(qs, ks, vs, g_cols, g_rows, betas, g_lasts, c):
    ri = lax.broadcasted_iota(jnp.int32, (c, LANES), 0)
    ci = lax.broadcasted_iota(jnp.int32, (c, LANES), 1)
    causal = ri >= ci
    strict = ri > ci
    decays = [jnp.where(causal, jnp.exp(jnp.where(causal, gc - gr, 0.0)), 0.0) for gc, gr in zip(g_cols, g_rows)]
    kbs = [k * b for k, b in zip(ks, betas)]
    if c < LANES:
        pad = jnp.zeros((LANES - c, A_DK), F32)
        k_rows = [jnp.concatenate([k, pad], axis=0).astype(BF16) for k in ks]
    else:
        k_rows = [k.astype(BF16) for k in ks]
    a_list = [jnp.where(strict, _mm_nt(kb, kr) * d, 0.0) for kb, kr, d in zip(kbs, k_rows, decays)]
    qks = [(_mm_nt(q, kr) * d)[:, :c].astype(BF16) for q, kr, d in zip(qs, k_rows, decays)]
    t_invs = _tri_inv_all(a_list, c)
    e_gs = [jnp.exp(gc) for gc in g_cols]
    uws = [_mm(t[:, :c], jnp.concatenate([v * b, kb * e], axis=1))
           for t, v, b, kb, e in zip(t_invs, vs, betas, kbs, e_gs)]
    us = [uw[:, :A_DK] for uw in uws]
    wqs = [jnp.concatenate([uw[:, A_DK:], q * e], axis=0).astype(BF16) for uw, q, e in zip(uws, qs, e_gs)]
    k_decs = [(k * jnp.exp(gl - gc)).astype(BF16) for k, gl, gc in zip(ks, g_lasts, g_cols)]
    e_lasts = [jnp.exp(gl) for gl in g_lasts]
    return us, wqs, qks, k_decs, e_lasts


def _delta_body(qkv_ref, gb_ref, s0_ref, o_ref, sout_ref, s_ref, *, nb, tl, c, slot):
    t = pl.program_id(1)
    nt = pl.num_programs(1)

    @pl.when(t == 0)
    def _():
        if s0_ref is None:
            s_ref[...] = jnp.zeros(s_ref.shape, F32)
        else:
            s_ref[...] = s0_ref[...]

    nc = tl // c
    qs, ks, vs, g_cols, g_rows, betas, g_lasts = [], [], [], [], [], [], []
    for ic in range(nc):
        rows = slice(ic * c, (ic + 1) * c)
        for b in range(nb):
            gbc = gb_ref[b, rows, :]
            g_all = _cumsum_rows(gbc)
            if c < LANES:
                g_t = jnp.concatenate([g_all, jnp.zeros((LANES - c, LANES), F32)], axis=0).T
            else:
                g_t = g_all.T
            for h in range(A_HEADS):
                qs.append(qkv_ref[b, rows, h * A_DK:(h + 1) * A_DK])
                ks.append(qkv_ref[b, rows, A_QK + h * A_DK:A_QK + (h + 1) * A_DK])
                vs.append(qkv_ref[b, rows, 2 * A_QK + h * A_DK:2 * A_QK + (h + 1) * A_DK])
                g_cols.append(g_all[:, h:h + 1])
                g_rows.append(g_t[h:h + 1, :])
                betas.append(gbc[:, A_HEADS + h:A_HEADS + h + 1])
                g_lasts.append(g_all[c - 1:c, h:h + 1])
    us, wqs, qks, k_decs, e_lasts = _delta_prep_all(qs, ks, vs, g_cols, g_rows, betas, g_lasts, c)

    nch = nb * A_HEADS
    states = [s_ref[b, h] for b in range(nb) for h in range(A_HEADS)]
    for ic in range(nc):
        sl = slice(ic * nch, (ic + 1) * nch)
        ws_qs = [_mm(wq, s) for wq, s in zip(wqs[sl], states)]
        v_news = [(u - x[:c]).astype(BF16) for u, x in zip(us[sl], ws_qs)]
        outs = [x[c:] + _mm(qk, vn) for x, qk, vn in zip(ws_qs, qks[sl], v_news)]
        states = [s * el + _mm_tn(kd, vn) for s, el, kd, vn in zip(states, e_lasts[sl], k_decs[sl], v_news)]
        for i, o in enumerate(outs):
            b, h = divmod(i, A_HEADS)
            o_ref[b, ic * c:(ic + 1) * c, h * A_DK:(h + 1) * A_DK] = o
    for i, s in enumerate(states):
        b, h = divmod(i, A_HEADS)
        s_ref[b, h] = s

    @pl.when(t == nt - 1)
    def _():
        _slot_view(sout_ref, slot)[...] = s_ref[...]


def _delta_kernel_s0(qkv_ref, gb_ref, s0_ref, o_ref, sout_ref, s_ref, **kw):
    _delta_body(qkv_ref, gb_ref, s0_ref, o_ref, sout_ref, s_ref, **kw)


def _delta_kernel_zero(qkv_ref, gb_ref, o_ref, sout_ref, s_ref, **kw):
    _delta_body(qkv_ref, gb_ref, None, o_ref, sout_ref, s_ref, **kw)


def _delta(qkv, gb, s0, j, stacked, n_layers, *, nb, tl, c):
    n, l, _ = qkv.shape
    grid = (n // nb, l // tl)
    row_map = lambda i, t: (i, t, 0)
    st_block = (nb, A_HEADS, A_DK, A_DK)
    st_map = lambda i, t: (i, 0, 0, 0)
    first = stacked is None
    kw = dict(nb=nb, tl=tl, c=c, slot=(j, n_layers) if first else None)
    in_specs = [pl.BlockSpec((nb, tl, A_CONV_CH), row_map), pl.BlockSpec((nb, tl, LANES), row_map)]
    args = [qkv, gb]
    if s0 is not None:
        in_specs.append(_stacked_spec(st_block, st_map, j, n_layers, False))
        args.append(s0)
        body = functools.partial(_delta_kernel_s0, **kw)
    else:
        body = functools.partial(_delta_kernel_zero, **kw)
    body, in_specs, args, aliases = _alias_stacked(body, in_specs, args, stacked, 1)
    return pl.pallas_call(
        body, grid=grid, in_specs=in_specs,
        out_specs=[pl.BlockSpec((nb, tl, A_VW), row_map), _stacked_spec(st_block, st_map, j, n_layers, first)],
        out_shape=[jax.ShapeDtypeStruct((n, l, A_VW), F32),
                   jax.ShapeDtypeStruct((n_layers, n, A_HEADS, A_DK, A_DK), F32)],
        scratch_shapes=[pltpu.VMEM(st_block, F32)], input_output_aliases=aliases,
        compiler_params=_params(2), name="delta",
    )(*args)


def _swa_tasks(qs, kxs, vxs, first_min_key, sink_ref):
    nq = qs[0].shape[0]
    nk = 2 * WINDOW
    rows = B_HEADS * nq
    lo = lax.broadcasted_iota(jnp.int32, (nq, LANES), 1) < B_HD
    ri = lax.broadcasted_iota(jnp.int32, (rows, nk), 0) % nq
    ci = lax.broadcasted_iota(jnp.int32, (rows, nk), 1)
    band = (ci >= ri) & (ci <= ri + WINDOW)
    masks = [band & (ci >= first_min_key)] + [band] * (len(qs) - 1)
    head = lax.broadcasted_iota(jnp.int32, (rows, 1), 0) // nq
    sink = jnp.zeros((rows, 1), F32)
    for h in range(B_HEADS):
        sink = jnp.where(head == h, sink_ref[h], sink)

    def stack_heads(q):
        parts = []
        for h in range(B_HEADS):
            j, e = divmod(h, 2)
            g = h // (B_HEADS // B_KV_HEADS)
            src = q[:, j * LANES:(j + 1) * LANES]
            if e != g:
                src = pltpu.roll(src, B_HD, axis=1)
            parts.append(jnp.where(lo, src, 0.0) if g == 0 else jnp.where(lo, 0.0, src))
        return jnp.concatenate(parts, axis=0).astype(BF16)

    lhs = [stack_heads(q) for q in qs]
    ss = [jnp.where(mask, _mm_nt(x, kx), MASK_NEG) for mask, x, kx in zip(masks, lhs, kxs)]
    ms = [jnp.maximum(jnp.max(s, axis=-1, keepdims=True), sink) for s in ss]
    ps = [jnp.where(mask, jnp.exp(s - m), 0.0) for mask, s, m in zip(masks, ss, ms)]
    dens = [jnp.sum(p, axis=-1, keepdims=True) + jnp.exp(sink - m) for p, m in zip(ps, ms)]
    os_ = [_mm(p, vx) / d for p, d, vx in zip(ps, dens, vxs)]
    outs = []
    for o in os_:
        cols = []
        for j in range(B_HEADS // 2):
            a = o[(2 * j) * nq:(2 * j + 1) * nq]
            b = o[(2 * j + 1) * nq:(2 * j + 2) * nq]
            if j < B_HEADS // 4:
                cols.append(jnp.where(lo, a, pltpu.roll(b, B_HD, axis=1)))
            else:
                cols.append(jnp.where(lo, pltpu.roll(a, B_HD, axis=1), b))
        outs.append(jnp.concatenate(cols, axis=1))
    return outs


def _swa_prompt_kernel(sink_ref, q_ref, kvc_ref, kvp_ref, o_ref, *, nblk):
    i = pl.program_id(1)
    kv = [kvp_ref[0].astype(BF16)] + [kvc_ref[0, t * WINDOW:(t + 1) * WINDOW, :].astype(BF16) for t in range(nblk)]
    qs = [q_ref[0, t * WINDOW:(t + 1) * WINDOW, :] for t in range(nblk)]
    kxs = [jnp.concatenate([kv[t][:, :B_KVW], kv[t + 1][:, :B_KVW]], axis=0) for t in range(nblk)]
    vxs = [jnp.concatenate([kv[t][:, B_KVW:], kv[t + 1][:, B_KVW:]], axis=0) for t in range(nblk)]
    outs = _swa_tasks(qs, kxs, vxs, jnp.where(i > 0, 0, WINDOW), sink_ref)
    for t, o in enumerate(outs):
        o_ref[0, t * WINDOW:(t + 1) * WINDOW, :] = o


def _swa_prompt(qb, kvb, sinks, *, nblk):
    n, l, _ = qb.shape
    tq = nblk * WINDOW
    return pl.pallas_call(
        functools.partial(_swa_prompt_kernel, nblk=nblk), grid=(n, l // tq),
        in_specs=[pl.BlockSpec(memory_space=pltpu.SMEM),
                  pl.BlockSpec((1, tq, B_QW), lambda b, i: (b, i, 0)),
                  pl.BlockSpec((1, tq, 2 * B_KVW), lambda b, i: (b, i, 0)),
                  pl.BlockSpec((1, WINDOW, 2 * B_KVW), lambda b, i: (b, jnp.maximum(i * nblk - 1, 0), 0))],
        out_specs=pl.BlockSpec((1, tq, B_QW), lambda b, i: (b, i, 0)),
        out_shape=jax.ShapeDtypeStruct((n, l, B_QW), F32),
        compiler_params=_params(2), name="swa_prompt",
    )(sinks, qb, kvb, kvb)


def _swa_sample_kernel(sink_ref, q_ref, kv_ref, ck_ref, cv_ref, o_ref, nk_ref, nv_ref, *, nb, ls, slot):
    nk_ref = _slot_view(nk_ref, slot)
    nv_ref = _slot_view(nv_ref, slot)
    pad = jnp.zeros((WINDOW - ls, B_KVW), F32)
    qs, kxs, vxs = [], [], []
    for b in range(nb):
        kv = kv_ref[b]
        k_new = kv[:, :B_KVW]
        v_new = kv[:, B_KVW:]
        ck = ck_ref[b]
        cv = cv_ref[b]
        qs.append(q_ref[b])
        kxs.append(jnp.concatenate([ck, k_new, pad], axis=0).astype(BF16))
        vxs.append(jnp.concatenate([cv, v_new, pad], axis=0).astype(BF16))
        nk_ref[b, 0:WINDOW - ls, :] = ck[ls:, :]
        nk_ref[b, WINDOW - ls:, :] = k_new
        nv_ref[b, 0:WINDOW - ls, :] = cv[ls:, :]
        nv_ref[b, WINDOW - ls:, :] = v_new
    for b, o in enumerate(_swa_tasks(qs, kxs, vxs, 0, sink_ref)):
        o_ref[b] = o


def _swa_sample(qb, kvb, ck, cv, sinks, j, stacked, *, nb):
    n, ls, _ = qb.shape
    row_map = lambda i: (i, 0, 0)
    n_layers = ck.shape[0]
    first = stacked is None
    cache_block = (nb, WINDOW, B_KVW)
    cache_spec = _stacked_spec(cache_block, row_map, j, n_layers, False)
    cache_out = _stacked_spec(cache_block, row_map, j, n_layers, first)
    cache_shape = jax.ShapeDtypeStruct(ck.shape, F32)
    body = functools.partial(_swa_sample_kernel, nb=nb, ls=ls, slot=(j, n_layers) if first else None)
    in_specs = [pl.BlockSpec(memory_space=pltpu.SMEM),
                pl.BlockSpec((nb, ls, B_QW), row_map),
                pl.BlockSpec((nb, ls, 2 * B_KVW), row_map),
                cache_spec, cache_spec]
    args = [sinks, qb, kvb, ck, cv]
    body, in_specs, args, aliases = _alias_stacked(body, in_specs, args, stacked, 1, 2)
    return pl.pallas_call(
        body, grid=(n // nb,), in_specs=in_specs,
        out_specs=[pl.BlockSpec((nb, ls, B_QW), row_map), cache_out, cache_out],
        out_shape=[jax.ShapeDtypeStruct((n, ls, B_QW), F32), cache_shape, cache_shape],
        input_output_aliases=aliases, compiler_params=_params(1), name="swa_sample",
    )(*args)


def _lower_bound(lbraw_ref, j):
    raw = lbraw_ref[...]
    e = jnp.exp(raw - jnp.max(raw, axis=0, keepdims=True))
    sm = e / jnp.sum(e, axis=0, keepdims=True)
    lb = jnp.zeros((1, D_MODEL), F32)
    for i in range(1, j + 1):
        lb = lb + sm[i:i + 1, :]
    return lb


def _odd_proj(x, w_ref, lb):
    d = D_MODEL

    def proj(c0):
        return jnp.dot(x, w_ref[:, c0:c0 + d], preferred_element_type=F32)

    q = _silu(proj(0)) * (C_DK ** -0.5)
    zf = proj(d)
    one_m_lb = 1.0 - lb
    u = one_m_lb * _sigmoid(zf)
    lf = jnp.log(jnp.maximum(lb + u, F_FLOOR))
    k = one_m_lb - u
    v = proj(2 * d)
    g = _silu(proj(3 * d))
    return q, k, v, lf, g


def _odd_in_kernel(x_ref, w_ref, lbraw_ref, q_ref, k_ref, v_ref, lf_ref, g_ref, *, j, ts):
    lb = _lower_bound(lbraw_ref, j)
    for s in range(x_ref.shape[0] // ts):
        rows = slice(s * ts, (s + 1) * ts)
        q, k, v, lf, g = _odd_proj(x_ref[rows, :].astype(BF16), w_ref, lb)
        q_ref[rows, :] = q
        k_ref[rows, :] = k
        v_ref[rows, :] = v
        lf_ref[rows, :] = lf
        g_ref[rows, :] = g


def _odd_in(x2, w, lb_raw, j, *, tr, ts=SUB_ROWS):
    rows = x2.shape[0]
    n_odd = lb_raw.shape[0]
    row_spec = pl.BlockSpec((tr, D_MODEL), lambda i: (i, 0))
    shp = jax.ShapeDtypeStruct((rows, D_MODEL), F32)
    return pl.pallas_call(
        functools.partial(_odd_in_kernel, j=j, ts=min(ts, tr)), grid=(rows // tr,),
        in_specs=[row_spec, _layer_spec(w, j), _const_spec((n_odd, D_MODEL))],
        out_specs=[row_spec] * 5, out_shape=[shp] * 5,
        compiler_params=_params(1), name="odd_in",
    )(x2, w, lb_raw)


def _hgrn_prep_all(qs, ks, vs, lfs, c, state_t):
    ng = c // SUBLANES
    shape3 = (ng, SUBLANES, C_DK)
    sub = lax.broadcasted_iota(jnp.int32, shape3, 1)

    def grouped(x):
        return x.reshape(shape3)

    def flat(x):
        return x.reshape(c, C_DK)

    gs = []
    for lf in lfs:
        x = grouped(lf * LOG2_E)
        s = 1
        while s < SUBLANES:
            x = x + jnp.where(sub >= s, pltpu.roll(x, s, axis=1), 0.0)
            s *= 2
        if ng > 1:
            tot = x[:, SUBLANES - 1:SUBLANES, :]
            run = jnp.zeros((1, 1, C_DK), F32)
            offs = []
            for i in range(ng):
                offs.append(run)
                run = run + tot[i:i + 1]
            x = x + jnp.concatenate(offs, axis=0)
        gs.append(x)
    rbs = [jnp.broadcast_to(g[:, 0:1, :], shape3) for g in gs]

    def block_start(rb, m):
        x = rb.reshape(ng // m, m, SUBLANES, C_DK)[:, 0:1]
        return jnp.broadcast_to(x, (ng // m, m, SUBLANES, C_DK)).reshape(shape3)

    ri = lax.broadcasted_iota(jnp.int32, (c, c), 0)
    ci = lax.broadcasted_iota(jnp.int32, (c, c), 1)
    q3s = [grouped(q) for q in qs]
    k3s = [grouped(k) for k in ks]
    accs = None
    bs = list(gs)
    h = 1
    while h < c:
        if h < SUBLANES:
            bns = [pltpu.roll(b, SUBLANES - h, axis=1) for b in bs]
        else:
            m = h // SUBLANES
            bs = [block_start(rb, m) for rb in rbs]
            bns = [jnp.concatenate([b[m:], g[ng - m:]], axis=0) for b, g in zip(bs, gs)]
        qhs = q3s if h == 1 else [q * jnp.exp2(g - b) for q, g, b in zip(q3s, gs, bs)]
        khs = [k * jnp.exp2(bn - g) for k, bn, g in zip(k3s, bns, gs)]
        lvl = ((ri // h) == (ci // h) + 1) & ((ri // (2 * h)) == (ci // (2 * h)))
        if h == 1 and ng > 1:
            gap = jnp.zeros((LANES - c, C_DK), F32)
            both = [_mm_nt(flat(q), jnp.concatenate([flat(k), gap, flat(kh)], axis=0))
                    for q, k, kh in zip(q3s, k3s, khs)]
            accs = [jnp.where(ri == ci, x[:, :c], jnp.where(lvl, x[:, LANES:], 0.0)) for x in both]
        elif h == 1:
            accs = [jnp.where(ri == ci, _mm_nt(flat(q), flat(k)), jnp.where(lvl, _mm_nt(flat(q), flat(kh)), 0.0))
                    for q, k, kh in zip(q3s, k3s, khs)]
        else:
            accs = [jnp.where(lvl, _mm_nt(flat(qh), flat(kh)), a) for a, qh, kh in zip(accs, qhs, khs)]
        if 2 * h < SUBLANES:
            sel = (sub & h) != 0
            bs = [jnp.where(sel, pltpu.roll(b, h, axis=1), b) for b in bs]
        h *= 2
    if accs is None:
        accs = [jnp.where(ri == ci, _mm_nt(q, k), 0.0) for q, k in zip(qs, ks)]
    g_lasts = [g[ng - 1, SUBLANES - 1:SUBLANES, :] for g in gs]
    intra = [_mm(a, v) for a, v in zip(accs, vs)]
    q_decs = [flat(q * jnp.exp2(g)).astype(BF16) for q, g in zip(q3s, gs)]
    k_decs = [flat(k * jnp.exp2(gl - g)) for k, gl, g in zip(k3s, g_lasts, gs)]
    e_lasts = [jnp.exp2(gl) for gl in g_lasts]
    if state_t:
        kvs = [_mm_tn(v, kd) for v, kd in zip(vs, k_decs)]
    else:
        kvs = [_mm_tn(kd, v) for v, kd in zip(vs, k_decs)]
        e_lasts = [jnp.broadcast_to(el, (C_DK, C_DK)).T for el in e_lasts]
    return intra, q_decs, kvs, e_lasts


def _hgrn_steps(qs, ks, vs, lfs, states, c, state_t):
    intra, q_decs, kvs, e_lasts = _hgrn_prep_all(qs, ks, vs, lfs, c, state_t)
    nch = len(states)
    read_state = _mm_nt if state_t else _mm
    outs = []
    for ic in range(len(qs) // nch):
        sl = slice(ic * nch, (ic + 1) * nch)
        outs += [read_state(qd, st) + x for qd, st, x in zip(q_decs[sl], states, intra[sl])]
        states = [st * el + kv for st, el, kv in zip(states, e_lasts[sl], kvs[sl])]
    return outs, states


def _hgrn_body(q_ref, k_ref, v_ref, lf_ref, s0_ref, o_ref, sout_ref, st_ref, *, nb, hb, tl, c, slot):
    t = pl.program_id(2)
    nt = pl.num_programs(2)
    nc = tl // c
    state_t = s0_ref is None

    @pl.when(t == 0)
    def _():
        if s0_ref is None:
            st_ref[...] = jnp.zeros(st_ref.shape, F32)
        else:
            st_ref[...] = s0_ref[...]

    chains = [(b, h) for b in range(nb) for h in range(hb)]
    where = [(ic, b, h) for ic in range(nc) for b, h in chains]

    def load(ref):
        return [ref[b, ic * c:(ic + 1) * c, h * C_DK:(h + 1) * C_DK] for ic, b, h in where]

    states = [st_ref[b, h] for b, h in chains]
    outs, states = _hgrn_steps(load(q_ref), load(k_ref), load(v_ref), load(lf_ref), states, c, state_t)
    for (ic, b, h), o in zip(where, outs):
        o_ref[b, ic * c:(ic + 1) * c, h * C_DK:(h + 1) * C_DK] = o
    for (b, h), st in zip(chains, states):
        st_ref[b, h] = st

    @pl.when(t == nt - 1)
    def _():
        out = _slot_view(sout_ref, slot)
        for b in range(nb):
            for h in range(hb):
                out[b, h] = st_ref[b, h].T if state_t else st_ref[b, h]


def _hgrn_kernel_s0(q_ref, k_ref, v_ref, lf_ref, s0_ref, o_ref, sout_ref, st_ref, **kw):
    _hgrn_body(q_ref, k_ref, v_ref, lf_ref, s0_ref, o_ref, sout_ref, st_ref, **kw)


def _hgrn_kernel_zero(q_ref, k_ref, v_ref, lf_ref, o_ref, sout_ref, st_ref, **kw):
    _hgrn_body(q_ref, k_ref, v_ref, lf_ref, None, o_ref, sout_ref, st_ref, **kw)


def _hgrn(q, k, v, lf, s0, j, stacked, n_layers, *, nb, hb, tl, c):
    n, l, _ = q.shape
    grid = (n // nb, C_HEADS // hb, l // tl)
    row_spec = pl.BlockSpec((nb, tl, hb * C_DK), lambda i, g, t: (i, t, g))
    st_block = (nb, hb, C_DK, C_DK)
    st_map = lambda i, g, t: (i, g, 0, 0)
    first = stacked is None
    st_spec = _stacked_spec(st_block, st_map, j, n_layers, first)
    in_specs = [row_spec] * 4
    args = [q, k, v, lf]
    kw = dict(nb=nb, hb=hb, tl=tl, c=c, slot=(j, n_layers) if first else None)
    if s0 is not None:
        in_specs = in_specs + [_stacked_spec(st_block, st_map, j, n_layers, False)]
        args.append(s0)
        body = functools.partial(_hgrn_kernel_s0, **kw)
    else:
        body = functools.partial(_hgrn_kernel_zero, **kw)
    body, in_specs, args, aliases = _alias_stacked(body, in_specs, args, stacked, 1)
    return pl.pallas_call(
        body, grid=grid, in_specs=in_specs, out_specs=[row_spec, st_spec],
        out_shape=[jax.ShapeDtypeStruct((n, l, D_MODEL), F32),
                   jax.ShapeDtypeStruct((n_layers, n, C_HEADS, C_DK, C_DK), F32)],
        scratch_shapes=[pltpu.VMEM(st_block, F32)], input_output_aliases=aliases,
        compiler_params=_params(3), name="hgrn",
    )(*args)


def _odd_fused_kernel(x_ref, w_ref, lbraw_ref, o_ref, g_ref, sout_ref, st_ref, *, j, tl, c, ts, slot):
    t = pl.program_id(1)
    nt = pl.num_programs(1)

    @pl.when(t == 0)
    def _():
        st_ref[...] = jnp.zeros(st_ref.shape, F32)

    lb = _lower_bound(lbraw_ref, j)
    heads = range(C_HEADS)

    def project(s):
        rows = slice(s * ts, (s + 1) * ts)
        q, k, v, lf, g = _odd_proj(x_ref[0, rows, :].astype(BF16), w_ref, lb)
        g_ref[0, rows, :] = g
        return q, k, v, lf

    def mix(s, qkvl, states):
        where = [(ic, h) for ic in range(ts // c) for h in heads]
        qs, ks, vs, lfs = ([a[ic * c:(ic + 1) * c, h * C_DK:(h + 1) * C_DK] for ic, h in where] for a in qkvl)
        outs, states = _hgrn_steps(qs, ks, vs, lfs, states, c, True)
        for (ic, h), o in zip(where, outs):
            o_ref[0, s * ts + ic * c:s * ts + (ic + 1) * c, h * C_DK:(h + 1) * C_DK] = o
        return states

    states = [st_ref[h] for h in heads]
    n_sub = tl // ts
    pending = project(0)
    for s in range(n_sub):
        following = project(s + 1) if s + 1 < n_sub else None
        states = mix(s, pending, states)
        pending = following
    for h in heads:
        st_ref[h] = states[h]

    @pl.when(t == nt - 1)
    def _():
        out = _slot_view(sout_ref, slot)
        for h in heads:
            out[0, h] = st_ref[h].T


def _odd_fused(x, w, lb_raw, j, stacked, n_layers, *, tl, c, ts=SUB_ROWS):
    n, l, _ = x.shape
    row_spec = pl.BlockSpec((1, tl, D_MODEL), lambda i, t: (i, t, 0))
    st_block = (1, C_HEADS, C_DK, C_DK)
    first = stacked is None
    body = functools.partial(_odd_fused_kernel, j=j, tl=tl, c=c, ts=ts, slot=(j, n_layers) if first else None)
    in_specs = [row_spec, _layer_spec(w, j), _const_spec(lb_raw.shape)]
    body, in_specs, args, aliases = _alias_stacked(body, in_specs, [x, w, lb_raw], stacked, 2)
    shp = jax.ShapeDtypeStruct((n, l, D_MODEL), F32)
    return pl.pallas_call(
        body, grid=(n, l // tl), in_specs=in_specs,
        out_specs=[row_spec, row_spec, _stacked_spec(st_block, lambda i, t: (i, 0, 0, 0), j, n_layers, first)],
        out_shape=[shp, shp, jax.ShapeDtypeStruct((n_layers, n, C_HEADS, C_DK, C_DK), F32)],
        scratch_shapes=[pltpu.VMEM((C_HEADS, C_DK, C_DK), F32)], input_output_aliases=aliases,
        compiler_params=_params(2), name="odd_fused",
    )(*args)


def _group_rms(o, g_row, heads):
    outs = []
    for h in range(heads):
        c = o[:, h * LANES:(h + 1) * LANES]
        inv = lax.rsqrt(jnp.mean(c * c, axis=-1, keepdims=True) + NORM_EPS)
        outs.append(c * inv * g_row)
    return jnp.concatenate(outs, axis=1)


def _post_tail(x, y, p, lng_ref, lnb_ref, wproj_ref, wgate_ref):
    pre = DEEPNORM_ALPHA * x + y
    mu = jnp.mean(pre, axis=-1, keepdims=True)
    xc = pre - mu
    var = jnp.mean(xc * xc, axis=-1, keepdims=True)
    hn = xc * lax.rsqrt(var + NORM_EPS) * lng_ref[...] + lnb_ref[...]
    gate = _sigmoid(jnp.dot(hn.astype(BF16), wgate_ref[:, 0:D_MODEL], preferred_element_type=F32))
    emb = jnp.dot(p.astype(BF16), wproj_ref[:, 0:D_MODEL], preferred_element_type=F32)
    return hn + gate * emb


def _sub_tiles(n_rows):
    ts = min(n_rows, POST_SUB_ROWS)
    return [slice(r0, r0 + ts) for r0 in range(0, n_rows, ts)]


def _post_even_kernel(x_ref, oa_ref, ga_ref, ob_ref, gbt_ref, p_ref, wout_ref, ng_ref, lng_ref, lnb_ref,
                      wproj_ref, wgate_ref, out_ref):
    for rows in _sub_tiles(x_ref.shape[0]):
        a = _group_rms(oa_ref[rows, :], ng_ref[...], A_HEADS) * ga_ref[rows, :]
        b = ob_ref[rows, :] * gbt_ref[rows, :]
        y = (jnp.dot(a.astype(BF16), wout_ref[0:A_VW, 0:D_MODEL], preferred_element_type=F32)
             + jnp.dot(b.astype(BF16), wout_ref[A_VW:, 0:D_MODEL], preferred_element_type=F32))
        out_ref[rows, :] = _post_tail(x_ref[rows, :], y, p_ref[rows, :], lng_ref, lnb_ref, wproj_ref, wgate_ref)


def _post_odd_kernel(x_ref, o_ref, g_ref, p_ref, wout_ref, ng_ref, lng_ref, lnb_ref,
                     wproj_ref, wgate_ref, out_ref):
    for rows in _sub_tiles(x_ref.shape[0]):
        a = _group_rms(o_ref[rows, :], ng_ref[...], C_HEADS) * g_ref[rows, :]
        y = jnp.dot(a.astype(BF16), wout_ref[:, 0:D_MODEL], preferred_element_type=F32)
        out_ref[rows, :] = _post_tail(x_ref[rows, :], y, p_ref[rows, :], lng_ref, lnb_ref, wproj_ref, wgate_ref)


def _post(x2, acts, p3, layer, wout, j, ng, lng, lnb, wproj, wgate, *, even, tr):
    rows = x2.shape[0]
    full = pl.BlockSpec((tr, D_MODEL), lambda i: (i, 0))
    half = pl.BlockSpec((tr, A_VW), lambda i: (i, 0))
    act_specs = [half] * 4 if even else [full] * 2
    in_specs = ([full] + act_specs + [pl.BlockSpec((None, tr, PLE_DIM), lambda i: (layer, i, 0)),
                                      _layer_spec(wout, j), _const_spec((1, LANES)),
                                      _const_spec((1, D_MODEL)), _const_spec((1, D_MODEL)),
                                      _layer_spec(wproj, layer), _layer_spec(wgate, layer)])
    return pl.pallas_call(
        _post_even_kernel if even else _post_odd_kernel, grid=(rows // tr,),
        in_specs=in_specs, out_specs=full, out_shape=jax.ShapeDtypeStruct((rows, D_MODEL), F32),
        compiler_params=_params(1), name="post_even" if even else "post_odd",
    )(x2, *acts, p3, wout, ng, lng, lnb, wproj, wgate)


def _bf16_padded(w):
    return jnp.pad(w.astype(BF16), ((0, 0), (0, 0), (0, LANES)))


def _pack_even_w(w):
    w = w.astype(BF16)
    qkv = w[..., 0:1536]
    ga = w[..., 1536:2048]
    ab = w[..., 2048:2056]
    qb = w[..., 2056:2568]
    kb = w[..., 2568:2696]
    vb = w[..., 2696:2824]
    gbt = w[..., 2824:3336]
    ab = jnp.pad(ab, ((0, 0), (0, 0), (0, LANES - 2 * A_HEADS)))
    return jnp.concatenate([qkv, ga, qb, kb, vb, gbt, ab], axis=-1)


def _rot_tables(pos, reps):
    inv = ROPE_THETA ** (-jnp.arange(0, ROT_DIM, 2, dtype=F32) / ROT_DIM)
    ang = pos.astype(F32)[:, None] * inv[None, :]
    cos = jnp.cos(ang)
    sin = jnp.sin(ang)
    n = pos.shape[0]
    half = ROT_DIM // 2
    cos64 = jnp.concatenate([cos, cos, jnp.ones((n, B_HD - ROT_DIM), F32)], axis=1)
    up64 = jnp.concatenate([-sin, jnp.zeros((n, B_HD - half), F32)], axis=1)
    dn64 = jnp.concatenate([jnp.zeros((n, half), F32), sin, jnp.zeros((n, B_HD - ROT_DIM), F32)], axis=1)
    tabs = [jnp.tile(t, (reps, LANES // B_HD)) for t in (cos64, up64, dn64)]
    return jnp.stack(tabs + [t * (B_HD ** -0.5) for t in tabs])


def kernel(x_prompt, x_sample, state_conv_a, state_delta_a, cache_win_k, cache_win_v, state_hgrn_c,
           p_prompt, p_sample, w_in_even, conv_w_a, a_log, dt_bias, norm_a, sinks_b, w_out_even,
           w_in_odd, lb_raw, norm_c, w_out_odd, ln_g, ln_b, w_ple_proj, w_ple_gate):
    n_p, l_p, _ = x_prompt.shape
    n_s, l_s, _ = x_sample.shape
    nb_s = 32
    rot_p = _rot_tables(jnp.arange(l_p), 1)
    rot_s = _rot_tables(PAST_LEN + jnp.arange(l_s), nb_s)
    n_even, n_odd = w_in_even.shape[0], w_in_odd.shape[0]
    conv_p, conv_s, wk_p, wv_p = [], [], [], []
    delta_p = delta_s = hg_p = hg_s = win_s = None
    hp, hs = x_prompt, x_sample
    pp3 = p_prompt.reshape(DEPTH, n_p * l_p, PLE_DIM)
    ps3 = p_sample.reshape(DEPTH, n_s * l_s, PLE_DIM)
    ck_all = cache_win_k.reshape(-1, n_s, WINDOW, B_KVW)
    cv_all = cache_win_v.reshape(-1, n_s, WINDOW, B_KVW)
    w_even = _pack_even_w(w_in_even)
    w_odd = _bf16_padded(w_in_odd)
    wout_even = _bf16_padded(w_out_even)
    wout_odd = _bf16_padded(w_out_odd)
    wproj = _bf16_padded(w_ple_proj)
    wgate = _bf16_padded(w_ple_gate)
    for layer in range(DEPTH):
        j = layer // 2
        lng = ln_g[layer].reshape(1, D_MODEL)
        lnb = ln_b[layer].reshape(1, D_MODEL)
        hp2 = hp.reshape(n_p * l_p, D_MODEL)
        hs2 = hs.reshape(n_s * l_s, D_MODEL)
        if layer % 2 == 0:
            w = w_even
            misc = jnp.zeros((8, LANES), F32)
            misc = misc.at[0, :A_HEADS].set(a_log[j]).at[1, :A_HEADS].set(dt_bias[j])
            cw = conv_w_a[j]
            sinks = sinks_b[j]
            wout = wout_even
            ng = norm_a[j].reshape(1, LANES)
            qkv, gb, ga, qb, kvb, gbt, c1 = _even_in(hp, w, cw, misc, rot_p, None, j, nb=1, tl=512)
            oa, delta_p = _delta(qkv, gb, None, j, delta_p, n_even, nb=1, tl=512, c=A_CHUNK)
            ob = _swa_prompt(qb, kvb, sinks, nblk=8)
            acts = [t.reshape(n_p * l_p, A_VW) for t in (oa, ga, ob, gbt)]
            hp = _post(hp2, acts, pp3, layer, wout, j, ng, lng, lnb, wproj, wgate, even=True, tr=512)
            hp = hp.reshape(n_p, l_p, D_MODEL)
            k1 = kvb[:, l_p - WINDOW:, :B_KVW].reshape(n_p, WINDOW, B_KV_HEADS, B_HD)
            v1 = kvb[:, l_p - WINDOW:, B_KVW:].reshape(n_p, WINDOW, B_KV_HEADS, B_HD)
            qkv, gb, ga, qb, kvb, gbt, c2 = _even_in(hs, w, cw, misc, rot_s, state_conv_a, j, nb=nb_s, tl=l_s)
            oa, delta_s = _delta(qkv, gb, state_delta_a, j, delta_s, n_even, nb=8, tl=l_s, c=l_s)
            ob, *win_s = _swa_sample(qb, kvb, ck_all, cv_all, sinks, j, win_s, nb=8)
            acts = [t.reshape(n_s * l_s, A_VW) for t in (oa, ga, ob, gbt)]
            hs = _post(hs2, acts, ps3, layer, wout, j, ng, lng, lnb, wproj, wgate, even=True, tr=512)
            hs = hs.reshape(n_s, l_s, D_MODEL)
            conv_p.append(c1); conv_s.append(c2)
            wk_p.append(k1); wv_p.append(v1)
        else:
            w = w_odd
            wout = wout_odd
            ng = norm_c[j].reshape(1, LANES)
            o, g, hg_p = _odd_fused(hp, w, lb_raw, j, hg_p, n_odd, tl=512, c=C_CHUNK)
            acts = [t.reshape(n_p * l_p, D_MODEL) for t in (o, g)]
            hp = _post(hp2, acts, pp3, layer, wout, j, ng, lng, lnb, wproj, wgate, even=False, tr=512)
            hp = hp.reshape(n_p, l_p, D_MODEL)
            q, k, v, lf, g = _odd_in(hs2, w, lb_raw, j, tr=256)
            shp = (n_s, l_s, D_MODEL)
            o, hg_s = _hgrn(q.reshape(shp), k.reshape(shp), v.reshape(shp), lf.reshape(shp), state_hgrn_c, j,
                            hg_s, n_odd, nb=4, hb=8, tl=l_s, c=l_s)
            hs = _post(hs2, [o.reshape(n_s * l_s, D_MODEL), g], ps3, layer, wout, j, ng, lng, lnb, wproj, wgate,
                       even=False, tr=512)
            hs = hs.reshape(n_s, l_s, D_MODEL)
    wk_s, wv_s = (t.reshape(n_even, n_s, WINDOW, B_KV_HEADS, B_HD) for t in win_s)
    return (hp, hs, jnp.stack(conv_p), jnp.stack(conv_s), delta_p, delta_s,
            jnp.stack(wk_p), wk_s, jnp.stack(wv_p), wv_s, hg_p, hg_s)
```

```python
import functools
import math

import jax
import jax.numpy as jnp
from jax import lax
from jax.experimental import pallas as pl
from jax.experimental.pallas import tpu as pltpu

F32 = jnp.float32
BF16 = jnp.bfloat16

D_MODEL = 1024
DEPTH = 4
PAST_LEN = 8192
PLE_DIM = 256
NORM_EPS = 1e-6
MASK_NEG = -1e30
F_FLOOR = 1e-30
DEEPNORM_ALPHA = (2 * DEPTH) ** 0.25

A_HEADS = 4
A_DK = 128
A_CONV = 4
A_QK = 512
A_VW = 512
A_CONV_CH = 1536
A_CHUNK = 64

B_HEADS = 8
B_KV_HEADS = 2
B_HD = 64
B_QW = 512
B_KVW = 128
WINDOW = 128
ROT_DIM = 16
ROPE_THETA = 500000.0

C_HEADS = 8
C_DK = 128
C_CHUNK = 64

LANES = 128
SUBLANES = 8
LOG2_E = math.log2(math.e)
VMEM_LIMIT = 56 * 1024 * 1024
SUB_ROWS = 128
POST_SUB_ROWS = 256

E_QKV = 0
E_GA = 1536
E_QB = 2048
E_KB = 2560
E_VB = 2688
E_GB = 2816
E_AB = 3328
E_TOT = 3456

NT = (((1,), (1,)), ((), ()))
TN = (((0,), (0,)), ((), ()))


def _mm(a, b):
    return jnp.dot(a.astype(BF16), b.astype(BF16), preferred_element_type=F32)


def _mm_nt(a, b):
    return lax.dot_general(a.astype(BF16), b.astype(BF16), NT, preferred_element_type=F32)


def _mm_tn(a, b):
    return lax.dot_general(a.astype(BF16), b.astype(BF16), TN, preferred_element_type=F32)


def _sigmoid(x):
    return 1.0 / (1.0 + jnp.exp(-x))


def _silu(x):
    return x * _sigmoid(x)


def _cumsum_rows(x):
    n = x.shape[0]
    row = lax.broadcasted_iota(jnp.int32, x.shape, 0)
    s = 1
    while s < n:
        x = x + jnp.where(row >= s, pltpu.roll(x, s, axis=0), 0.0)
        s *= 2
    return x


def _params(n_grid):
    return pltpu.CompilerParams(dimension_semantics=("arbitrary",) * n_grid,
                                vmem_limit_bytes=VMEM_LIMIT)


def _const_spec(shape):
    nd = len(shape)
    return pl.BlockSpec(shape, lambda *_: (0,) * nd)


def _layer_spec(stacked, layer):
    nd = stacked.ndim
    return pl.BlockSpec((None,) + stacked.shape[1:], lambda *_: (layer,) + (0,) * (nd - 1))


def _ignore_first_ref(body, _stacked_ref, *refs):
    body(*refs)


def _stacked_spec(block, index_fn, j, n_layers, first):
    if first:
        return pl.BlockSpec((n_layers,) + block, lambda *g: (0,) + index_fn(*g))
    return pl.BlockSpec((None,) + block, lambda *g: (j,) + index_fn(*g))


def _slot_view(ref, slot):
    if slot is None:
        return ref
    j, n_layers = slot
    for i in range(n_layers):
        if i != j:
            ref[i] = jnp.zeros(ref.shape[1:], ref.dtype)
    return ref.at[j]


def _alias_stacked(body, in_specs, args, stacked, *out_indices):
    if stacked is None:
        return body, in_specs, args, {}
    if not isinstance(stacked, (list, tuple)):
        stacked = [stacked]
    for arr in reversed(stacked):
        body = functools.partial(_ignore_first_ref, body)
        in_specs = [pl.BlockSpec(memory_space=pl.ANY)] + list(in_specs)
        args = [arr] + list(args)
    return body, in_specs, args, {i: o for i, o in enumerate(out_indices)}


def _even_in_body(x_ref, w_ref, cw_ref, misc_ref, rot_ref, buf_ref,
                  qkv_ref, gb_ref, ga_ref, qb_ref, kvb_ref, gbt_ref, conv_ref, xp_ref,
                  *, nb, tl):
    t = pl.program_id(1)
    if buf_ref is not None:
        xp_ref[:, 5:8, :] = buf_ref[...]
    else:
        @pl.when(t == 0)
        def _():
            xp_ref[:, 5:8, :] = jnp.zeros((nb, 3, A_CONV_CH), F32)

    neg_a = -jnp.exp(misc_ref[0:1, :])
    dt_bias = misc_ref[1:2, :]
    half = ROT_DIM // 2
    rep = B_QW // LANES

    if nb == 1:
        nbs, ts = 1, min(tl, SUB_ROWS)
    else:
        nbs, ts = max(1, min(nb, SUB_ROWS // tl)), tl
    r = nbs * ts
    for b0 in range(0, nb, nbs):
        for r0 in range(0, tl, ts):
            bs = slice(b0, b0 + nbs)
            rs = slice(r0, r0 + ts)
            flat = slice(b0 * tl + r0, b0 * tl + r0 + r)
            x = x_ref[bs, rs, :].reshape(r, D_MODEL).astype(BF16)

            cw_blk = 2 * A_DK
            for c0 in range(0, A_CONV_CH, cw_blk):
                cols = slice(c0, c0 + cw_blk)
                z = jnp.dot(x, w_ref[:, E_QKV + c0:E_QKV + c0 + cw_blk], preferred_element_type=F32)
                xp_ref[bs, 8 + r0:8 + r0 + ts, cols] = z.reshape(nbs, ts, cw_blk)
                y = xp_ref[bs, 5 + r0:5 + r0 + ts, cols] * cw_ref[0:1, cols].reshape(1, 1, cw_blk)
                for j in range(1, A_CONV):
                    y = y + xp_ref[bs, 5 + j + r0:5 + j + r0 + ts, cols] * cw_ref[j:j + 1, cols].reshape(1, 1, cw_blk)
                y = _silu(y).reshape(r, cw_blk)
                for h0 in range(0, cw_blk, A_DK):
                    c = y[:, h0:h0 + A_DK]
                    if c0 + h0 < 2 * A_QK:
                        inv = lax.rsqrt(jnp.sum(c * c, axis=-1, keepdims=True) + NORM_EPS)
                        if c0 + h0 < A_QK:
                            inv = inv * (A_DK ** -0.5)
                        c = c * inv
                    qkv_ref[bs, rs, c0 + h0:c0 + h0 + A_DK] = c.reshape(nbs, ts, A_DK)

            ab = jnp.dot(x, w_ref[:, E_AB:E_AB + LANES], preferred_element_type=F32)
            sp_in = ab + dt_bias
            softplus = jnp.maximum(sp_in, 0.0) + jnp.log1p(jnp.exp(-jnp.abs(sp_in)))
            lane = lax.broadcasted_iota(jnp.int32, ab.shape, 1)
            gb = jnp.where(lane < A_HEADS, neg_a * softplus, _sigmoid(ab))
            gb_ref[bs, rs, :] = gb.reshape(nbs, ts, LANES)

            ga_ref[bs, rs, :] = _silu(jnp.dot(x, w_ref[:, E_GA:E_GA + A_VW],
                                              preferred_element_type=F32)).reshape(nbs, ts, A_VW)
            gbt_ref[bs, rs, :] = _silu(jnp.dot(x, w_ref[:, E_GB:E_GB + B_QW],
                                               preferred_element_type=F32)).reshape(nbs, ts, B_QW)

            cos_t = rot_ref[0, flat, :]
            s_up = rot_ref[1, flat, :]
            s_dn = rot_ref[2, flat, :]
            zq = jnp.dot(x, w_ref[:, E_QB:E_QB + B_QW], preferred_element_type=F32)
            cos4 = jnp.concatenate([rot_ref[3, flat, :]] * rep, axis=1)
            up4 = jnp.concatenate([rot_ref[4, flat, :]] * rep, axis=1)
            dn4 = jnp.concatenate([rot_ref[5, flat, :]] * rep, axis=1)
            qrot = zq * cos4 + pltpu.roll(zq, B_QW - half, axis=1) * up4 + pltpu.roll(zq, half, axis=1) * dn4
            qb_ref[bs, rs, :] = qrot.reshape(nbs, ts, B_QW)
            zk = jnp.dot(x, w_ref[:, E_KB:E_KB + B_KVW], preferred_element_type=F32)
            krot = zk * cos_t + pltpu.roll(zk, LANES - half, axis=1) * s_up + pltpu.roll(zk, half, axis=1) * s_dn
            kvb_ref[bs, rs, 0:B_KVW] = krot.reshape(nbs, ts, B_KVW)
            zv = jnp.dot(x, w_ref[:, E_VB:E_VB + B_KVW], preferred_element_type=F32)
            kvb_ref[bs, rs, B_KVW:] = zv.reshape(nbs, ts, B_KVW)

    last = xp_ref[:, tl + 5:tl + 8, :]
    conv_ref[...] = last
    if buf_ref is None:
        xp_ref[:, 5:8, :] = last


def _even_in_kernel_buf(x_ref, w_ref, cw_ref, misc_ref, rot_ref, buf_ref, *rest, nb, tl):
    _even_in_body(x_ref, w_ref, cw_ref, misc_ref, rot_ref, buf_ref, *rest, nb=nb, tl=tl)


def _even_in_kernel_nobuf(x_ref, w_ref, cw_ref, misc_ref, rot_ref, *rest, nb, tl):
    _even_in_body(x_ref, w_ref, cw_ref, misc_ref, rot_ref, None, *rest, nb=nb, tl=tl)


def _even_in(x, w, cw, misc, rot, buf, j, *, nb, tl):
    n, l, _ = x.shape
    grid = (n // nb, l // tl)
    r = nb * tl
    row_map = lambda i, t: (i, t, 0)
    rot_map = (lambda i, t: (0, t, 0)) if buf is None else (lambda i, t: (0, 0, 0))
    in_specs = [
        pl.BlockSpec((nb, tl, D_MODEL), row_map),
        _layer_spec(w, j),
        _const_spec((A_CONV, A_CONV_CH)),
        _const_spec((8, LANES)),
        pl.BlockSpec((6, r, LANES), rot_map),
    ]
    args = [x, w, cw, misc, rot]
    if buf is not None:
        in_specs.append(pl.BlockSpec((None, nb, A_CONV - 1, A_CONV_CH), lambda i, t: (j, i, 0, 0)))
        args.append(buf)
        body = functools.partial(_even_in_kernel_buf, nb=nb, tl=tl)
    else:
        body = functools.partial(_even_in_kernel_nobuf, nb=nb, tl=tl)
    widths = (A_CONV_CH, LANES, A_VW, B_QW, 2 * B_KVW, B_QW)
    out_shape = [jax.ShapeDtypeStruct((n, l, wd), F32) for wd in widths]
    out_specs = [pl.BlockSpec((nb, tl, wd), row_map) for wd in widths]
    out_shape.append(jax.ShapeDtypeStruct((n, A_CONV - 1, A_CONV_CH), F32))
    out_specs.append(pl.BlockSpec((nb, A_CONV - 1, A_CONV_CH), lambda i, t: (i, 0, 0)))
    return pl.pallas_call(
        body, grid=grid, in_specs=in_specs, out_specs=out_specs, out_shape=out_shape,
        scratch_shapes=[pltpu.VMEM((nb, tl + 8, A_CONV_CH), F32)],
        compiler_params=_params(2), name="even_in",
    )(*args)


def _tri_inv_all(a_list, c):
    ri = lax.broadcasted_iota(jnp.int32, (c, LANES), 0)
    ci = lax.broadcasted_iota(jnp.int32, (c, LANES), 1)
    eye = jnp.where(ri == ci, 1.0, 0.0)
    ps = [eye for _ in a_list]
    ms = [-a for a in a_list]
    span = 1
    while span < c:
        mps = [_mm(m[:, :c], jnp.concatenate([m, p], axis=1)) for m, p in zip(ms, ps)]
        ms = [mp[:, :LANES] for mp in mps]
        ps = [p + mp[:, LANES:] for p, mp in zip(ps, mps)]
        span *= 2
    return ps


def _delta_prep_all(qs, ks, vs, g_cols, g_rows, betas, g_lasts, c):
    ri = lax.broadcasted_iota(jnp.int32, (c, LANES), 0)
    ci = lax.broadcasted_iota(jnp.int32, (c, LANES), 1)
    causal = ri >= ci
    strict = ri > ci
    decays = [jnp.where(causal, jnp.exp(jnp.where(causal, gc - gr, 0.0)), 0.0) for gc, gr in zip(g_cols, g_rows)]
    kbs = [k * b for k, b in zip(ks, betas)]
    if c < LANES:
        pad = jnp.zeros((LANES - c, A_DK), F32)
        k_rows = [jnp.concatenate([k, pad], axis=0).astype(BF16) for k in ks]
    else:
        k_rows = [k.astype(BF16) for k in ks]
    a_list = [jnp.where(strict, _mm_nt(kb, kr) * d, 0.0) for kb, kr, d in zip(kbs, k_rows, decays)]
    qks = [(_mm_nt(q, kr) * d)[:, :c].astype(BF16) for q, kr, d in zip(qs, k_rows, decays)]
    t_invs = _tri_inv_all(a_list, c)
    e_gs = [jnp.exp(gc) for gc in g_cols]
    uws = [_mm(t[:, :c], jnp.concatenate([v * b, kb * e], axis=1))
           for t, v, b, kb, e in zip(t_invs, vs, betas, kbs, e_gs)]
    us = [uw[:, :A_DK] for uw in uws]
    wqs = [jnp.concatenate([uw[:, A_DK:], q * e], axis=0).astype(BF16) for uw, q, e in zip(uws, qs, e_gs)]
    k_decs = [(k * jnp.exp(gl - gc)).astype(BF16) for k, gl, gc in zip(ks, g_lasts, g_cols)]
    e_lasts = [jnp.exp(gl) for gl in g_lasts]
    return us, wqs, qks, k_decs, e_lasts


def _delta_body(qkv_ref, gb_ref, s0_ref, o_ref, sout_ref, s_ref, *, nb, tl, c, slot):
    t = pl.program_id(1)
    nt = pl.num_programs(1)

    @pl.when(t == 0)
    def _():
        if s0_ref is None:
            s_ref[...] = jnp.zeros(s_ref.shape, F32)
        else:
            s_ref[...] = s0_ref[...]

    nc = tl // c
    qs, ks, vs, g_cols, g_rows, betas, g_lasts = [], [], [], [], [], [], []
    for ic in range(nc):
        rows = slice(ic * c, (ic + 1) * c)
        for b in range(nb):
            gbc = gb_ref[b, rows, :]
            g_all = _cumsum_rows(gbc)
            if c < LANES:
                g_t = jnp.concatenate([g_all, jnp.zeros((LANES - c, LANES), F32)], axis=0).T
            else:
                g_t = g_all.T
            for h in range(A_HEADS):
                qs.append(qkv_ref[b, rows, h * A_DK:(h + 1) * A_DK])
                ks.append(qkv_ref[b, rows, A_QK + h * A_DK:A_QK + (h + 1) * A_DK])
                vs.append(qkv_ref[b, rows, 2 * A_QK + h * A_DK:2 * A_QK + (h + 1) * A_DK])
                g_cols.append(g_all[:, h:h + 1])
                g_rows.append(g_t[h:h + 1, :])
                betas.append(gbc[:, A_HEADS + h:A_HEADS + h + 1])
                g_lasts.append(g_all[c - 1:c, h:h + 1])
    us, wqs, qks, k_decs, e_lasts = _delta_prep_all(qs, ks, vs, g_cols, g_rows, betas, g_lasts, c)

    nch = nb * A_HEADS
    states = [s_ref[b, h] for b in range(nb) for h in range(A_HEADS)]
    for ic in range(nc):
        sl = slice(ic * nch, (ic + 1) * nch)
        ws_qs = [_mm(wq, s) for wq, s in zip(wqs[sl], states)]
        v_news = [(u - x[:c]).astype(BF16) for u, x in zip(us[sl], ws_qs)]
        outs = [x[c:] + _mm(qk, vn) for x, qk, vn in zip(ws_qs, qks[sl], v_news)]
        states = [s * el + _mm_tn(kd, vn) for s, el, kd, vn in zip(states, e_lasts[sl], k_decs[sl], v_news)]
        for i, o in enumerate(outs):
            b, h = divmod(i, A_HEADS)
            o_ref[b, ic * c:(ic + 1) * c, h * A_DK:(h + 1) * A_DK] = o
    for i, s in enumerate(states):
        b, h = divmod(i, A_HEADS)
        s_ref[b, h] = s

    @pl.when(t == nt - 1)
    def _():
        _slot_view(sout_ref, slot)[...] = s_ref[...]


def _delta_kernel_s0(qkv_ref, gb_ref, s0_ref, o_ref, sout_ref, s_ref, **kw):
    _delta_body(qkv_ref, gb_ref, s0_ref, o_ref, sout_ref, s_ref, **kw)


def _delta_kernel_zero(qkv_ref, gb_ref, o_ref, sout_ref, s_ref, **kw):
    _delta_body(qkv_ref, gb_ref, None, o_ref, sout_ref, s_ref, **kw)


def _delta(qkv, gb, s0, j, stacked, n_layers, *, nb, tl, c):
    n, l, _ = qkv.shape
    grid = (n // nb, l // tl)
    row_map = lambda i, t: (i, t, 0)
    st_block = (nb, A_HEADS, A_DK, A_DK)
    st_map = lambda i, t: (i, 0, 0, 0)
    first = stacked is None
    kw = dict(nb=nb, tl=tl, c=c, slot=(j, n_layers) if first else None)
    in_specs = [pl.BlockSpec((nb, tl, A_CONV_CH), row_map), pl.BlockSpec((nb, tl, LANES), row_map)]
    args = [qkv, gb]
    if s0 is not None:
        in_specs.append(_stacked_spec(st_block, st_map, j, n_layers, False))
        args.append(s0)
        body = functools.partial(_delta_kernel_s0, **kw)
    else:
        body = functools.partial(_delta_kernel_zero, **kw)
    body, in_specs, args, aliases = _alias_stacked(body, in_specs, args, stacked, 1)
    return pl.pallas_call(
        body, grid=grid, in_specs=in_specs,
        out_specs=[pl.BlockSpec((nb, tl, A_VW), row_map), _stacked_spec(st_block, st_map, j, n_layers, first)],
        out_shape=[jax.ShapeDtypeStruct((n, l, A_VW), F32),
                   jax.ShapeDtypeStruct((n_layers, n, A_HEADS, A_DK, A_DK), F32)],
        scratch_shapes=[pltpu.VMEM(st_block, F32)], input_output_aliases=aliases,
        compiler_params=_params(2), name="delta",
    )(*args)


def _swa_tasks(qs, kxs, vxs, first_min_key, sink_ref):
    nq = qs[0].shape[0]
    nk = 2 * WINDOW
    rows = B_HEADS * nq
    lo = lax.broadcasted_iota(jnp.int32, (nq, LANES), 1) < B_HD
    ri = lax.broadcasted_iota(jnp.int32, (rows, nk), 0) % nq
    ci = lax.broadcasted_iota(jnp.int32, (rows, nk), 1)
    band = (ci >= ri) & (ci <= ri + WINDOW)
    masks = [band & (ci >= first_min_key)] + [band] * (len(qs) - 1)
    head = lax.broadcasted_iota(jnp.int32, (rows, 1), 0) // nq
    sink = jnp.zeros((rows, 1), F32)
    for h in range(B_HEADS):
        sink = jnp.where(head == h, sink_ref[h], sink)

    def stack_heads(q):
        parts = []
        for h in range(B_HEADS):
            j, e = divmod(h, 2)
            g = h // (B_HEADS // B_KV_HEADS)
            src = q[:, j * LANES:(j + 1) * LANES]
            if e != g:
                src = pltpu.roll(src, B_HD, axis=1)
            parts.append(jnp.where(lo, src, 0.0) if g == 0 else jnp.where(lo, 0.0, src))
        return jnp.concatenate(parts, axis=0).astype(BF16)

    lhs = [stack_heads(q) for q in qs]
    ss = [jnp.where(mask, _mm_nt(x, kx), MASK_NEG) for mask, x, kx in zip(masks, lhs, kxs)]
    ms = [jnp.maximum(jnp.max(s, axis=-1, keepdims=True), sink) for s in ss]
    ps = [jnp.where(mask, jnp.exp(s - m), 0.0) for mask, s, m in zip(masks, ss, ms)]
    dens = [jnp.sum(p, axis=-1, keepdims=True) + jnp.exp(sink - m) for p, m in zip(ps, ms)]
    os_ = [_mm(p, vx) / d for p, d, vx in zip(ps, dens, vxs)]
    outs = []
    for o in os_:
        cols = []
        for j in range(B_HEADS // 2):
            a = o[(2 * j) * nq:(2 * j + 1) * nq]
            b = o[(2 * j + 1) * nq:(2 * j + 2) * nq]
            if j < B_HEADS // 4:
                cols.append(jnp.where(lo, a, pltpu.roll(b, B_HD, axis=1)))
            else:
                cols.append(jnp.where(lo, pltpu.roll(a, B_HD, axis=1), b))
        outs.append(jnp.concatenate(cols, axis=1))
    return outs


def _swa_prompt_kernel(sink_ref, q_ref, kvc_ref, kvp_ref, o_ref, *, nblk):
    i = pl.program_id(1)
    kv = [kvp_ref[0].astype(BF16)] + [kvc_ref[0, t * WINDOW:(t + 1) * WINDOW, :].astype(BF16) for t in range(nblk)]
    qs = [q_ref[0, t * WINDOW:(t + 1) * WINDOW, :] for t in range(nblk)]
    kxs = [jnp.concatenate([kv[t][:, :B_KVW], kv[t + 1][:, :B_KVW]], axis=0) for t in range(nblk)]
    vxs = [jnp.concatenate([kv[t][:, B_KVW:], kv[t + 1][:, B_KVW:]], axis=0) for t in range(nblk)]
    outs = _swa_tasks(qs, kxs, vxs, jnp.where(i > 0, 0, WINDOW), sink_ref)
    for t, o in enumerate(outs):
        o_ref[0, t * WINDOW:(t + 1) * WINDOW, :] = o


def _swa_prompt(qb, kvb, sinks, *, nblk):
    n, l, _ = qb.shape
    tq = nblk * WINDOW
    return pl.pallas_call(
        functools.partial(_swa_prompt_kernel, nblk=nblk), grid=(n, l // tq),
        in_specs=[pl.BlockSpec(memory_space=pltpu.SMEM),
                  pl.BlockSpec((1, tq, B_QW), lambda b, i: (b, i, 0)),
                  pl.BlockSpec((1, tq, 2 * B_KVW), lambda b, i: (b, i, 0)),
                  pl.BlockSpec((1, WINDOW, 2 * B_KVW), lambda b, i: (b, jnp.maximum(i * nblk - 1, 0), 0))],
        out_specs=pl.BlockSpec((1, tq, B_QW), lambda b, i: (b, i, 0)),
        out_shape=jax.ShapeDtypeStruct((n, l, B_QW), F32),
        compiler_params=_params(2), name="swa_prompt",
    )(sinks, qb, kvb, kvb)


def _swa_sample_kernel(sink_ref, q_ref, kv_ref, ck_ref, cv_ref, o_ref, nk_ref, nv_ref, *, nb, ls, slot):
    nk_ref = _slot_view(nk_ref, slot)
    nv_ref = _slot_view(nv_ref, slot)
    pad = jnp.zeros((WINDOW - ls, B_KVW), F32)
    qs, kxs, vxs = [], [], []
    for b in range(nb):
        kv = kv_ref[b]
        k_new = kv[:, :B_KVW]
        v_new = kv[:, B_KVW:]
        ck = ck_ref[b]
        cv = cv_ref[b]
        qs.append(q_ref[b])
        kxs.append(jnp.concatenate([ck, k_new, pad], axis=0).astype(BF16))
        vxs.append(jnp.concatenate([cv, v_new, pad], axis=0).astype(BF16))
        nk_ref[b, 0:WINDOW - ls, :] = ck[ls:, :]
        nk_ref[b, WINDOW - ls:, :] = k_new
        nv_ref[b, 0:WINDOW - ls, :] = cv[ls:, :]
        nv_ref[b, WINDOW - ls:, :] = v_new
    for b, o in enumerate(_swa_tasks(qs, kxs, vxs, 0, sink_ref)):
        o_ref[b] = o


def _swa_sample(qb, kvb, ck, cv, sinks, j, stacked, *, nb):
    n, ls, _ = qb.shape
    row_map = lambda i: (i, 0, 0)
    n_layers = ck.shape[0]
    first = stacked is None
    cache_block = (nb, WINDOW, B_KVW)
    cache_spec = _stacked_spec(cache_block, row_map, j, n_layers, False)
    cache_out = _stacked_spec(cache_block, row_map, j, n_layers, first)
    cache_shape = jax.ShapeDtypeStruct(ck.shape, F32)
    body = functools.partial(_swa_sample_kernel, nb=nb, ls=ls, slot=(j, n_layers) if first else None)
    in_specs = [pl.BlockSpec(memory_space=pltpu.SMEM),
                pl.BlockSpec((nb, ls, B_QW), row_map),
                pl.BlockSpec((nb, ls, 2 * B_KVW), row_map),
                cache_spec, cache_spec]
    args = [sinks, qb, kvb, ck, cv]
    body, in_specs, args, aliases = _alias_stacked(body, in_specs, args, stacked, 1, 2)
    return pl.pallas_call(
        body, grid=(n // nb,), in_specs=in_specs,
        out_specs=[pl.BlockSpec((nb, ls, B_QW), row_map), cache_out, cache_out],
        out_shape=[jax.ShapeDtypeStruct((n, ls, B_QW), F32), cache_shape, cache_shape],
        input_output_aliases=aliases, compiler_params=_params(1), name="swa_sample",
    )(*args)


def _lower_bound(lbraw_ref, j):
    raw = lbraw_ref[...]
    e = jnp.exp(raw - jnp.max(raw, axis=0, keepdims=True))
    sm = e / jnp.sum(e, axis=0, keepdims=True)
    lb = jnp.zeros((1, D_MODEL), F32)
    for i in range(1, j + 1):
        lb = lb + sm[i:i + 1, :]
    return lb


def _odd_proj(x, w_ref, lb):
    d = D_MODEL

    def proj(c0):
        return jnp.dot(x, w_ref[:, c0:c0 + d], preferred_element_type=F32)

    q = _silu(proj(0)) * (C_DK ** -0.5)
    zf = proj(d)
    one_m_lb = 1.0 - lb
    u = one_m_lb * _sigmoid(zf)
    lf = jnp.log(jnp.maximum(lb + u, F_FLOOR))
    k = one_m_lb - u
    v = proj(2 * d)
    g = _silu(proj(3 * d))
    return q, k, v, lf, g


def _odd_in_kernel(x_ref, w_ref, lbraw_ref, q_ref, k_ref, v_ref, lf_ref, g_ref, *, j, ts):
    lb = _lower_bound(lbraw_ref, j)
    for s in range(x_ref.shape[0] // ts):
        rows = slice(s * ts, (s + 1) * ts)
        q, k, v, lf, g = _odd_proj(x_ref[rows, :].astype(BF16), w_ref, lb)
        q_ref[rows, :] = q
        k_ref[rows, :] = k
        v_ref[rows, :] = v
        lf_ref[rows, :] = lf
        g_ref[rows, :] = g


def _odd_in(x2, w, lb_raw, j, *, tr, ts=SUB_ROWS):
    rows = x2.shape[0]
    n_odd = lb_raw.shape[0]
    row_spec = pl.BlockSpec((tr, D_MODEL), lambda i: (i, 0))
    shp = jax.ShapeDtypeStruct((rows, D_MODEL), F32)
    return pl.pallas_call(
        functools.partial(_odd_in_kernel, j=j, ts=min(ts, tr)), grid=(rows // tr,),
        in_specs=[row_spec, _layer_spec(w, j), _const_spec((n_odd, D_MODEL))],
        out_specs=[row_spec] * 5, out_shape=[shp] * 5,
        compiler_params=_params(1), name="odd_in",
    )(x2, w, lb_raw)


def _hgrn_prep_all(qs, ks, vs, lfs, c, state_t):
    ng = c // SUBLANES
    shape3 = (ng, SUBLANES, C_DK)
    sub = lax.broadcasted_iota(jnp.int32, shape3, 1)

    def grouped(x):
        return x.reshape(shape3)

    def flat(x):
        return x.reshape(c, C_DK)

    gs = []
    for lf in lfs:
        x = grouped(lf * LOG2_E)
        s = 1
        while s < SUBLANES:
            x = x + jnp.where(sub >= s, pltpu.roll(x, s, axis=1), 0.0)
            s *= 2
        if ng > 1:
            tot = x[:, SUBLANES - 1:SUBLANES, :]
            run = jnp.zeros((1, 1, C_DK), F32)
            offs = []
            for i in range(ng):
                offs.append(run)
                run = run + tot[i:i + 1]
            x = x + jnp.concatenate(offs, axis=0)
        gs.append(x)
    rbs = [jnp.broadcast_to(g[:, 0:1, :], shape3) for g in gs]

    def block_start(rb, m):
        x = rb.reshape(ng // m, m, SUBLANES, C_DK)[:, 0:1]
        return jnp.broadcast_to(x, (ng // m, m, SUBLANES, C_DK)).reshape(shape3)

    ri = lax.broadcasted_iota(jnp.int32, (c, c), 0)
    ci = lax.broadcasted_iota(jnp.int32, (c, c), 1)
    q3s = [grouped(q) for q in qs]
    k3s = [grouped(k) for k in ks]
    accs = None
    bs = list(gs)
    h = 1
    while h < c:
        if h < SUBLANES:
            bns = [pltpu.roll(b, SUBLANES - h, axis=1) for b in bs]
        else:
            m = h // SUBLANES
            bs = [block_start(rb, m) for rb in rbs]
            bns = [jnp.concatenate([b[m:], g[ng - m:]], axis=0) for b, g in zip(bs, gs)]
        qhs = q3s if h == 1 else [q * jnp.exp2(g - b) for q, g, b in zip(q3s, gs, bs)]
        khs = [k * jnp.exp2(bn - g) for k, bn, g in zip(k3s, bns, gs)]
        lvl = ((ri // h) == (ci // h) + 1) & ((ri // (2 * h)) == (ci // (2 * h)))
        if h == 1 and ng > 1:
            gap = jnp.zeros((LANES - c, C_DK), F32)
            both = [_mm_nt(flat(q), jnp.concatenate([flat(k), gap, flat(kh)], axis=0))
                    for q, k, kh in zip(q3s, k3s, khs)]
            accs = [jnp.where(ri == ci, x[:, :c], jnp.where(lvl, x[:, LANES:], 0.0)) for x in both]
        elif h == 1:
            accs = [jnp.where(ri == ci, _mm_nt(flat(q), flat(k)), jnp.where(lvl, _mm_nt(flat(q), flat(kh)), 0.0))
                    for q, k, kh in zip(q3s, k3s, khs)]
        else:
            accs = [jnp.where(lvl, _mm_nt(flat(qh), flat(kh)), a) for a, qh, kh in zip(accs, qhs, khs)]
        if 2 * h < SUBLANES:
            sel = (sub & h) != 0
            bs = [jnp.where(sel, pltpu.roll(b, h, axis=1), b) for b in bs]
        h *= 2
    if accs is None:
        accs = [jnp.where(ri == ci, _mm_nt(q, k), 0.0) for q, k in zip(qs, ks)]
    g_lasts = [g[ng - 1, SUBLANES - 1:SUBLANES, :] for g in gs]
    intra = [_mm(a, v) for a, v in zip(accs, vs)]
    q_decs = [flat(q * jnp.exp2(g)).astype(BF16) for q, g in zip(q3s, gs)]
    k_decs = [flat(k * jnp.exp2(gl - g)) for k, gl, g in zip(k3s, g_lasts, gs)]
    e_lasts = [jnp.exp2(gl) for gl in g_lasts]
    if state_t:
        kvs = [_mm_tn(v, kd) for v, kd in zip(vs, k_decs)]
    else:
        kvs = [_mm_tn(kd, v) for v, kd in zip(vs, k_decs)]
        e_lasts = [jnp.broadcast_to(el, (C_DK, C_DK)).T for el in e_lasts]
    return intra, q_decs, kvs, e_lasts


def _hgrn_steps(qs, ks, vs, lfs, states, c, state_t):
    intra, q_decs, kvs, e_lasts = _hgrn_prep_all(qs, ks, vs, lfs, c, state_t)
    nch = len(states)
    read_state = _mm_nt if state_t else _mm
    outs = []
    for ic in range(len(qs) // nch):
        sl = slice(ic * nch, (ic + 1) * nch)
        outs += [read_state(qd, st) + x for qd, st, x in zip(q_decs[sl], states, intra[sl])]
        states = [st * el + kv for st, el, kv in zip(states, e_lasts[sl], kvs[sl])]
    return outs, states


def _hgrn_body(q_ref, k_ref, v_ref, lf_ref, s0_ref, o_ref, sout_ref, st_ref, *, nb, hb, tl, c, slot):
    t = pl.program_id(2)
    nt = pl.num_programs(2)
    nc = tl // c
    state_t = s0_ref is None

    @pl.when(t == 0)
    def _():
        if s0_ref is None:
            st_ref[...] = jnp.zeros(st_ref.shape, F32)
        else:
            st_ref[...] = s0_ref[...]

    chains = [(b, h) for b in range(nb) for h in range(hb)]
    where = [(ic, b, h) for ic in range(nc) for b, h in chains]

    def load(ref):
        return [ref[b, ic * c:(ic + 1) * c, h * C_DK:(h + 1) * C_DK] for ic, b, h in where]

    states = [st_ref[b, h] for b, h in chains]
    outs, states = _hgrn_steps(load(q_ref), load(k_ref), load(v_ref), load(lf_ref), states, c, state_t)
    for (ic, b, h), o in zip(where, outs):
        o_ref[b, ic * c:(ic + 1) * c, h * C_DK:(h + 1) * C_DK] = o
    for (b, h), st in zip(chains, states):
        st_ref[b, h] = st

    @pl.when(t == nt - 1)
    def _():
        out = _slot_view(sout_ref, slot)
        for b in range(nb):
            for h in range(hb):
                out[b, h] = st_ref[b, h].T if state_t else st_ref[b, h]


def _hgrn_kernel_s0(q_ref, k_ref, v_ref, lf_ref, s0_ref, o_ref, sout_ref, st_ref, **kw):
    _hgrn_body(q_ref, k_ref, v_ref, lf_ref, s0_ref, o_ref, sout_ref, st_ref, **kw)


def _hgrn_kernel_zero(q_ref, k_ref, v_ref, lf_ref, o_ref, sout_ref, st_ref, **kw):
    _hgrn_body(q_ref, k_ref, v_ref, lf_ref, None, o_ref, sout_ref, st_ref, **kw)


def _hgrn(q, k, v, lf, s0, j, stacked, n_layers, *, nb, hb, tl, c):
    n, l, _ = q.shape
    grid = (n // nb, C_HEADS // hb, l // tl)
    row_spec = pl.BlockSpec((nb, tl, hb * C_DK), lambda i, g, t: (i, t, g))
    st_block = (nb, hb, C_DK, C_DK)
    st_map = lambda i, g, t: (i, g, 0, 0)
    first = stacked is None
    st_spec = _stacked_spec(st_block, st_map, j, n_layers, first)
    in_specs = [row_spec] * 4
    args = [q, k, v, lf]
    kw = dict(nb=nb, hb=hb, tl=tl, c=c, slot=(j, n_layers) if first else None)
    if s0 is not None:
        in_specs = in_specs + [_stacked_spec(st_block, st_map, j, n_layers, False)]
        args.append(s0)
        body = functools.partial(_hgrn_kernel_s0, **kw)
    else:
        body = functools.partial(_hgrn_kernel_zero, **kw)
    body, in_specs, args, aliases = _alias_stacked(body, in_specs, args, stacked, 1)
    return pl.pallas_call(
        body, grid=grid, in_specs=in_specs, out_specs=[row_spec, st_spec],
        out_shape=[jax.ShapeDtypeStruct((n, l, D_MODEL), F32),
                   jax.ShapeDtypeStruct((n_layers, n, C_HEADS, C_DK, C_DK), F32)],
        scratch_shapes=[pltpu.VMEM(st_block, F32)], input_output_aliases=aliases,
        compiler_params=_params(3), name="hgrn",
    )(*args)


def _odd_fused_kernel(x_ref, w_ref, lbraw_ref, o_ref, g_ref, sout_ref, st_ref, *, j, tl, c, ts, slot):
    t = pl.program_id(1)
    nt = pl.num_programs(1)

    @pl.when(t == 0)
    def _():
        st_ref[...] = jnp.zeros(st_ref.shape, F32)

    lb = _lower_bound(lbraw_ref, j)
    heads = range(C_HEADS)

    def project(s):
        rows = slice(s * ts, (s + 1) * ts)
        q, k, v, lf, g = _odd_proj(x_ref[0, rows, :].astype(BF16), w_ref, lb)
        g_ref[0, rows, :] = g
        return q, k, v, lf

    def mix(s, qkvl, states):
        where = [(ic, h) for ic in range(ts // c) for h in heads]
        qs, ks, vs, lfs = ([a[ic * c:(ic + 1) * c, h * C_DK:(h + 1) * C_DK] for ic, h in where] for a in qkvl)
        outs, states = _hgrn_steps(qs, ks, vs, lfs, states, c, True)
        for (ic, h), o in zip(where, outs):
            o_ref[0, s * ts + ic * c:s * ts + (ic + 1) * c, h * C_DK:(h + 1) * C_DK] = o
        return states

    states = [st_ref[h] for h in heads]
    n_sub = tl // ts
    pending = project(0)
    for s in range(n_sub):
        following = project(s + 1) if s + 1 < n_sub else None
        states = mix(s, pending, states)
        pending = following
    for h in heads:
        st_ref[h] = states[h]

    @pl.when(t == nt - 1)
    def _():
        out = _slot_view(sout_ref, slot)
        for h in heads:
            out[0, h] = st_ref[h].T


def _odd_fused(x, w, lb_raw, j, stacked, n_layers, *, tl, c, ts=SUB_ROWS):
    n, l, _ = x.shape
    row_spec = pl.BlockSpec((1, tl, D_MODEL), lambda i, t: (i, t, 0))
    st_block = (1, C_HEADS, C_DK, C_DK)
    first = stacked is None
    body = functools.partial(_odd_fused_kernel, j=j, tl=tl, c=c, ts=ts, slot=(j, n_layers) if first else None)
    in_specs = [row_spec, _layer_spec(w, j), _const_spec(lb_raw.shape)]
    body, in_specs, args, aliases = _alias_stacked(body, in_specs, [x, w, lb_raw], stacked, 2)
    shp = jax.ShapeDtypeStruct((n, l, D_MODEL), F32)
    return pl.pallas_call(
        body, grid=(n, l // tl), in_specs=in_specs,
        out_specs=[row_spec, row_spec, _stacked_spec(st_block, lambda i, t: (i, 0, 0, 0), j, n_layers, first)],
        out_shape=[shp, shp, jax.ShapeDtypeStruct((n_layers, n, C_HEADS, C_DK, C_DK), F32)],
        scratch_shapes=[pltpu.VMEM((C_HEADS, C_DK, C_DK), F32)], input_output_aliases=aliases,
        compiler_params=_params(2), name="odd_fused",
    )(*args)


def _group_rms(o, g_row, heads):
    outs = []
    for h in range(heads):
        c = o[:, h * LANES:(h + 1) * LANES]
        inv = lax.rsqrt(jnp.mean(c * c, axis=-1, keepdims=True) + NORM_EPS)
        outs.append(c * inv * g_row)
    return jnp.concatenate(outs, axis=1)


def _post_tail(x_ref, ys, p_ref, lng_ref, lnb_ref, wproj_ref, wgate_ref, out_ref, tiles):
    hns = []
    for rows, y in zip(tiles, ys):
        pre = DEEPNORM_ALPHA * x_ref[rows, :] + y
        mu = jnp.mean(pre, axis=-1, keepdims=True)
        xc = pre - mu
        var = jnp.mean(xc * xc, axis=-1, keepdims=True)
        hns.append(xc * lax.rsqrt(var + NORM_EPS) * lng_ref[...] + lnb_ref[...])
    embs = [jnp.dot(p_ref[rows, :].astype(BF16), wproj_ref[:, 0:D_MODEL], preferred_element_type=F32)
            for rows in tiles]
    gates = [_sigmoid(jnp.dot(hn.astype(BF16), wgate_ref[:, 0:D_MODEL], preferred_element_type=F32)) for hn in hns]
    for rows, hn, gate, emb in zip(tiles, hns, gates, embs):
        out_ref[rows, :] = hn + gate * emb


def _sub_tiles(n_rows):
    ts = min(n_rows, POST_SUB_ROWS)
    return [slice(r0, r0 + ts) for r0 in range(0, n_rows, ts)]


def _post_even_kernel(x_ref, oa_ref, ga_ref, ob_ref, gbt_ref, p_ref, wout_ref, ng_ref, lng_ref, lnb_ref,
                      wproj_ref, wgate_ref, out_ref):
    tiles = _sub_tiles(x_ref.shape[0])
    a_s = [(_group_rms(oa_ref[rows, :], ng_ref[...], A_HEADS) * ga_ref[rows, :]).astype(BF16) for rows in tiles]
    b_s = [(ob_ref[rows, :] * gbt_ref[rows, :]).astype(BF16) for rows in tiles]
    ys = [jnp.dot(a, wout_ref[0:A_VW, 0:D_MODEL], preferred_element_type=F32)
          + jnp.dot(b, wout_ref[A_VW:, 0:D_MODEL], preferred_element_type=F32) for a, b in zip(a_s, b_s)]
    _post_tail(x_ref, ys, p_ref, lng_ref, lnb_ref, wproj_ref, wgate_ref, out_ref, tiles)


def _post_odd_kernel(x_ref, o_ref, g_ref, p_ref, wout_ref, ng_ref, lng_ref, lnb_ref,
                     wproj_ref, wgate_ref, out_ref):
    tiles = _sub_tiles(x_ref.shape[0])
    a_s = [(_group_rms(o_ref[rows, :], ng_ref[...], C_HEADS) * g_ref[rows, :]).astype(BF16) for rows in tiles]
    ys = [jnp.dot(a, wout_ref[:, 0:D_MODEL], preferred_element_type=F32) for a in a_s]
    _post_tail(x_ref, ys, p_ref, lng_ref, lnb_ref, wproj_ref, wgate_ref, out_ref, tiles)


def _post(x2, acts, p3, layer, wout, j, ng, lng, lnb, wproj, wgate, *, even, tr):
    rows = x2.shape[0]
    full = pl.BlockSpec((tr, D_MODEL), lambda i: (i, 0))
    half = pl.BlockSpec((tr, A_VW), lambda i: (i, 0))
    act_specs = [half] * 4 if even else [full] * 2
    in_specs = ([full] + act_specs + [pl.BlockSpec((None, tr, PLE_DIM), lambda i: (layer, i, 0)),
                                      _layer_spec(wout, j), _const_spec((1, LANES)),
                                      _const_spec((1, D_MODEL)), _const_spec((1, D_MODEL)),
                                      _layer_spec(wproj, layer), _layer_spec(wgate, layer)])
    return pl.pallas_call(
        _post_even_kernel if even else _post_odd_kernel, grid=(rows // tr,),
        in_specs=in_specs, out_specs=full, out_shape=jax.ShapeDtypeStruct((rows, D_MODEL), F32),
        compiler_params=_params(1), name="post_even" if even else "post_odd",
    )(x2, *acts, p3, wout, ng, lng, lnb, wproj, wgate)


def _bf16_padded(w):
    return jnp.pad(w.astype(BF16), ((0, 0), (0, 0), (0, LANES)))


def _pack_even_w(w):
    w = w.astype(BF16)
    qkv = w[..., 0:1536]
    ga = w[..., 1536:2048]
    ab = w[..., 2048:2056]
    qb = w[..., 2056:2568]
    kb = w[..., 2568:2696]
    vb = w[..., 2696:2824]
    gbt = w[..., 2824:3336]
    ab = jnp.pad(ab, ((0, 0), (0, 0), (0, LANES - 2 * A_HEADS)))
    return jnp.concatenate([qkv, ga, qb, kb, vb, gbt, ab], axis=-1)


def _rot_tables(pos, reps):
    inv = ROPE_THETA ** (-jnp.arange(0, ROT_DIM, 2, dtype=F32) / ROT_DIM)
    ang = pos.astype(F32)[:, None] * inv[None, :]
    cos = jnp.cos(ang)
    sin = jnp.sin(ang)
    n = pos.shape[0]
    half = ROT_DIM // 2
    cos64 = jnp.concatenate([cos, cos, jnp.ones((n, B_HD - ROT_DIM), F32)], axis=1)
    up64 = jnp.concatenate([-sin, jnp.zeros((n, B_HD - half), F32)], axis=1)
    dn64 = jnp.concatenate([jnp.zeros((n, half), F32), sin, jnp.zeros((n, B_HD - ROT_DIM), F32)], axis=1)
    tabs = [jnp.tile(t, (reps, LANES // B_HD)) for t in (cos64, up64, dn64)]
    return jnp.stack(tabs + [t * (B_HD ** -0.5) for t in tabs])


def kernel(x_prompt, x_sample, state_conv_a, state_delta_a, cache_win_k, cache_win_v, state_hgrn_c,
           p_prompt, p_sample, w_in_even, conv_w_a, a_log, dt_bias, norm_a, sinks_b, w_out_even,
           w_in_odd, lb_raw, norm_c, w_out_odd, ln_g, ln_b, w_ple_proj, w_ple_gate):
    n_p, l_p, _ = x_prompt.shape
    n_s, l_s, _ = x_sample.shape
    nb_s = 32
    rot_p = _rot_tables(jnp.arange(l_p), 1)
    rot_s = _rot_tables(PAST_LEN + jnp.arange(l_s), nb_s)
    n_even, n_odd = w_in_even.shape[0], w_in_odd.shape[0]
    conv_p, conv_s, wk_p, wv_p = [], [], [], []
    delta_p = delta_s = hg_p = hg_s = win_s = None
    hp, hs = x_prompt, x_sample
    pp3 = p_prompt.reshape(DEPTH, n_p * l_p, PLE_DIM)
    ps3 = p_sample.reshape(DEPTH, n_s * l_s, PLE_DIM)
    ck_all = cache_win_k.reshape(-1, n_s, WINDOW, B_KVW)
    cv_all = cache_win_v.reshape(-1, n_s, WINDOW, B_KVW)
    w_even = _pack_even_w(w_in_even)
    w_odd = _bf16_padded(w_in_odd)
    wout_even = _bf16_padded(w_out_even)
    wout_odd = _bf16_padded(w_out_odd)
    wproj = _bf16_padded(w_ple_proj)
    wgate = _bf16_padded(w_ple_gate)
    for layer in range(DEPTH):
        j = layer // 2
        lng = ln_g[layer].reshape(1, D_MODEL)
        lnb = ln_b[layer].reshape(1, D_MODEL)
        hp2 = hp.reshape(n_p * l_p, D_MODEL)
        hs2 = hs.reshape(n_s * l_s, D_MODEL)
        if layer % 2 == 0:
            w = w_even
            misc = jnp.zeros((8, LANES), F32)
            misc = misc.at[0, :A_HEADS].set(a_log[j]).at[1, :A_HEADS].set(dt_bias[j])
            cw = conv_w_a[j]
            sinks = sinks_b[j]
            wout = wout_even
            ng = norm_a[j].reshape(1, LANES)
            qkv, gb, ga, qb, kvb, gbt, c1 = _even_in(hp, w, cw, misc, rot_p, None, j, nb=1, tl=512)
            oa, delta_p = _delta(qkv, gb, None, j, delta_p, n_even, nb=1, tl=512, c=A_CHUNK)
            ob = _swa_prompt(qb, kvb, sinks, nblk=8)
            acts = [t.reshape(n_p * l_p, A_VW) for t in (oa, ga, ob, gbt)]
            hp = _post(hp2, acts, pp3, layer, wout, j, ng, lng, lnb, wproj, wgate, even=True, tr=512)
            hp = hp.reshape(n_p, l_p, D_MODEL)
            k1 = kvb[:, l_p - WINDOW:, :B_KVW].reshape(n_p, WINDOW, B_KV_HEADS, B_HD)
            v1 = kvb[:, l_p - WINDOW:, B_KVW:].reshape(n_p, WINDOW, B_KV_HEADS, B_HD)
            qkv, gb, ga, qb, kvb, gbt, c2 = _even_in(hs, w, cw, misc, rot_s, state_conv_a, j, nb=nb_s, tl=l_s)
            oa, delta_s = _delta(qkv, gb, state_delta_a, j, delta_s, n_even, nb=8, tl=l_s, c=l_s)
            ob, *win_s = _swa_sample(qb, kvb, ck_all, cv_all, sinks, j, win_s, nb=8)
            acts = [t.reshape(n_s * l_s, A_VW) for t in (oa, ga, ob, gbt)]
            hs = _post(hs2, acts, ps3, layer, wout, j, ng, lng, lnb, wproj, wgate, even=True, tr=512)
            hs = hs.reshape(n_s, l_s, D_MODEL)
            conv_p.append(c1); conv_s.append(c2)
            wk_p.append(k1); wv_p.append(v1)
        else:
            w = w_odd
            wout = wout_odd
            ng = norm_c[j].reshape(1, LANES)
            o, g, hg_p = _odd_fused(hp, w, lb_raw, j, hg_p, n_odd, tl=1024, c=C_CHUNK)
            acts = [t.reshape(n_p * l_p, D_MODEL) for t in (o, g)]
            hp = _post(hp2, acts, pp3, layer, wout, j, ng, lng, lnb, wproj, wgate, even=False, tr=512)
            hp = hp.reshape(n_p, l_p, D_MODEL)
            q, k, v, lf, g = _odd_in(hs2, w, lb_raw, j, tr=256)
            shp = (n_s, l_s, D_MODEL)
            o, hg_s = _hgrn(q.reshape(shp), k.reshape(shp), v.reshape(shp), lf.reshape(shp), state_hgrn_c, j,
                            hg_s, n_odd, nb=4, hb=8, tl=l_s, c=l_s)
            hs = _post(hs2, [o.reshape(n_s * l_s, D_MODEL), g], ps3, layer, wout, j, ng, lng, lnb, wproj, wgate,
                       even=False, tr=512)
            hs = hs.reshape(n_s, l_s, D_MODEL)
    wk_s, wv_s = (t.reshape(n_even, n_s, WINDOW, B_KV_HEADS, B_HD) for t in win_s)
    return (hp, hs, jnp.stack(conv_p), jnp.stack(conv_s), delta_p, delta_s,
            jnp.stack(wk_p), wk_s, jnp.stack(wv_p), wv_s, hg_p, hg_s)
```

```python
import functools
import math

import jax
import jax.numpy as jnp
from jax import lax
from jax.experimental import pallas as pl
from jax.experimental.pallas import tpu as pltpu

F32 = jnp.float32
BF16 = jnp.bfloat16

D_MODEL = 1024
DEPTH = 4
PAST_LEN = 8192
PLE_DIM = 256
NORM_EPS = 1e-6
MASK_NEG = -1e30
F_FLOOR = 1e-30
DEEPNORM_ALPHA = (2 * DEPTH) ** 0.25

A_HEADS = 4
A_DK = 128
A_CONV = 4
A_QK = 512
A_VW = 512
A_CONV_CH = 1536
A_CHUNK = 64

B_HEADS = 8
B_KV_HEADS = 2
B_HD = 64
B_QW = 512
B_KVW = 128
WINDOW = 128
ROT_DIM = 16
ROPE_THETA = 500000.0

C_HEADS = 8
C_DK = 128
C_CHUNK = 64

LANES = 128
SUBLANES = 8
LOG2_E = math.log2(math.e)
VMEM_LIMIT = 56 * 1024 * 1024
SUB_ROWS = 128
ACT_DTYPE = BF16
POST_SUB_ROWS = 256

E_QKV = 0
E_GA = 1536
E_QB = 2048
E_KB = 2560
E_VB = 2688
E_GB = 2816
E_AB = 3328
E_TOT = 3456

NT = (((1,), (1,)), ((), ()))
TN = (((0,), (0,)), ((), ()))


def _mm(a, b):
    return jnp.dot(a.astype(BF16), b.astype(BF16), preferred_element_type=F32)


def _mm_nt(a, b):
    return lax.dot_general(a.astype(BF16), b.astype(BF16), NT, preferred_element_type=F32)


def _mm_tn(a, b):
    return lax.dot_general(a.astype(BF16), b.astype(BF16), TN, preferred_element_type=F32)


def _sigmoid(x):
    return 1.0 / (1.0 + jnp.exp(-x))


def _silu(x):
    return x * _sigmoid(x)


def _cumsum_rows(x):
    n = x.shape[0]
    row = lax.broadcasted_iota(jnp.int32, x.shape, 0)
    s = 1
    while s < n:
        x = x + jnp.where(row >= s, pltpu.roll(x, s, axis=0), 0.0)
        s *= 2
    return x


def _params(n_grid):
    return pltpu.CompilerParams(dimension_semantics=("arbitrary",) * n_grid,
                                vmem_limit_bytes=VMEM_LIMIT)


def _const_spec(shape):
    nd = len(shape)
    return pl.BlockSpec(shape, lambda *_: (0,) * nd)


def _layer_spec(stacked, layer):
    nd = stacked.ndim
    return pl.BlockSpec((None,) + stacked.shape[1:], lambda *_: (layer,) + (0,) * (nd - 1))


def _ignore_first_ref(body, _stacked_ref, *refs):
    body(*refs)


def _stacked_spec(block, index_fn, j, n_layers, first):
    if first:
        return pl.BlockSpec((n_layers,) + block, lambda *g: (0,) + index_fn(*g))
    return pl.BlockSpec((None,) + block, lambda *g: (j,) + index_fn(*g))


def _slot_view(ref, slot):
    if slot is None:
        return ref
    j, n_layers = slot
    for i in range(n_layers):
        if i != j:
            ref[i] = jnp.zeros(ref.shape[1:], ref.dtype)
    return ref.at[j]


def _alias_stacked(body, in_specs, args, stacked, *out_indices):
    if stacked is None:
        return body, in_specs, args, {}
    if not isinstance(stacked, (list, tuple)):
        stacked = [stacked]
    for arr in reversed(stacked):
        body = functools.partial(_ignore_first_ref, body)
        in_specs = [pl.BlockSpec(memory_space=pl.ANY)] + list(in_specs)
        args = [arr] + list(args)
    return body, in_specs, args, {i: o for i, o in enumerate(out_indices)}


def _even_in_body(x_ref, w_ref, cw_ref, misc_ref, rot_ref, buf_ref,
                  qkv_ref, gb_ref, ga_ref, qb_ref, kvb_ref, gbt_ref, conv_ref, xp_ref,
                  *, nb, tl):
    t = pl.program_id(1)
    if buf_ref is not None:
        xp_ref[:, 5:8, :] = buf_ref[...]
    else:
        @pl.when(t == 0)
        def _():
            xp_ref[:, 5:8, :] = jnp.zeros((nb, 3, A_CONV_CH), F32)

    neg_a = -jnp.exp(misc_ref[0:1, :])
    dt_bias = misc_ref[1:2, :]
    half = ROT_DIM // 2
    rep = B_QW // LANES

    if nb == 1:
        nbs, ts = 1, min(tl, SUB_ROWS)
    else:
        nbs, ts = max(1, min(nb, SUB_ROWS // tl)), tl
    r = nbs * ts
    for b0 in range(0, nb, nbs):
        for r0 in range(0, tl, ts):
            bs = slice(b0, b0 + nbs)
            rs = slice(r0, r0 + ts)
            flat = slice(b0 * tl + r0, b0 * tl + r0 + r)
            x = x_ref[bs, rs, :].reshape(r, D_MODEL).astype(BF16)

            cw_blk = 2 * A_DK
            for c0 in range(0, A_CONV_CH, cw_blk):
                cols = slice(c0, c0 + cw_blk)
                z = jnp.dot(x, w_ref[:, E_QKV + c0:E_QKV + c0 + cw_blk], preferred_element_type=F32)
                xp_ref[bs, 8 + r0:8 + r0 + ts, cols] = z.reshape(nbs, ts, cw_blk)
                y = xp_ref[bs, 5 + r0:5 + r0 + ts, cols] * cw_ref[0:1, cols].reshape(1, 1, cw_blk)
                for j in range(1, A_CONV):
                    y = y + xp_ref[bs, 5 + j + r0:5 + j + r0 + ts, cols] * cw_ref[j:j + 1, cols].reshape(1, 1, cw_blk)
                y = _silu(y).reshape(r, cw_blk)
                for h0 in range(0, cw_blk, A_DK):
                    c = y[:, h0:h0 + A_DK]
                    if c0 + h0 < 2 * A_QK:
                        inv = lax.rsqrt(jnp.sum(c * c, axis=-1, keepdims=True) + NORM_EPS)
                        if c0 + h0 < A_QK:
                            inv = inv * (A_DK ** -0.5)
                        c = c * inv
                    qkv_ref[bs, rs, c0 + h0:c0 + h0 + A_DK] = c.reshape(nbs, ts, A_DK)

            ab = jnp.dot(x, w_ref[:, E_AB:E_AB + LANES], preferred_element_type=F32)
            sp_in = ab + dt_bias
            softplus = jnp.maximum(sp_in, 0.0) + jnp.log1p(jnp.exp(-jnp.abs(sp_in)))
            lane = lax.broadcasted_iota(jnp.int32, ab.shape, 1)
            gb = jnp.where(lane < A_HEADS, neg_a * softplus, _sigmoid(ab))
            gb_ref[bs, rs, :] = gb.reshape(nbs, ts, LANES)

            ga = _silu(jnp.dot(x, w_ref[:, E_GA:E_GA + A_VW], preferred_element_type=F32))
            ga_ref[bs, rs, :] = ga.reshape(nbs, ts, A_VW).astype(ga_ref.dtype)
            gbt = _silu(jnp.dot(x, w_ref[:, E_GB:E_GB + B_QW], preferred_element_type=F32))
            gbt_ref[bs, rs, :] = gbt.reshape(nbs, ts, B_QW).astype(gbt_ref.dtype)

            cos_t = rot_ref[0, flat, :]
            s_up = rot_ref[1, flat, :]
            s_dn = rot_ref[2, flat, :]
            zq = jnp.dot(x, w_ref[:, E_QB:E_QB + B_QW], preferred_element_type=F32)
            cos4 = jnp.concatenate([rot_ref[3, flat, :]] * rep, axis=1)
            up4 = jnp.concatenate([rot_ref[4, flat, :]] * rep, axis=1)
            dn4 = jnp.concatenate([rot_ref[5, flat, :]] * rep, axis=1)
            qrot = zq * cos4 + pltpu.roll(zq, B_QW - half, axis=1) * up4 + pltpu.roll(zq, half, axis=1) * dn4
            qb_ref[bs, rs, :] = qrot.reshape(nbs, ts, B_QW)
            zk = jnp.dot(x, w_ref[:, E_KB:E_KB + B_KVW], preferred_element_type=F32)
            krot = zk * cos_t + pltpu.roll(zk, LANES - half, axis=1) * s_up + pltpu.roll(zk, half, axis=1) * s_dn
            kvb_ref[bs, rs, 0:B_KVW] = krot.reshape(nbs, ts, B_KVW)
            zv = jnp.dot(x, w_ref[:, E_VB:E_VB + B_KVW], preferred_element_type=F32)
            kvb_ref[bs, rs, B_KVW:] = zv.reshape(nbs, ts, B_KVW)

    last = xp_ref[:, tl + 5:tl + 8, :]
    conv_ref[...] = last
    if buf_ref is None:
        xp_ref[:, 5:8, :] = last


def _even_in_kernel_buf(x_ref, w_ref, cw_ref, misc_ref, rot_ref, buf_ref, *rest, nb, tl):
    _even_in_body(x_ref, w_ref, cw_ref, misc_ref, rot_ref, buf_ref, *rest, nb=nb, tl=tl)


def _even_in_kernel_nobuf(x_ref, w_ref, cw_ref, misc_ref, rot_ref, *rest, nb, tl):
    _even_in_body(x_ref, w_ref, cw_ref, misc_ref, rot_ref, None, *rest, nb=nb, tl=tl)


def _even_in(x, w, cw, misc, rot, buf, j, *, nb, tl, gate_dtype=F32):
    n, l, _ = x.shape
    grid = (n // nb, l // tl)
    r = nb * tl
    row_map = lambda i, t: (i, t, 0)
    rot_map = (lambda i, t: (0, t, 0)) if buf is None else (lambda i, t: (0, 0, 0))
    in_specs = [
        pl.BlockSpec((nb, tl, D_MODEL), row_map),
        _layer_spec(w, j),
        _const_spec((A_CONV, A_CONV_CH)),
        _const_spec((8, LANES)),
        pl.BlockSpec((6, r, LANES), rot_map),
    ]
    args = [x, w, cw, misc, rot]
    if buf is not None:
        in_specs.append(pl.BlockSpec((None, nb, A_CONV - 1, A_CONV_CH), lambda i, t: (j, i, 0, 0)))
        args.append(buf)
        body = functools.partial(_even_in_kernel_buf, nb=nb, tl=tl)
    else:
        body = functools.partial(_even_in_kernel_nobuf, nb=nb, tl=tl)
    widths = (A_CONV_CH, LANES, A_VW, B_QW, 2 * B_KVW, B_QW)
    dtypes = (F32, F32, gate_dtype, F32, F32, gate_dtype)
    out_shape = [jax.ShapeDtypeStruct((n, l, wd), dt) for wd, dt in zip(widths, dtypes)]
    out_specs = [pl.BlockSpec((nb, tl, wd), row_map) for wd in widths]
    out_shape.append(jax.ShapeDtypeStruct((n, A_CONV - 1, A_CONV_CH), F32))
    out_specs.append(pl.BlockSpec((nb, A_CONV - 1, A_CONV_CH), lambda i, t: (i, 0, 0)))
    return pl.pallas_call(
        body, grid=grid, in_specs=in_specs, out_specs=out_specs, out_shape=out_shape,
        scratch_shapes=[pltpu.VMEM((nb, tl + 8, A_CONV_CH), F32)],
        compiler_params=_params(2), name="even_in",
    )(*args)


def _tri_inv_all(a_list, c):
    ri = lax.broadcasted_iota(jnp.int32, (c, LANES), 0)
    ci = lax.broadcasted_iota(jnp.int32, (c, LANES), 1)
    eye = jnp.where(ri == ci, 1.0, 0.0)
    ps = [eye for _ in a_list]
    ms = [-a for a in a_list]
    span = 1
    while span < c:
        mps = [_mm(m[:, :c], jnp.concatenate([m, p], axis=1)) for m, p in zip(ms, ps)]
        ms = [mp[:, :LANES] for mp in mps]
        ps = [p + mp[:, LANES:] for p, mp in zip(ps, mps)]
        span *= 2
    return ps


def _delta_prep_all(qs, ks, vs, g_cols, g_rows, betas, g_lasts, c):
    ri = lax.broadcasted_iota(jnp.int32, (c, LANES), 0)
    ci = lax.broadcasted_iota(jnp.int32, (c, LANES), 1)
    causal = ri >= ci
    strict = ri > ci
    decays = [jnp.where(causal, jnp.exp(jnp.where(causal, gc - gr, 0.0)), 0.0) for gc, gr in zip(g_cols, g_rows)]
    kbs = [k * b for k, b in zip(ks, betas)]
    if c < LANES:
        pad = jnp.zeros((LANES - c, A_DK), F32)
        k_rows = [jnp.concatenate([k, pad], axis=0).astype(BF16) for k in ks]
    else:
        k_rows = [k.astype(BF16) for k in ks]
    a_list = [jnp.where(strict, _mm_nt(kb, kr) * d, 0.0) for kb, kr, d in zip(kbs, k_rows, decays)]
    qks = [(_mm_nt(q, kr) * d)[:, :c].astype(BF16) for q, kr, d in zip(qs, k_rows, decays)]
    t_invs = _tri_inv_all(a_list, c)
    e_gs = [jnp.exp(gc) for gc in g_cols]
    uws = [_mm(t[:, :c], jnp.concatenate([v * b, kb * e], axis=1))
           for t, v, b, kb, e in zip(t_invs, vs, betas, kbs, e_gs)]
    us = [uw[:, :A_DK] for uw in uws]
    wqs = [jnp.concatenate([uw[:, A_DK:], q * e], axis=0).astype(BF16) for uw, q, e in zip(uws, qs, e_gs)]
    k_decs = [(k * jnp.exp(gl - gc)).astype(BF16) for k, gl, gc in zip(ks, g_lasts, g_cols)]
    e_lasts = [jnp.exp(gl) for gl in g_lasts]
    return us, wqs, qks, k_decs, e_lasts


def _delta_body(qkv_ref, gb_ref, s0_ref, o_ref, sout_ref, s_ref, *, nb, tl, c, slot):
    t = pl.program_id(1)
    nt = pl.num_programs(1)

    @pl.when(t == 0)
    def _():
        if s0_ref is None:
            s_ref[...] = jnp.zeros(s_ref.shape, F32)
        else:
            s_ref[...] = s0_ref[...]

    nc = tl // c
    qs, ks, vs, g_cols, g_rows, betas, g_lasts = [], [], [], [], [], [], []
    for ic in range(nc):
        rows = slice(ic * c, (ic + 1) * c)
        for b in range(nb):
            gbc = gb_ref[b, rows, :]
            g_all = _cumsum_rows(gbc)
            if c < LANES:
                g_t = jnp.concatenate([g_all, jnp.zeros((LANES - c, LANES), F32)], axis=0).T
            else:
                g_t = g_all.T
            for h in range(A_HEADS):
                qs.append(qkv_ref[b, rows, h * A_DK:(h + 1) * A_DK])
                ks.append(qkv_ref[b, rows, A_QK + h * A_DK:A_QK + (h + 1) * A_DK])
                vs.append(qkv_ref[b, rows, 2 * A_QK + h * A_DK:2 * A_QK + (h + 1) * A_DK])
                g_cols.append(g_all[:, h:h + 1])
                g_rows.append(g_t[h:h + 1, :])
                betas.append(gbc[:, A_HEADS + h:A_HEADS + h + 1])
                g_lasts.append(g_all[c - 1:c, h:h + 1])
    us, wqs, qks, k_decs, e_lasts = _delta_prep_all(qs, ks, vs, g_cols, g_rows, betas, g_lasts, c)

    nch = nb * A_HEADS
    states = [s_ref[b, h] for b in range(nb) for h in range(A_HEADS)]
    for ic in range(nc):
        sl = slice(ic * nch, (ic + 1) * nch)
        ws_qs = [_mm(wq, s) for wq, s in zip(wqs[sl], states)]
        v_news = [(u - x[:c]).astype(BF16) for u, x in zip(us[sl], ws_qs)]
        outs = [x[c:] + _mm(qk, vn) for x, qk, vn in zip(ws_qs, qks[sl], v_news)]
        states = [s * el + _mm_tn(kd, vn) for s, el, kd, vn in zip(states, e_lasts[sl], k_decs[sl], v_news)]
        for i, o in enumerate(outs):
            b, h = divmod(i, A_HEADS)
            o_ref[b, ic * c:(ic + 1) * c, h * A_DK:(h + 1) * A_DK] = o.astype(o_ref.dtype)
    for i, s in enumerate(states):
        b, h = divmod(i, A_HEADS)
        s_ref[b, h] = s

    @pl.when(t == nt - 1)
    def _():
        _slot_view(sout_ref, slot)[...] = s_ref[...]


def _delta_kernel_s0(qkv_ref, gb_ref, s0_ref, o_ref, sout_ref, s_ref, **kw):
    _delta_body(qkv_ref, gb_ref, s0_ref, o_ref, sout_ref, s_ref, **kw)


def _delta_kernel_zero(qkv_ref, gb_ref, o_ref, sout_ref, s_ref, **kw):
    _delta_body(qkv_ref, gb_ref, None, o_ref, sout_ref, s_ref, **kw)


def _delta(qkv, gb, s0, j, stacked, n_layers, *, nb, tl, c, out_dtype=F32):
    n, l, _ = qkv.shape
    grid = (n // nb, l // tl)
    row_map = lambda i, t: (i, t, 0)
    st_block = (nb, A_HEADS, A_DK, A_DK)
    st_map = lambda i, t: (i, 0, 0, 0)
    first = stacked is None
    kw = dict(nb=nb, tl=tl, c=c, slot=(j, n_layers) if first else None)
    in_specs = [pl.BlockSpec((nb, tl, A_CONV_CH), row_map), pl.BlockSpec((nb, tl, LANES), row_map)]
    args = [qkv, gb]
    if s0 is not None:
        in_specs.append(_stacked_spec(st_block, st_map, j, n_layers, False))
        args.append(s0)
        body = functools.partial(_delta_kernel_s0, **kw)
    else:
        body = functools.partial(_delta_kernel_zero, **kw)
    body, in_specs, args, aliases = _alias_stacked(body, in_specs, args, stacked, 1)
    return pl.pallas_call(
        body, grid=grid, in_specs=in_specs,
        out_specs=[pl.BlockSpec((nb, tl, A_VW), row_map), _stacked_spec(st_block, st_map, j, n_layers, first)],
        out_shape=[jax.ShapeDtypeStruct((n, l, A_VW), out_dtype),
                   jax.ShapeDtypeStruct((n_layers, n, A_HEADS, A_DK, A_DK), F32)],
        scratch_shapes=[pltpu.VMEM(st_block, F32)], input_output_aliases=aliases,
        compiler_params=_params(2), name="delta",
    )(*args)


def _swa_tasks(qs, kxs, vxs, first_min_key, sink_ref):
    nq = qs[0].shape[0]
    nk = 2 * WINDOW
    rows = B_HEADS * nq
    lo = lax.broadcasted_iota(jnp.int32, (nq, LANES), 1) < B_HD
    ri = lax.broadcasted_iota(jnp.int32, (rows, nk), 0) % nq
    ci = lax.broadcasted_iota(jnp.int32, (rows, nk), 1)
    band = (ci >= ri) & (ci <= ri + WINDOW)
    masks = [band & (ci >= first_min_key)] + [band] * (len(qs) - 1)
    head = lax.broadcasted_iota(jnp.int32, (rows, 1), 0) // nq
    sink = jnp.zeros((rows, 1), F32)
    for h in range(B_HEADS):
        sink = jnp.where(head == h, sink_ref[h], sink)

    def stack_heads(q):
        parts = []
        for h in range(B_HEADS):
            j, e = divmod(h, 2)
            g = h // (B_HEADS // B_KV_HEADS)
            src = q[:, j * LANES:(j + 1) * LANES]
            if e != g:
                src = pltpu.roll(src, B_HD, axis=1)
            parts.append(jnp.where(lo, src, 0.0) if g == 0 else jnp.where(lo, 0.0, src))
        return jnp.concatenate(parts, axis=0).astype(BF16)

    lhs = [stack_heads(q) for q in qs]
    ss = [jnp.where(mask, _mm_nt(x, kx), MASK_NEG) for mask, x, kx in zip(masks, lhs, kxs)]
    ms = [jnp.maximum(jnp.max(s, axis=-1, keepdims=True), sink) for s in ss]
    ps = [jnp.where(mask, jnp.exp(s - m), 0.0) for mask, s, m in zip(masks, ss, ms)]
    dens = [jnp.sum(p, axis=-1, keepdims=True) + jnp.exp(sink - m) for p, m in zip(ps, ms)]
    os_ = [_mm(p, vx) / d for p, d, vx in zip(ps, dens, vxs)]
    outs = []
    for o in os_:
        cols = []
        for j in range(B_HEADS // 2):
            a = o[(2 * j) * nq:(2 * j + 1) * nq]
            b = o[(2 * j + 1) * nq:(2 * j + 2) * nq]
            if j < B_HEADS // 4:
                cols.append(jnp.where(lo, a, pltpu.roll(b, B_HD, axis=1)))
            else:
                cols.append(jnp.where(lo, pltpu.roll(a, B_HD, axis=1), b))
        outs.append(jnp.concatenate(cols, axis=1))
    return outs


def _swa_prompt_kernel(sink_ref, q_ref, kvc_ref, kvp_ref, o_ref, *, nblk):
    i = pl.program_id(1)
    kv = [kvp_ref[0].astype(BF16)] + [kvc_ref[0, t * WINDOW:(t + 1) * WINDOW, :].astype(BF16) for t in range(nblk)]
    qs = [q_ref[0, t * WINDOW:(t + 1) * WINDOW, :] for t in range(nblk)]
    kxs = [jnp.concatenate([kv[t][:, :B_KVW], kv[t + 1][:, :B_KVW]], axis=0) for t in range(nblk)]
    vxs = [jnp.concatenate([kv[t][:, B_KVW:], kv[t + 1][:, B_KVW:]], axis=0) for t in range(nblk)]
    outs = _swa_tasks(qs, kxs, vxs, jnp.where(i > 0, 0, WINDOW), sink_ref)
    for t, o in enumerate(outs):
        o_ref[0, t * WINDOW:(t + 1) * WINDOW, :] = o.astype(o_ref.dtype)


def _swa_prompt(qb, kvb, sinks, *, nblk, out_dtype=F32):
    n, l, _ = qb.shape
    tq = nblk * WINDOW
    return pl.pallas_call(
        functools.partial(_swa_prompt_kernel, nblk=nblk), grid=(n, l // tq),
        in_specs=[pl.BlockSpec(memory_space=pltpu.SMEM),
                  pl.BlockSpec((1, tq, B_QW), lambda b, i: (b, i, 0)),
                  pl.BlockSpec((1, tq, 2 * B_KVW), lambda b, i: (b, i, 0)),
                  pl.BlockSpec((1, WINDOW, 2 * B_KVW), lambda b, i: (b, jnp.maximum(i * nblk - 1, 0), 0))],
        out_specs=pl.BlockSpec((1, tq, B_QW), lambda b, i: (b, i, 0)),
        out_shape=jax.ShapeDtypeStruct((n, l, B_QW), out_dtype),
        compiler_params=_params(2), name="swa_prompt",
    )(sinks, qb, kvb, kvb)


def _swa_sample_kernel(sink_ref, q_ref, kv_ref, ck_ref, cv_ref, o_ref, nk_ref, nv_ref, *, nb, ls, slot):
    nk_ref = _slot_view(nk_ref, slot)
    nv_ref = _slot_view(nv_ref, slot)
    pad = jnp.zeros((WINDOW - ls, B_KVW), F32)
    qs, kxs, vxs = [], [], []
    for b in range(nb):
        kv = kv_ref[b]
        k_new = kv[:, :B_KVW]
        v_new = kv[:, B_KVW:]
        ck = ck_ref[b]
        cv = cv_ref[b]
        qs.append(q_ref[b])
        kxs.append(jnp.concatenate([ck, k_new, pad], axis=0).astype(BF16))
        vxs.append(jnp.concatenate([cv, v_new, pad], axis=0).astype(BF16))
        nk_ref[b, 0:WINDOW - ls, :] = ck[ls:, :]
        nk_ref[b, WINDOW - ls:, :] = k_new
        nv_ref[b, 0:WINDOW - ls, :] = cv[ls:, :]
        nv_ref[b, WINDOW - ls:, :] = v_new
    for b, o in enumerate(_swa_tasks(qs, kxs, vxs, 0, sink_ref)):
        o_ref[b] = o


def _swa_sample(qb, kvb, ck, cv, sinks, j, stacked, *, nb):
    n, ls, _ = qb.shape
    row_map = lambda i: (i, 0, 0)
    n_layers = ck.shape[0]
    first = stacked is None
    cache_block = (nb, WINDOW, B_KVW)
    cache_spec = _stacked_spec(cache_block, row_map, j, n_layers, False)
    cache_out = _stacked_spec(cache_block, row_map, j, n_layers, first)
    cache_shape = jax.ShapeDtypeStruct(ck.shape, F32)
    body = functools.partial(_swa_sample_kernel, nb=nb, ls=ls, slot=(j, n_layers) if first else None)
    in_specs = [pl.BlockSpec(memory_space=pltpu.SMEM),
                pl.BlockSpec((nb, ls, B_QW), row_map),
                pl.BlockSpec((nb, ls, 2 * B_KVW), row_map),
                cache_spec, cache_spec]
    args = [sinks, qb, kvb, ck, cv]
    body, in_specs, args, aliases = _alias_stacked(body, in_specs, args, stacked, 1, 2)
    return pl.pallas_call(
        body, grid=(n // nb,), in_specs=in_specs,
        out_specs=[pl.BlockSpec((nb, ls, B_QW), row_map), cache_out, cache_out],
        out_shape=[jax.ShapeDtypeStruct((n, ls, B_QW), F32), cache_shape, cache_shape],
        input_output_aliases=aliases, compiler_params=_params(1), name="swa_sample",
    )(*args)


def _lower_bound(lbraw_ref, j):
    raw = lbraw_ref[...]
    e = jnp.exp(raw - jnp.max(raw, axis=0, keepdims=True))
    sm = e / jnp.sum(e, axis=0, keepdims=True)
    lb = jnp.zeros((1, D_MODEL), F32)
    for i in range(1, j + 1):
        lb = lb + sm[i:i + 1, :]
    return lb


def _odd_proj(x, w_ref, lb):
    d = D_MODEL

    def proj(c0):
        return jnp.dot(x, w_ref[:, c0:c0 + d], preferred_element_type=F32)

    q = _silu(proj(0)) * (C_DK ** -0.5)
    zf = proj(d)
    one_m_lb = 1.0 - lb
    u = one_m_lb * _sigmoid(zf)
    lf = jnp.log(jnp.maximum(lb + u, F_FLOOR))
    k = one_m_lb - u
    v = proj(2 * d)
    g = _silu(proj(3 * d))
    return q, k, v, lf, g


def _odd_in_kernel(x_ref, w_ref, lbraw_ref, q_ref, k_ref, v_ref, lf_ref, g_ref, *, j, ts):
    lb = _lower_bound(lbraw_ref, j)
    for s in range(x_ref.shape[0] // ts):
        rows = slice(s * ts, (s + 1) * ts)
        q, k, v, lf, g = _odd_proj(x_ref[rows, :].astype(BF16), w_ref, lb)
        q_ref[rows, :] = q
        k_ref[rows, :] = k
        v_ref[rows, :] = v
        lf_ref[rows, :] = lf
        g_ref[rows, :] = g


def _odd_in(x2, w, lb_raw, j, *, tr, ts=SUB_ROWS):
    rows = x2.shape[0]
    n_odd = lb_raw.shape[0]
    row_spec = pl.BlockSpec((tr, D_MODEL), lambda i: (i, 0))
    shp = jax.ShapeDtypeStruct((rows, D_MODEL), F32)
    return pl.pallas_call(
        functools.partial(_odd_in_kernel, j=j, ts=min(ts, tr)), grid=(rows // tr,),
        in_specs=[row_spec, _layer_spec(w, j), _const_spec((n_odd, D_MODEL))],
        out_specs=[row_spec] * 5, out_shape=[shp] * 5,
        compiler_params=_params(1), name="odd_in",
    )(x2, w, lb_raw)


def _hgrn_prep_all(qs, ks, vs, lfs, c, state_t):
    ng = c // SUBLANES
    shape3 = (ng, SUBLANES, C_DK)
    sub = lax.broadcasted_iota(jnp.int32, shape3, 1)

    def grouped(x):
        return x.reshape(shape3)

    def flat(x):
        return x.reshape(c, C_DK)

    gs = []
    for lf in lfs:
        x = grouped(lf * LOG2_E)
        s = 1
        while s < SUBLANES:
            x = x + jnp.where(sub >= s, pltpu.roll(x, s, axis=1), 0.0)
            s *= 2
        if ng > 1:
            tot = x[:, SUBLANES - 1:SUBLANES, :]
            run = jnp.zeros((1, 1, C_DK), F32)
            offs = []
            for i in range(ng):
                offs.append(run)
                run = run + tot[i:i + 1]
            x = x + jnp.concatenate(offs, axis=0)
        gs.append(x)
    rbs = [jnp.broadcast_to(g[:, 0:1, :], shape3) for g in gs]

    def block_start(rb, m):
        x = rb.reshape(ng // m, m, SUBLANES, C_DK)[:, 0:1]
        return jnp.broadcast_to(x, (ng // m, m, SUBLANES, C_DK)).reshape(shape3)

    ri = lax.broadcasted_iota(jnp.int32, (c, c), 0)
    ci = lax.broadcasted_iota(jnp.int32, (c, c), 1)
    q3s = [grouped(q) for q in qs]
    k3s = [grouped(k) for k in ks]
    accs = None
    bs = list(gs)
    h = 1
    while h < c:
        if h < SUBLANES:
            bns = [pltpu.roll(b, SUBLANES - h, axis=1) for b in bs]
        else:
            m = h // SUBLANES
            bs = [block_start(rb, m) for rb in rbs]
            bns = [jnp.concatenate([b[m:], g[ng - m:]], axis=0) for b, g in zip(bs, gs)]
        qhs = q3s if h == 1 else [q * jnp.exp2(g - b) for q, g, b in zip(q3s, gs, bs)]
        khs = [k * jnp.exp2(bn - g) for k, bn, g in zip(k3s, bns, gs)]
        lvl = ((ri // h) == (ci // h) + 1) & ((ri // (2 * h)) == (ci // (2 * h)))
        if h == 1 and ng > 1:
            gap = jnp.zeros((LANES - c, C_DK), F32)
            both = [_mm_nt(flat(q), jnp.concatenate([flat(k), gap, flat(kh)], axis=0))
                    for q, k, kh in zip(q3s, k3s, khs)]
            accs = [jnp.where(ri == ci, x[:, :c], jnp.where(lvl, x[:, LANES:], 0.0)) for x in both]
        elif h == 1:
            accs = [jnp.where(ri == ci, _mm_nt(flat(q), flat(k)), jnp.where(lvl, _mm_nt(flat(q), flat(kh)), 0.0))
                    for q, k, kh in zip(q3s, k3s, khs)]
        else:
            accs = [jnp.where(lvl, _mm_nt(flat(qh), flat(kh)), a) for a, qh, kh in zip(accs, qhs, khs)]
        if 2 * h < SUBLANES:
            sel = (sub & h) != 0
            bs = [jnp.where(sel, pltpu.roll(b, h, axis=1), b) for b in bs]
        h *= 2
    if accs is None:
        accs = [jnp.where(ri == ci, _mm_nt(q, k), 0.0) for q, k in zip(qs, ks)]
    g_lasts = [g[ng - 1, SUBLANES - 1:SUBLANES, :] for g in gs]
    intra = [_mm(a, v) for a, v in zip(accs, vs)]
    q_decs = [flat(q * jnp.exp2(g)).astype(BF16) for q, g in zip(q3s, gs)]
    k_decs = [flat(k * jnp.exp2(gl - g)) for k, gl, g in zip(k3s, g_lasts, gs)]
    e_lasts = [jnp.exp2(gl) for gl in g_lasts]
    if state_t:
        kvs = [_mm_tn(v, kd) for v, kd in zip(vs, k_decs)]
    else:
        kvs = [_mm_tn(kd, v) for v, kd in zip(vs, k_decs)]
        e_lasts = [jnp.broadcast_to(el, (C_DK, C_DK)).T for el in e_lasts]
    return intra, q_decs, kvs, e_lasts


def _hgrn_steps(qs, ks, vs, lfs, states, c, state_t):
    intra, q_decs, kvs, e_lasts = _hgrn_prep_all(qs, ks, vs, lfs, c, state_t)
    nch = len(states)
    read_state = _mm_nt if state_t else _mm
    outs = []
    for ic in range(len(qs) // nch):
        sl = slice(ic * nch, (ic + 1) * nch)
        outs += [read_state(qd, st) + x for qd, st, x in zip(q_decs[sl], states, intra[sl])]
        states = [st * el + kv for st, el, kv in zip(states, e_lasts[sl], kvs[sl])]
    return outs, states


def _hgrn_body(q_ref, k_ref, v_ref, lf_ref, s0_ref, o_ref, sout_ref, st_ref, *, nb, hb, tl, c, slot):
    t = pl.program_id(2)
    nt = pl.num_programs(2)
    nc = tl // c
    state_t = s0_ref is None

    @pl.when(t == 0)
    def _():
        if s0_ref is None:
            st_ref[...] = jnp.zeros(st_ref.shape, F32)
        else:
            st_ref[...] = s0_ref[...]

    chains = [(b, h) for b in range(nb) for h in range(hb)]
    where = [(ic, b, h) for ic in range(nc) for b, h in chains]

    def load(ref):
        return [ref[b, ic * c:(ic + 1) * c, h * C_DK:(h + 1) * C_DK] for ic, b, h in where]

    states = [st_ref[b, h] for b, h in chains]
    outs, states = _hgrn_steps(load(q_ref), load(k_ref), load(v_ref), load(lf_ref), states, c, state_t)
    for (ic, b, h), o in zip(where, outs):
        o_ref[b, ic * c:(ic + 1) * c, h * C_DK:(h + 1) * C_DK] = o
    for (b, h), st in zip(chains, states):
        st_ref[b, h] = st

    @pl.when(t == nt - 1)
    def _():
        out = _slot_view(sout_ref, slot)
        for b in range(nb):
            for h in range(hb):
                out[b, h] = st_ref[b, h].T if state_t else st_ref[b, h]


def _hgrn_kernel_s0(q_ref, k_ref, v_ref, lf_ref, s0_ref, o_ref, sout_ref, st_ref, **kw):
    _hgrn_body(q_ref, k_ref, v_ref, lf_ref, s0_ref, o_ref, sout_ref, st_ref, **kw)


def _hgrn_kernel_zero(q_ref, k_ref, v_ref, lf_ref, o_ref, sout_ref, st_ref, **kw):
    _hgrn_body(q_ref, k_ref, v_ref, lf_ref, None, o_ref, sout_ref, st_ref, **kw)


def _hgrn(q, k, v, lf, s0, j, stacked, n_layers, *, nb, hb, tl, c):
    n, l, _ = q.shape
    grid = (n // nb, C_HEADS // hb, l // tl)
    row_spec = pl.BlockSpec((nb, tl, hb * C_DK), lambda i, g, t: (i, t, g))
    st_block = (nb, hb, C_DK, C_DK)
    st_map = lambda i, g, t: (i, g, 0, 0)
    first = stacked is None
    st_spec = _stacked_spec(st_block, st_map, j, n_layers, first)
    in_specs = [row_spec] * 4
    args = [q, k, v, lf]
    kw = dict(nb=nb, hb=hb, tl=tl, c=c, slot=(j, n_layers) if first else None)
    if s0 is not None:
        in_specs = in_specs + [_stacked_spec(st_block, st_map, j, n_layers, False)]
        args.append(s0)
        body = functools.partial(_hgrn_kernel_s0, **kw)
    else:
        body = functools.partial(_hgrn_kernel_zero, **kw)
    body, in_specs, args, aliases = _alias_stacked(body, in_specs, args, stacked, 1)
    return pl.pallas_call(
        body, grid=grid, in_specs=in_specs, out_specs=[row_spec, st_spec],
        out_shape=[jax.ShapeDtypeStruct((n, l, D_MODEL), F32),
                   jax.ShapeDtypeStruct((n_layers, n, C_HEADS, C_DK, C_DK), F32)],
        scratch_shapes=[pltpu.VMEM(st_block, F32)], input_output_aliases=aliases,
        compiler_params=_params(3), name="hgrn",
    )(*args)


def _odd_fused_kernel(x_ref, w_ref, lbraw_ref, o_ref, g_ref, sout_ref, st_ref, *, j, tl, c, ts, slot):
    t = pl.program_id(1)
    nt = pl.num_programs(1)

    @pl.when(t == 0)
    def _():
        st_ref[...] = jnp.zeros(st_ref.shape, F32)

    lb = _lower_bound(lbraw_ref, j)
    heads = range(C_HEADS)

    def project(s):
        rows = slice(s * ts, (s + 1) * ts)
        q, k, v, lf, g = _odd_proj(x_ref[0, rows, :].astype(BF16), w_ref, lb)
        g_ref[0, rows, :] = g.astype(g_ref.dtype)
        return q, k, v, lf

    def mix(s, qkvl, states):
        where = [(ic, h) for ic in range(ts // c) for h in heads]
        qs, ks, vs, lfs = ([a[ic * c:(ic + 1) * c, h * C_DK:(h + 1) * C_DK] for ic, h in where] for a in qkvl)
        outs, states = _hgrn_steps(qs, ks, vs, lfs, states, c, True)
        for (ic, h), o in zip(where, outs):
            o_ref[0, s * ts + ic * c:s * ts + (ic + 1) * c, h * C_DK:(h + 1) * C_DK] = o.astype(o_ref.dtype)
        return states

    states = [st_ref[h] for h in heads]
    n_sub = tl // ts
    pending = project(0)
    for s in range(n_sub):
        following = project(s + 1) if s + 1 < n_sub else None
        states = mix(s, pending, states)
        pending = following
    for h in heads:
        st_ref[h] = states[h]

    @pl.when(t == nt - 1)
    def _():
        out = _slot_view(sout_ref, slot)
        for h in heads:
            out[0, h] = st_ref[h].T


def _odd_fused(x, w, lb_raw, j, stacked, n_layers, *, tl, c, ts=SUB_ROWS, out_dtype=F32):
    n, l, _ = x.shape
    row_spec = pl.BlockSpec((1, tl, D_MODEL), lambda i, t: (i, t, 0))
    st_block = (1, C_HEADS, C_DK, C_DK)
    first = stacked is None
    body = functools.partial(_odd_fused_kernel, j=j, tl=tl, c=c, ts=ts, slot=(j, n_layers) if first else None)
    in_specs = [row_spec, _layer_spec(w, j), _const_spec(lb_raw.shape)]
    body, in_specs, args, aliases = _alias_stacked(body, in_specs, [x, w, lb_raw], stacked, 2)
    shp = jax.ShapeDtypeStruct((n, l, D_MODEL), out_dtype)
    return pl.pallas_call(
        body, grid=(n, l // tl), in_specs=in_specs,
        out_specs=[row_spec, row_spec, _stacked_spec(st_block, lambda i, t: (i, 0, 0, 0), j, n_layers, first)],
        out_shape=[shp, shp, jax.ShapeDtypeStruct((n_layers, n, C_HEADS, C_DK, C_DK), F32)],
        scratch_shapes=[pltpu.VMEM((C_HEADS, C_DK, C_DK), F32)], input_output_aliases=aliases,
        compiler_params=_params(2), name="odd_fused",
    )(*args)


def _group_rms(o, g_row, heads):
    outs = []
    for h in range(heads):
        c = o[:, h * LANES:(h + 1) * LANES]
        inv = lax.rsqrt(jnp.mean(c * c, axis=-1, keepdims=True) + NORM_EPS)
        outs.append(c * inv * g_row)
    return jnp.concatenate(outs, axis=1)


def _post_tail(x_ref, ys, p_ref, lng_ref, lnb_ref, wproj_ref, wgate_ref, out_ref, tiles):
    hns = []
    for rows, y in zip(tiles, ys):
        pre = DEEPNORM_ALPHA * x_ref[rows, :] + y
        mu = jnp.mean(pre, axis=-1, keepdims=True)
        xc = pre - mu
        var = jnp.mean(xc * xc, axis=-1, keepdims=True)
        hns.append(xc * lax.rsqrt(var + NORM_EPS) * lng_ref[...] + lnb_ref[...])
    embs = [jnp.dot(p_ref[rows, :].astype(BF16), wproj_ref[:, 0:D_MODEL], preferred_element_type=F32)
            for rows in tiles]
    gates = [_sigmoid(jnp.dot(hn.astype(BF16), wgate_ref[:, 0:D_MODEL], preferred_element_type=F32)) for hn in hns]
    for rows, hn, gate, emb in zip(tiles, hns, gates, embs):
        out_ref[rows, :] = hn + gate * emb


def _sub_tiles(n_rows):
    ts = min(n_rows, POST_SUB_ROWS)
    return [slice(r0, r0 + ts) for r0 in range(0, n_rows, ts)]


def _post_even_kernel(x_ref, oa_ref, ga_ref, ob_ref, gbt_ref, p_ref, wout_ref, ng_ref, lng_ref, lnb_ref,
                      wproj_ref, wgate_ref, out_ref):
    tiles = _sub_tiles(x_ref.shape[0])
    a_s = [(_group_rms(oa_ref[rows, :].astype(F32), ng_ref[...], A_HEADS) * ga_ref[rows, :].astype(F32)).astype(BF16)
           for rows in tiles]
    b_s = [(ob_ref[rows, :].astype(F32) * gbt_ref[rows, :].astype(F32)).astype(BF16) for rows in tiles]
    ys = [jnp.dot(a, wout_ref[0:A_VW, 0:D_MODEL], preferred_element_type=F32)
          + jnp.dot(b, wout_ref[A_VW:, 0:D_MODEL], preferred_element_type=F32) for a, b in zip(a_s, b_s)]
    _post_tail(x_ref, ys, p_ref, lng_ref, lnb_ref, wproj_ref, wgate_ref, out_ref, tiles)


def _post_odd_kernel(x_ref, o_ref, g_ref, p_ref, wout_ref, ng_ref, lng_ref, lnb_ref,
                     wproj_ref, wgate_ref, out_ref):
    tiles = _sub_tiles(x_ref.shape[0])
    a_s = [(_group_rms(o_ref[rows, :].astype(F32), ng_ref[...], C_HEADS) * g_ref[rows, :].astype(F32)).astype(BF16)
           for rows in tiles]
    ys = [jnp.dot(a, wout_ref[:, 0:D_MODEL], preferred_element_type=F32) for a in a_s]
    _post_tail(x_ref, ys, p_ref, lng_ref, lnb_ref, wproj_ref, wgate_ref, out_ref, tiles)


def _post(x2, acts, p3, layer, wout, j, ng, lng, lnb, wproj, wgate, *, even, tr):
    rows = x2.shape[0]
    full = pl.BlockSpec((tr, D_MODEL), lambda i: (i, 0))
    half = pl.BlockSpec((tr, A_VW), lambda i: (i, 0))
    act_specs = [half] * 4 if even else [full] * 2
    in_specs = ([full] + act_specs + [pl.BlockSpec((None, tr, PLE_DIM), lambda i: (layer, i, 0)),
                                      _layer_spec(wout, j), _const_spec((1, LANES)),
                                      _const_spec((1, D_MODEL)), _const_spec((1, D_MODEL)),
                                      _layer_spec(wproj, layer), _layer_spec(wgate, layer)])
    return pl.pallas_call(
        _post_even_kernel if even else _post_odd_kernel, grid=(rows // tr,),
        in_specs=in_specs, out_specs=full, out_shape=jax.ShapeDtypeStruct((rows, D_MODEL), F32),
        compiler_params=_params(1), name="post_even" if even else "post_odd",
    )(x2, *acts, p3, wout, ng, lng, lnb, wproj, wgate)


def _bf16_padded(w):
    return jnp.pad(w.astype(BF16), ((0, 0), (0, 0), (0, LANES)))


def _pack_even_w(w):
    w = w.astype(BF16)
    qkv = w[..., 0:1536]
    ga = w[..., 1536:2048]
    ab = w[..., 2048:2056]
    qb = w[..., 2056:2568]
    kb = w[..., 2568:2696]
    vb = w[..., 2696:2824]
    gbt = w[..., 2824:3336]
    ab = jnp.pad(ab, ((0, 0), (0, 0), (0, LANES - 2 * A_HEADS)))
    return jnp.concatenate([qkv, ga, qb, kb, vb, gbt, ab], axis=-1)


def _rot_tables(pos, reps):
    inv = ROPE_THETA ** (-jnp.arange(0, ROT_DIM, 2, dtype=F32) / ROT_DIM)
    ang = pos.astype(F32)[:, None] * inv[None, :]
    cos = jnp.cos(ang)
    sin = jnp.sin(ang)
    n = pos.shape[0]
    half = ROT_DIM // 2
    cos64 = jnp.concatenate([cos, cos, jnp.ones((n, B_HD - ROT_DIM), F32)], axis=1)
    up64 = jnp.concatenate([-sin, jnp.zeros((n, B_HD - half), F32)], axis=1)
    dn64 = jnp.concatenate([jnp.zeros((n, half), F32), sin, jnp.zeros((n, B_HD - ROT_DIM), F32)], axis=1)
    tabs = [jnp.tile(t, (reps, LANES // B_HD)) for t in (cos64, up64, dn64)]
    return jnp.stack(tabs + [t * (B_HD ** -0.5) for t in tabs])


def kernel(x_prompt, x_sample, state_conv_a, state_delta_a, cache_win_k, cache_win_v, state_hgrn_c,
           p_prompt, p_sample, w_in_even, conv_w_a, a_log, dt_bias, norm_a, sinks_b, w_out_even,
           w_in_odd, lb_raw, norm_c, w_out_odd, ln_g, ln_b, w_ple_proj, w_ple_gate):
    n_p, l_p, _ = x_prompt.shape
    n_s, l_s, _ = x_sample.shape
    nb_s = 32
    rot_p = _rot_tables(jnp.arange(l_p), 1)
    rot_s = _rot_tables(PAST_LEN + jnp.arange(l_s), nb_s)
    n_even, n_odd = w_in_even.shape[0], w_in_odd.shape[0]
    conv_p, conv_s, wk_p, wv_p = [], [], [], []
    delta_p = delta_s = hg_p = hg_s = win_s = None
    hp, hs = x_prompt, x_sample
    pp3 = p_prompt.reshape(DEPTH, n_p * l_p, PLE_DIM)
    ps3 = p_sample.reshape(DEPTH, n_s * l_s, PLE_DIM)
    ck_all = cache_win_k.reshape(-1, n_s, WINDOW, B_KVW)
    cv_all = cache_win_v.reshape(-1, n_s, WINDOW, B_KVW)
    w_even = _pack_even_w(w_in_even)
    w_odd = _bf16_padded(w_in_odd)
    wout_even = _bf16_padded(w_out_even)
    wout_odd = _bf16_padded(w_out_odd)
    wproj = _bf16_padded(w_ple_proj)
    wgate = _bf16_padded(w_ple_gate)
    for layer in range(DEPTH):
        j = layer // 2
        lng = ln_g[layer].reshape(1, D_MODEL)
        lnb = ln_b[layer].reshape(1, D_MODEL)
        hp2 = hp.reshape(n_p * l_p, D_MODEL)
        hs2 = hs.reshape(n_s * l_s, D_MODEL)
        if layer % 2 == 0:
            w = w_even
            misc = jnp.zeros((8, LANES), F32)
            misc = misc.at[0, :A_HEADS].set(a_log[j]).at[1, :A_HEADS].set(dt_bias[j])
            cw = conv_w_a[j]
            sinks = sinks_b[j]
            wout = wout_even
            ng = norm_a[j].reshape(1, LANES)
            qkv, gb, ga, qb, kvb, gbt, c1 = _even_in(hp, w, cw, misc, rot_p, None, j, nb=1, tl=512,
                                                     gate_dtype=ACT_DTYPE)
            oa, delta_p = _delta(qkv, gb, None, j, delta_p, n_even, nb=1, tl=512, c=A_CHUNK, out_dtype=ACT_DTYPE)
            ob = _swa_prompt(qb, kvb, sinks, nblk=8, out_dtype=ACT_DTYPE)
            acts = [t.reshape(n_p * l_p, A_VW) for t in (oa, ga, ob, gbt)]
            hp = _post(hp2, acts, pp3, layer, wout, j, ng, lng, lnb, wproj, wgate, even=True, tr=512)
            hp = hp.reshape(n_p, l_p, D_MODEL)
            k1 = kvb[:, l_p - WINDOW:, :B_KVW].reshape(n_p, WINDOW, B_KV_HEADS, B_HD)
            v1 = kvb[:, l_p - WINDOW:, B_KVW:].reshape(n_p, WINDOW, B_KV_HEADS, B_HD)
            qkv, gb, ga, qb, kvb, gbt, c2 = _even_in(hs, w, cw, misc, rot_s, state_conv_a, j, nb=nb_s, tl=l_s)
            oa, delta_s = _delta(qkv, gb, state_delta_a, j, delta_s, n_even, nb=8, tl=l_s, c=l_s)
            ob, *win_s = _swa_sample(qb, kvb, ck_all, cv_all, sinks, j, win_s, nb=8)
            acts = [t.reshape(n_s * l_s, A_VW) for t in (oa, ga, ob, gbt)]
            hs = _post(hs2, acts, ps3, layer, wout, j, ng, lng, lnb, wproj, wgate, even=True, tr=512)
            hs = hs.reshape(n_s, l_s, D_MODEL)
            conv_p.append(c1); conv_s.append(c2)
            wk_p.append(k1); wv_p.append(v1)
        else:
            w = w_odd
            wout = wout_odd
            ng = norm_c[j].reshape(1, LANES)
            o, g, hg_p = _odd_fused(hp, w, lb_raw, j, hg_p, n_odd, tl=512, c=C_CHUNK, out_dtype=ACT_DTYPE)
            acts = [t.reshape(n_p * l_p, D_MODEL) for t in (o, g)]
            hp = _post(hp2, acts, pp3, layer, wout, j, ng, lng, lnb, wproj, wgate, even=False, tr=512)
            hp = hp.reshape(n_p, l_p, D_MODEL)
            q, k, v, lf, g = _odd_in(hs2, w, lb_raw, j, tr=256)
            shp = (n_s, l_s, D_MODEL)
            o, hg_s = _hgrn(q.reshape(shp), k.reshape(shp), v.reshape(shp), lf.reshape(shp), state_hgrn_c, j,
                            hg_s, n_odd, nb=4, hb=8, tl=l_s, c=l_s)
            hs = _post(hs2, [o.reshape(n_s * l_s, D_MODEL), g], ps3, layer, wout, j, ng, lng, lnb, wproj, wgate,
                       even=False, tr=512)
            hs = hs.reshape(n_s, l_s, D_MODEL)
    wk_s, wv_s = (t.reshape(n_even, n_s, WINDOW, B_KV_HEADS, B_HD) for t in win_s)
    return (hp, hs, jnp.stack(conv_p), jnp.stack(conv_s), delta_p, delta_s,
            jnp.stack(wk_p), wk_s, jnp.stack(wv_p), wv_s, hg_p, hg_s)
```

```python
import functools
import math

import jax
import jax.numpy as jnp
from jax import lax
from jax.experimental import pallas as pl
from jax.experimental.pallas import tpu as pltpu

F32 = jnp.float32
BF16 = jnp.bfloat16

D_MODEL = 1024
DEPTH = 4
PAST_LEN = 8192
PLE_DIM = 256
NORM_EPS = 1e-6
MASK_NEG = -1e30
F_FLOOR = 1e-30
DEEPNORM_ALPHA = (2 * DEPTH) ** 0.25

A_HEADS = 4
A_DK = 128
A_CONV = 4
A_QK = 512
A_VW = 512
A_CONV_CH = 1536
A_CHUNK = 64

B_HEADS = 8
B_KV_HEADS = 2
B_HD = 64
B_QW = 512
B_KVW = 128
WINDOW = 128
ROT_DIM = 16
ROPE_THETA = 500000.0

C_HEADS = 8
C_DK = 128
C_CHUNK = 64

LANES = 128
SUBLANES = 8
LOG2_E = math.log2(math.e)
VMEM_LIMIT = 56 * 1024 * 1024

SUB_ROWS = 128
POST_SUB_ROWS = 256
PROMPT_TILE = 512
POST_TILE = 512
SWA_BLOCKS = 8
SAMPLE_NB_PROJ = 32
SAMPLE_NB_DELTA = 8
SAMPLE_NB_SWA = 8
SAMPLE_NB_HGRN = 4
SAMPLE_TILE_ODD = 256
ACT_DTYPE = BF16
CONV_ROW0 = SUBLANES
CONV_HIST = A_CONV - 1

E_QKV = 0
E_GA = 1536
E_QB = 2048
E_KB = 2560
E_VB = 2688
E_GB = 2816
E_AB = 3328
E_TOT = 3456

NT = (((1,), (1,)), ((), ()))
TN = (((0,), (0,)), ((), ()))


def _mm(a, b):
    return jnp.dot(a.astype(BF16), b.astype(BF16), preferred_element_type=F32)


def _mm_nt(a, b):
    return lax.dot_general(a.astype(BF16), b.astype(BF16), NT, preferred_element_type=F32)


def _mm_tn(a, b):
    return lax.dot_general(a.astype(BF16), b.astype(BF16), TN, preferred_element_type=F32)


def _sigmoid(x):
    return 1.0 / (1.0 + jnp.exp(-x))


def _silu(x):
    return x * _sigmoid(x)


def _cumsum_rows(x):
    n = x.shape[0]
    row = lax.broadcasted_iota(jnp.int32, x.shape, 0)
    s = 1
    while s < n:
        x = x + jnp.where(row >= s, pltpu.roll(x, s, axis=0), 0.0)
        s *= 2
    return x


def _params(n_grid):
    return pltpu.CompilerParams(dimension_semantics=("arbitrary",) * n_grid,
                                vmem_limit_bytes=VMEM_LIMIT)


def _const_spec(shape):
    nd = len(shape)
    return pl.BlockSpec(shape, lambda *_: (0,) * nd)


def _layer_spec(stacked, layer):
    nd = stacked.ndim
    return pl.BlockSpec((None,) + stacked.shape[1:], lambda *_: (layer,) + (0,) * (nd - 1))


def _ignore_first_ref(body, _stacked_ref, *refs):
    body(*refs)


def _stacked_spec(block, index_fn, j, n_layers, first):
    if first:
        return pl.BlockSpec((n_layers,) + block, lambda *g: (0,) + index_fn(*g))
    return pl.BlockSpec((None,) + block, lambda *g: (j,) + index_fn(*g))


def _slot_view(ref, slot):
    if slot is None:
        return ref
    j, n_layers = slot
    for i in range(n_layers):
        if i != j:
            ref[i] = jnp.zeros(ref.shape[1:], ref.dtype)
    return ref.at[j]


def _alias_stacked(body, in_specs, args, stacked, *out_indices):
    if stacked is None:
        return body, in_specs, args, {}
    if not isinstance(stacked, (list, tuple)):
        stacked = [stacked]
    for arr in reversed(stacked):
        body = functools.partial(_ignore_first_ref, body)
        in_specs = [pl.BlockSpec(memory_space=pl.ANY)] + list(in_specs)
        args = [arr] + list(args)
    return body, in_specs, args, {i: o for i, o in enumerate(out_indices)}


def _even_in_body(x_ref, w_ref, cw_ref, misc_ref, rot_ref, buf_ref,
                  qkv_ref, gb_ref, ga_ref, qb_ref, kvb_ref, gbt_ref, conv_ref, xp_ref,
                  *, nb, tl):
    t = pl.program_id(1)
    hist0 = CONV_ROW0 - CONV_HIST
    if buf_ref is not None:
        xp_ref[:, hist0:CONV_ROW0, :] = buf_ref[...]
    else:
        @pl.when(t == 0)
        def _():
            xp_ref[:, hist0:CONV_ROW0, :] = jnp.zeros((nb, CONV_HIST, A_CONV_CH), F32)

    neg_a = -jnp.exp(misc_ref[0:1, :])
    dt_bias = misc_ref[1:2, :]
    half = ROT_DIM // 2
    rep = B_QW // LANES
    q_scale = B_HD ** -0.5

    if nb == 1:
        nbs, ts = 1, min(tl, SUB_ROWS)
    else:
        nbs, ts = max(1, min(nb, SUB_ROWS // tl)), tl
    r = nbs * ts
    for b0 in range(0, nb, nbs):
        for r0 in range(0, tl, ts):
            bs = slice(b0, b0 + nbs)
            rs = slice(r0, r0 + ts)
            flat = slice(b0 * tl + r0, b0 * tl + r0 + r)
            x = x_ref[bs, rs, :].reshape(r, D_MODEL).astype(BF16)

            cw_blk = 2 * A_DK
            for c0 in range(0, A_CONV_CH, cw_blk):
                cols = slice(c0, c0 + cw_blk)
                z = jnp.dot(x, w_ref[:, E_QKV + c0:E_QKV + c0 + cw_blk], preferred_element_type=F32)
                xp_ref[bs, CONV_ROW0 + r0:CONV_ROW0 + r0 + ts, cols] = z.reshape(nbs, ts, cw_blk)
                y = xp_ref[bs, hist0 + r0:hist0 + r0 + ts, cols] * cw_ref[0:1, cols].reshape(1, 1, cw_blk)
                for j in range(1, A_CONV):
                    tap = xp_ref[bs, hist0 + j + r0:hist0 + j + r0 + ts, cols]
                    y = y + tap * cw_ref[j:j + 1, cols].reshape(1, 1, cw_blk)
                y = _silu(y).reshape(r, cw_blk)
                for h0 in range(0, cw_blk, A_DK):
                    c = y[:, h0:h0 + A_DK]
                    if c0 + h0 < 2 * A_QK:
                        inv = lax.rsqrt(jnp.sum(c * c, axis=-1, keepdims=True) + NORM_EPS)
                        if c0 + h0 < A_QK:
                            inv = inv * (A_DK ** -0.5)
                        c = c * inv
                    qkv_ref[bs, rs, c0 + h0:c0 + h0 + A_DK] = c.reshape(nbs, ts, A_DK)

            ab = jnp.dot(x, w_ref[:, E_AB:E_AB + LANES], preferred_element_type=F32)
            sp_in = ab + dt_bias
            softplus = jnp.maximum(sp_in, 0.0) + jnp.log1p(jnp.exp(-jnp.abs(sp_in)))
            lane = lax.broadcasted_iota(jnp.int32, ab.shape, 1)
            gb = jnp.where(lane < A_HEADS, neg_a * softplus, _sigmoid(ab))
            gb_ref[bs, rs, :] = gb.reshape(nbs, ts, LANES)

            ga = _silu(jnp.dot(x, w_ref[:, E_GA:E_GA + A_VW], preferred_element_type=F32))
            ga_ref[bs, rs, :] = ga.reshape(nbs, ts, A_VW).astype(ga_ref.dtype)
            gbt = _silu(jnp.dot(x, w_ref[:, E_GB:E_GB + B_QW], preferred_element_type=F32))
            gbt_ref[bs, rs, :] = gbt.reshape(nbs, ts, B_QW).astype(gbt_ref.dtype)

            cos_t = rot_ref[0, flat, :]
            s_up = rot_ref[1, flat, :]
            s_dn = rot_ref[2, flat, :]
            zq = jnp.dot(x, w_ref[:, E_QB:E_QB + B_QW], preferred_element_type=F32)
            cos4 = jnp.concatenate([cos_t * q_scale] * rep, axis=1)
            up4 = jnp.concatenate([s_up * q_scale] * rep, axis=1)
            dn4 = jnp.concatenate([s_dn * q_scale] * rep, axis=1)
            qrot = zq * cos4 + pltpu.roll(zq, B_QW - half, axis=1) * up4 + pltpu.roll(zq, half, axis=1) * dn4
            qb_ref[bs, rs, :] = qrot.reshape(nbs, ts, B_QW)
            zk = jnp.dot(x, w_ref[:, E_KB:E_KB + B_KVW], preferred_element_type=F32)
            krot = zk * cos_t + pltpu.roll(zk, LANES - half, axis=1) * s_up + pltpu.roll(zk, half, axis=1) * s_dn
            kvb_ref[bs, rs, 0:B_KVW] = krot.reshape(nbs, ts, B_KVW)
            zv = jnp.dot(x, w_ref[:, E_VB:E_VB + B_KVW], preferred_element_type=F32)
            kvb_ref[bs, rs, B_KVW:] = zv.reshape(nbs, ts, B_KVW)

    last = xp_ref[:, hist0 + tl:CONV_ROW0 + tl, :]
    conv_ref[...] = last
    if buf_ref is None:
        xp_ref[:, hist0:CONV_ROW0, :] = last


def _even_in_kernel_buf(x_ref, w_ref, cw_ref, misc_ref, rot_ref, buf_ref, *rest, nb, tl):
    _even_in_body(x_ref, w_ref, cw_ref, misc_ref, rot_ref, buf_ref, *rest, nb=nb, tl=tl)


def _even_in_kernel_nobuf(x_ref, w_ref, cw_ref, misc_ref, rot_ref, *rest, nb, tl):
    _even_in_body(x_ref, w_ref, cw_ref, misc_ref, rot_ref, None, *rest, nb=nb, tl=tl)


def _even_in(x, w, cw, misc, rot, buf, j, *, nb, tl, gate_dtype=F32):
    n, l, _ = x.shape
    grid = (n // nb, l // tl)
    r = nb * tl
    row_map = lambda i, t: (i, t, 0)
    rot_map = (lambda i, t: (0, t, 0)) if buf is None else (lambda i, t: (0, 0, 0))
    in_specs = [
        pl.BlockSpec((nb, tl, D_MODEL), row_map),
        _layer_spec(w, j),
        _const_spec((A_CONV, A_CONV_CH)),
        _const_spec((8, LANES)),
        pl.BlockSpec((3, r, LANES), rot_map),
    ]
    args = [x, w, cw, misc, rot]
    if buf is not None:
        in_specs.append(pl.BlockSpec((None, nb, A_CONV - 1, A_CONV_CH), lambda i, t: (j, i, 0, 0)))
        args.append(buf)
        body = functools.partial(_even_in_kernel_buf, nb=nb, tl=tl)
    else:
        body = functools.partial(_even_in_kernel_nobuf, nb=nb, tl=tl)
    widths = (A_CONV_CH, LANES, A_VW, B_QW, 2 * B_KVW, B_QW)
    dtypes = (F32, F32, gate_dtype, F32, F32, gate_dtype)
    out_shape = [jax.ShapeDtypeStruct((n, l, wd), dt) for wd, dt in zip(widths, dtypes)]
    out_specs = [pl.BlockSpec((nb, tl, wd), row_map) for wd in widths]
    out_shape.append(jax.ShapeDtypeStruct((n, A_CONV - 1, A_CONV_CH), F32))
    out_specs.append(pl.BlockSpec((nb, A_CONV - 1, A_CONV_CH), lambda i, t: (i, 0, 0)))
    return pl.pallas_call(
        body, grid=grid, in_specs=in_specs, out_specs=out_specs, out_shape=out_shape,
        scratch_shapes=[pltpu.VMEM((nb, CONV_ROW0 + tl, A_CONV_CH), F32)],
        compiler_params=_params(2), name="even_in",
    )(*args)


def _tri_inv_all(a_list, c):
    ri = lax.broadcasted_iota(jnp.int32, (c, LANES), 0)
    ci = lax.broadcasted_iota(jnp.int32, (c, LANES), 1)
    eye = jnp.where(ri == ci, 1.0, 0.0)
    ps = [eye for _ in a_list]
    ms = [-a for a in a_list]
    span = 1
    while span < c:
        mps = [_mm(m[:, :c], jnp.concatenate([m, p], axis=1)) for m, p in zip(ms, ps)]
        ms = [mp[:, :LANES] for mp in mps]
        ps = [p + mp[:, LANES:] for p, mp in zip(ps, mps)]
        span *= 2
    return ps


def _delta_prep_all(qs, ks, vs, g_cols, g_rows, betas, g_lasts, c):
    ri = lax.broadcasted_iota(jnp.int32, (c, LANES), 0)
    ci = lax.broadcasted_iota(jnp.int32, (c, LANES), 1)
    causal = ri >= ci
    strict = ri > ci
    decays = [jnp.where(causal, jnp.exp(jnp.where(causal, gc - gr, 0.0)), 0.0) for gc, gr in zip(g_cols, g_rows)]
    kbs = [k * b for k, b in zip(ks, betas)]
    if c < LANES:
        pad = jnp.zeros((LANES - c, A_DK), F32)
        k_rows = [jnp.concatenate([k, pad], axis=0).astype(BF16) for k in ks]
    else:
        k_rows = [k.astype(BF16) for k in ks]
    a_list = [jnp.where(strict, _mm_nt(kb, kr) * d, 0.0) for kb, kr, d in zip(kbs, k_rows, decays)]
    qks = [(_mm_nt(q, kr) * d)[:, :c].astype(BF16) for q, kr, d in zip(qs, k_rows, decays)]
    t_invs = _tri_inv_all(a_list, c)
    e_gs = [jnp.exp(gc) for gc in g_cols]
    uws = [_mm(t[:, :c], jnp.concatenate([v * b, kb * e], axis=1))
           for t, v, b, kb, e in zip(t_invs, vs, betas, kbs, e_gs)]
    us = [uw[:, :A_DK] for uw in uws]
    wqs = [jnp.concatenate([uw[:, A_DK:], q * e], axis=0).astype(BF16) for uw, q, e in zip(uws, qs, e_gs)]
    k_decs = [(k * jnp.exp(gl - gc)).astype(BF16) for k, gl, gc in zip(ks, g_lasts, g_cols)]
    e_lasts = [jnp.exp(gl) for gl in g_lasts]
    return us, wqs, qks, k_decs, e_lasts


def _delta_body(qkv_ref, gb_ref, s0_ref, o_ref, sout_ref, s_ref, *, nb, tl, c, slot):
    t = pl.program_id(1)
    nt = pl.num_programs(1)

    @pl.when(t == 0)
    def _():
        if s0_ref is None:
            s_ref[...] = jnp.zeros(s_ref.shape, F32)
        else:
            s_ref[...] = s0_ref[...]

    nc = tl // c
    qs, ks, vs, g_cols, g_rows, betas, g_lasts = [], [], [], [], [], [], []
    for ic in range(nc):
        rows = slice(ic * c, (ic + 1) * c)
        for b in range(nb):
            gbc = gb_ref[b, rows, :]
            g_all = _cumsum_rows(gbc)
            if c < LANES:
                g_t = jnp.concatenate([g_all, jnp.zeros((LANES - c, LANES), F32)], axis=0).T
            else:
                g_t = g_all.T
            for h in range(A_HEADS):
                qs.append(qkv_ref[b, rows, h * A_DK:(h + 1) * A_DK])
                ks.append(qkv_ref[b, rows, A_QK + h * A_DK:A_QK + (h + 1) * A_DK])
                vs.append(qkv_ref[b, rows, 2 * A_QK + h * A_DK:2 * A_QK + (h + 1) * A_DK])
                g_cols.append(g_all[:, h:h + 1])
                g_rows.append(g_t[h:h + 1, :])
                betas.append(gbc[:, A_HEADS + h:A_HEADS + h + 1])
                g_lasts.append(g_all[c - 1:c, h:h + 1])
    us, wqs, qks, k_decs, e_lasts = _delta_prep_all(qs, ks, vs, g_cols, g_rows, betas, g_lasts, c)

    nch = nb * A_HEADS
    states = [s_ref[b, h] for b in range(nb) for h in range(A_HEADS)]
    for ic in range(nc):
        sl = slice(ic * nch, (ic + 1) * nch)
        ws_qs = [_mm(wq, s) for wq, s in zip(wqs[sl], states)]
        v_news = [(u - x[:c]).astype(BF16) for u, x in zip(us[sl], ws_qs)]
        outs = [x[c:] + _mm(qk, vn) for x, qk, vn in zip(ws_qs, qks[sl], v_news)]
        states = [s * el + _mm_tn(kd, vn) for s, el, kd, vn in zip(states, e_lasts[sl], k_decs[sl], v_news)]
        for i, o in enumerate(outs):
            b, h = divmod(i, A_HEADS)
            o_ref[b, ic * c:(ic + 1) * c, h * A_DK:(h + 1) * A_DK] = o.astype(o_ref.dtype)
    for i, s in enumerate(states):
        b, h = divmod(i, A_HEADS)
        s_ref[b, h] = s

    @pl.when(t == nt - 1)
    def _():
        _slot_view(sout_ref, slot)[...] = s_ref[...]


def _delta_kernel_s0(qkv_ref, gb_ref, s0_ref, o_ref, sout_ref, s_ref, **kw):
    _delta_body(qkv_ref, gb_ref, s0_ref, o_ref, sout_ref, s_ref, **kw)


def _delta_kernel_zero(qkv_ref, gb_ref, o_ref, sout_ref, s_ref, **kw):
    _delta_body(qkv_ref, gb_ref, None, o_ref, sout_ref, s_ref, **kw)


def _delta(qkv, gb, s0, j, stacked, n_layers, *, nb, tl, c, out_dtype=F32):
    n, l, _ = qkv.shape
    grid = (n // nb, l // tl)
    row_map = lambda i, t: (i, t, 0)
    st_block = (nb, A_HEADS, A_DK, A_DK)
    st_map = lambda i, t: (i, 0, 0, 0)
    first = stacked is None
    kw = dict(nb=nb, tl=tl, c=c, slot=(j, n_layers) if first else None)
    in_specs = [pl.BlockSpec((nb, tl, A_CONV_CH), row_map), pl.BlockSpec((nb, tl, LANES), row_map)]
    args = [qkv, gb]
    if s0 is not None:
        in_specs.append(_stacked_spec(st_block, st_map, j, n_layers, False))
        args.append(s0)
        body = functools.partial(_delta_kernel_s0, **kw)
    else:
        body = functools.partial(_delta_kernel_zero, **kw)
    body, in_specs, args, aliases = _alias_stacked(body, in_specs, args, stacked, 1)
    return pl.pallas_call(
        body, grid=grid, in_specs=in_specs,
        out_specs=[pl.BlockSpec((nb, tl, A_VW), row_map), _stacked_spec(st_block, st_map, j, n_layers, first)],
        out_shape=[jax.ShapeDtypeStruct((n, l, A_VW), out_dtype),
                   jax.ShapeDtypeStruct((n_layers, n, A_HEADS, A_DK, A_DK), F32)],
        scratch_shapes=[pltpu.VMEM(st_block, F32)], input_output_aliases=aliases,
        compiler_params=_params(2), name="delta",
    )(*args)


def _swa_tasks(qs, kxs, vxs, first_min_key, sink_ref):
    nq = qs[0].shape[0]
    nk = 2 * WINDOW
    rows = B_HEADS * nq
    lo = lax.broadcasted_iota(jnp.int32, (nq, LANES), 1) < B_HD
    ri = lax.broadcasted_iota(jnp.int32, (rows, nk), 0) % nq
    ci = lax.broadcasted_iota(jnp.int32, (rows, nk), 1)
    band = (ci >= ri) & (ci <= ri + WINDOW)
    masks = [band & (ci >= first_min_key)] + [band] * (len(qs) - 1)
    head = lax.broadcasted_iota(jnp.int32, (rows, 1), 0) // nq
    sink = jnp.zeros((rows, 1), F32)
    for h in range(B_HEADS):
        sink = jnp.where(head == h, sink_ref[h], sink)

    def stack_heads(q):
        parts = []
        for h in range(B_HEADS):
            j, e = divmod(h, 2)
            g = h // (B_HEADS // B_KV_HEADS)
            src = q[:, j * LANES:(j + 1) * LANES]
            if e != g:
                src = pltpu.roll(src, B_HD, axis=1)
            parts.append(jnp.where(lo, src, 0.0) if g == 0 else jnp.where(lo, 0.0, src))
        return jnp.concatenate(parts, axis=0).astype(BF16)

    lhs = [stack_heads(q) for q in qs]
    ss = [jnp.where(mask, _mm_nt(x, kx), MASK_NEG) for mask, x, kx in zip(masks, lhs, kxs)]
    ms = [jnp.maximum(jnp.max(s, axis=-1, keepdims=True), sink) for s in ss]
    ps = [jnp.where(mask, jnp.exp(s - m), 0.0) for mask, s, m in zip(masks, ss, ms)]
    dens = [jnp.sum(p, axis=-1, keepdims=True) + jnp.exp(sink - m) for p, m in zip(ps, ms)]
    os_ = [_mm(p, vx) / d for p, d, vx in zip(ps, dens, vxs)]
    outs = []
    for o in os_:
        cols = []
        for j in range(B_HEADS // 2):
            a = o[(2 * j) * nq:(2 * j + 1) * nq]
            b = o[(2 * j + 1) * nq:(2 * j + 2) * nq]
            if j < B_HEADS // 4:
                cols.append(jnp.where(lo, a, pltpu.roll(b, B_HD, axis=1)))
            else:
                cols.append(jnp.where(lo, pltpu.roll(a, B_HD, axis=1), b))
        outs.append(jnp.concatenate(cols, axis=1))
    return outs


def _swa_prompt_kernel(sink_ref, q_ref, kvc_ref, kvp_ref, o_ref, *, nblk):
    i = pl.program_id(1)
    kv = [kvp_ref[0].astype(BF16)] + [kvc_ref[0, t * WINDOW:(t + 1) * WINDOW, :].astype(BF16) for t in range(nblk)]
    qs = [q_ref[0, t * WINDOW:(t + 1) * WINDOW, :] for t in range(nblk)]
    kxs = [jnp.concatenate([kv[t][:, :B_KVW], kv[t + 1][:, :B_KVW]], axis=0) for t in range(nblk)]
    vxs = [jnp.concatenate([kv[t][:, B_KVW:], kv[t + 1][:, B_KVW:]], axis=0) for t in range(nblk)]
    outs = _swa_tasks(qs, kxs, vxs, jnp.where(i > 0, 0, WINDOW), sink_ref)
    for t, o in enumerate(outs):
        o_ref[0, t * WINDOW:(t + 1) * WINDOW, :] = o.astype(o_ref.dtype)


def _swa_prompt(qb, kvb, sinks, *, nblk, out_dtype=F32):
    n, l, _ = qb.shape
    tq = nblk * WINDOW
    return pl.pallas_call(
        functools.partial(_swa_prompt_kernel, nblk=nblk), grid=(n, l // tq),
        in_specs=[pl.BlockSpec(memory_space=pltpu.SMEM),
                  pl.BlockSpec((1, tq, B_QW), lambda b, i: (b, i, 0)),
                  pl.BlockSpec((1, tq, 2 * B_KVW), lambda b, i: (b, i, 0)),
                  pl.BlockSpec((1, WINDOW, 2 * B_KVW), lambda b, i: (b, jnp.maximum(i * nblk - 1, 0), 0))],
        out_specs=pl.BlockSpec((1, tq, B_QW), lambda b, i: (b, i, 0)),
        out_shape=jax.ShapeDtypeStruct((n, l, B_QW), out_dtype),
        compiler_params=_params(2), name="swa_prompt",
    )(sinks, qb, kvb, kvb)


def _swa_sample_kernel(sink_ref, q_ref, kv_ref, ck_ref, cv_ref, o_ref, nk_ref, nv_ref, *, nb, ls, slot):
    nk_ref = _slot_view(nk_ref, slot)
    nv_ref = _slot_view(nv_ref, slot)
    pad = jnp.zeros((WINDOW - ls, B_KVW), F32)
    qs, kxs, vxs = [], [], []
    for b in range(nb):
        kv = kv_ref[b]
        k_new = kv[:, :B_KVW]
        v_new = kv[:, B_KVW:]
        ck = ck_ref[b]
        cv = cv_ref[b]
        qs.append(q_ref[b])
        kxs.append(jnp.concatenate([ck, k_new, pad], axis=0).astype(BF16))
        vxs.append(jnp.concatenate([cv, v_new, pad], axis=0).astype(BF16))
        nk_ref[b, 0:WINDOW - ls, :] = ck[ls:, :]
        nk_ref[b, WINDOW - ls:, :] = k_new
        nv_ref[b, 0:WINDOW - ls, :] = cv[ls:, :]
        nv_ref[b, WINDOW - ls:, :] = v_new
    for b, o in enumerate(_swa_tasks(qs, kxs, vxs, 0, sink_ref)):
        o_ref[b] = o


def _swa_sample(qb, kvb, ck, cv, sinks, j, stacked, *, nb):
    n, ls, _ = qb.shape
    row_map = lambda i: (i, 0, 0)
    n_layers = ck.shape[0]
    first = stacked is None
    cache_block = (nb, WINDOW, B_KVW)
    cache_spec = _stacked_spec(cache_block, row_map, j, n_layers, False)
    cache_out = _stacked_spec(cache_block, row_map, j, n_layers, first)
    cache_shape = jax.ShapeDtypeStruct(ck.shape, F32)
    body = functools.partial(_swa_sample_kernel, nb=nb, ls=ls, slot=(j, n_layers) if first else None)
    in_specs = [pl.BlockSpec(memory_space=pltpu.SMEM),
                pl.BlockSpec((nb, ls, B_QW), row_map),
                pl.BlockSpec((nb, ls, 2 * B_KVW), row_map),
                cache_spec, cache_spec]
    args = [sinks, qb, kvb, ck, cv]
    body, in_specs, args, aliases = _alias_stacked(body, in_specs, args, stacked, 1, 2)
    return pl.pallas_call(
        body, grid=(n // nb,), in_specs=in_specs,
        out_specs=[pl.BlockSpec((nb, ls, B_QW), row_map), cache_out, cache_out],
        out_shape=[jax.ShapeDtypeStruct((n, ls, B_QW), F32), cache_shape, cache_shape],
        input_output_aliases=aliases, compiler_params=_params(1), name="swa_sample",
    )(*args)


def _lower_bound(lbraw_ref, j):
    raw = lbraw_ref[...]
    e = jnp.exp(raw - jnp.max(raw, axis=0, keepdims=True))
    sm = e / jnp.sum(e, axis=0, keepdims=True)
    lb = jnp.zeros((1, D_MODEL), F32)
    for i in range(1, j + 1):
        lb = lb + sm[i:i + 1, :]
    return lb


def _odd_proj(x, w_ref, lb):
    d = D_MODEL

    def proj(c0):
        return jnp.dot(x, w_ref[:, c0:c0 + d], preferred_element_type=F32)

    q = _silu(proj(0)) * (C_DK ** -0.5)
    zf = proj(d)
    one_m_lb = 1.0 - lb
    u = one_m_lb * _sigmoid(zf)
    lf = jnp.log(jnp.maximum(lb + u, F_FLOOR))
    k = one_m_lb - u
    v = proj(2 * d)
    g = _silu(proj(3 * d))
    return q, k, v, lf, g


def _odd_in_kernel(x_ref, w_ref, lbraw_ref, q_ref, k_ref, v_ref, lf_ref, g_ref, *, j, ts):
    lb = _lower_bound(lbraw_ref, j)
    for s in range(x_ref.shape[0] // ts):
        rows = slice(s * ts, (s + 1) * ts)
        q, k, v, lf, g = _odd_proj(x_ref[rows, :].astype(BF16), w_ref, lb)
        q_ref[rows, :] = q
        k_ref[rows, :] = k
        v_ref[rows, :] = v
        lf_ref[rows, :] = lf
        g_ref[rows, :] = g


def _odd_in(x2, w, lb_raw, j, *, tr, ts=SUB_ROWS):
    rows = x2.shape[0]
    n_odd = lb_raw.shape[0]
    row_spec = pl.BlockSpec((tr, D_MODEL), lambda i: (i, 0))
    shp = jax.ShapeDtypeStruct((rows, D_MODEL), F32)
    return pl.pallas_call(
        functools.partial(_odd_in_kernel, j=j, ts=min(ts, tr)), grid=(rows // tr,),
        in_specs=[row_spec, _layer_spec(w, j), _const_spec((n_odd, D_MODEL))],
        out_specs=[row_spec] * 5, out_shape=[shp] * 5,
        compiler_params=_params(1), name="odd_in",
    )(x2, w, lb_raw)


def _hgrn_prep_all(qs, ks, vs, lfs, c, state_t):
    ng = c // SUBLANES
    shape3 = (ng, SUBLANES, C_DK)
    sub = lax.broadcasted_iota(jnp.int32, shape3, 1)

    def grouped(x):
        return x.reshape(shape3)

    def flat(x):
        return x.reshape(c, C_DK)

    gs = []
    for lf in lfs:
        x = grouped(lf * LOG2_E)
        s = 1
        while s < SUBLANES:
            x = x + jnp.where(sub >= s, pltpu.roll(x, s, axis=1), 0.0)
            s *= 2
        if ng > 1:
            tot = x[:, SUBLANES - 1:SUBLANES, :]
            run = jnp.zeros((1, 1, C_DK), F32)
            offs = []
            for i in range(ng):
                offs.append(run)
                run = run + tot[i:i + 1]
            x = x + jnp.concatenate(offs, axis=0)
        gs.append(x)
    rbs = [jnp.broadcast_to(g[:, 0:1, :], shape3) for g in gs]

    def block_start(rb, m):
        x = rb.reshape(ng // m, m, SUBLANES, C_DK)[:, 0:1]
        return jnp.broadcast_to(x, (ng // m, m, SUBLANES, C_DK)).reshape(shape3)

    ri = lax.broadcasted_iota(jnp.int32, (c, c), 0)
    ci = lax.broadcasted_iota(jnp.int32, (c, c), 1)
    q3s = [grouped(q) for q in qs]
    k3s = [grouped(k) for k in ks]
    accs = None
    bs = list(gs)
    h = 1
    while h < c:
        if h < SUBLANES:
            bns = [pltpu.roll(b, SUBLANES - h, axis=1) for b in bs]
        else:
            m = h // SUBLANES
            bs = [block_start(rb, m) for rb in rbs]
            bns = [jnp.concatenate([b[m:], g[ng - m:]], axis=0) for b, g in zip(bs, gs)]
        qhs = q3s if h == 1 else [q * jnp.exp2(g - b) for q, g, b in zip(q3s, gs, bs)]
        khs = [k * jnp.exp2(bn - g) for k, bn, g in zip(k3s, bns, gs)]
        lvl = ((ri // h) == (ci // h) + 1) & ((ri // (2 * h)) == (ci // (2 * h)))
        if h == 1 and ng > 1:
            gap = jnp.zeros((LANES - c, C_DK), F32)
            both = [_mm_nt(flat(q), jnp.concatenate([flat(k), gap, flat(kh)], axis=0))
                    for q, k, kh in zip(q3s, k3s, khs)]
            accs = [jnp.where(ri == ci, x[:, :c], jnp.where(lvl, x[:, LANES:], 0.0)) for x in both]
        elif h == 1:
            accs = [jnp.where(ri == ci, _mm_nt(flat(q), flat(k)), jnp.where(lvl, _mm_nt(flat(q), flat(kh)), 0.0))
                    for q, k, kh in zip(q3s, k3s, khs)]
        else:
            accs = [jnp.where(lvl, _mm_nt(flat(qh), flat(kh)), a) for a, qh, kh in zip(accs, qhs, khs)]
        if 2 * h < SUBLANES:
            sel = (sub & h) != 0
            bs = [jnp.where(sel, pltpu.roll(b, h, axis=1), b) for b in bs]
        h *= 2
    if accs is None:
        accs = [jnp.where(ri == ci, _mm_nt(q, k), 0.0) for q, k in zip(qs, ks)]
    g_lasts = [g[ng - 1, SUBLANES - 1:SUBLANES, :] for g in gs]
    intra = [_mm(a, v) for a, v in zip(accs, vs)]
    q_decs = [flat(q * jnp.exp2(g)).astype(BF16) for q, g in zip(q3s, gs)]
    k_decs = [flat(k * jnp.exp2(gl - g)) for k, gl, g in zip(k3s, g_lasts, gs)]
    e_lasts = [jnp.exp2(gl) for gl in g_lasts]
    if state_t:
        kvs = [_mm_tn(v, kd) for v, kd in zip(vs, k_decs)]
    else:
        kvs = [_mm_tn(kd, v) for v, kd in zip(vs, k_decs)]
        e_lasts = [jnp.broadcast_to(el, (C_DK, C_DK)).T for el in e_lasts]
    return intra, q_decs, kvs, e_lasts


def _hgrn_steps(qs, ks, vs, lfs, states, c, state_t):
    intra, q_decs, kvs, e_lasts = _hgrn_prep_all(qs, ks, vs, lfs, c, state_t)
    nch = len(states)
    read_state = _mm_nt if state_t else _mm
    outs = []
    for ic in range(len(qs) // nch):
        sl = slice(ic * nch, (ic + 1) * nch)
        outs += [read_state(qd, st) + x for qd, st, x in zip(q_decs[sl], states, intra[sl])]
        states = [st * el + kv for st, el, kv in zip(states, e_lasts[sl], kvs[sl])]
    return outs, states


def _hgrn_body(q_ref, k_ref, v_ref, lf_ref, s0_ref, o_ref, sout_ref, st_ref, *, nb, hb, tl, c, slot):
    t = pl.program_id(2)
    nt = pl.num_programs(2)
    nc = tl // c
    state_t = s0_ref is None

    @pl.when(t == 0)
    def _():
        if s0_ref is None:
            st_ref[...] = jnp.zeros(st_ref.shape, F32)
        else:
            st_ref[...] = s0_ref[...]

    chains = [(b, h) for b in range(nb) for h in range(hb)]
    where = [(ic, b, h) for ic in range(nc) for b, h in chains]

    def load(ref):
        return [ref[b, ic * c:(ic + 1) * c, h * C_DK:(h + 1) * C_DK] for ic, b, h in where]

    states = [st_ref[b, h] for b, h in chains]
    outs, states = _hgrn_steps(load(q_ref), load(k_ref), load(v_ref), load(lf_ref), states, c, state_t)
    for (ic, b, h), o in zip(where, outs):
        o_ref[b, ic * c:(ic + 1) * c, h * C_DK:(h + 1) * C_DK] = o
    for (b, h), st in zip(chains, states):
        st_ref[b, h] = st

    @pl.when(t == nt - 1)
    def _():
        out = _slot_view(sout_ref, slot)
        for b in range(nb):
            for h in range(hb):
                out[b, h] = st_ref[b, h].T if state_t else st_ref[b, h]


def _hgrn_kernel_s0(q_ref, k_ref, v_ref, lf_ref, s0_ref, o_ref, sout_ref, st_ref, **kw):
    _hgrn_body(q_ref, k_ref, v_ref, lf_ref, s0_ref, o_ref, sout_ref, st_ref, **kw)


def _hgrn_kernel_zero(q_ref, k_ref, v_ref, lf_ref, o_ref, sout_ref, st_ref, **kw):
    _hgrn_body(q_ref, k_ref, v_ref, lf_ref, None, o_ref, sout_ref, st_ref, **kw)


def _hgrn(q, k, v, lf, s0, j, stacked, n_layers, *, nb, hb, tl, c):
    n, l, _ = q.shape
    grid = (n // nb, C_HEADS // hb, l // tl)
    row_spec = pl.BlockSpec((nb, tl, hb * C_DK), lambda i, g, t: (i, t, g))
    st_block = (nb, hb, C_DK, C_DK)
    st_map = lambda i, g, t: (i, g, 0, 0)
    first = stacked is None
    st_spec = _stacked_spec(st_block, st_map, j, n_layers, first)
    in_specs = [row_spec] * 4
    args = [q, k, v, lf]
    kw = dict(nb=nb, hb=hb, tl=tl, c=c, slot=(j, n_layers) if first else None)
    if s0 is not None:
        in_specs = in_specs + [_stacked_spec(st_block, st_map, j, n_layers, False)]
        args.append(s0)
        body = functools.partial(_hgrn_kernel_s0, **kw)
    else:
        body = functools.partial(_hgrn_kernel_zero, **kw)
    body, in_specs, args, aliases = _alias_stacked(body, in_specs, args, stacked, 1)
    return pl.pallas_call(
        body, grid=grid, in_specs=in_specs, out_specs=[row_spec, st_spec],
        out_shape=[jax.ShapeDtypeStruct((n, l, D_MODEL), F32),
                   jax.ShapeDtypeStruct((n_layers, n, C_HEADS, C_DK, C_DK), F32)],
        scratch_shapes=[pltpu.VMEM(st_block, F32)], input_output_aliases=aliases,
        compiler_params=_params(3), name="hgrn",
    )(*args)


def _odd_fused_kernel(x_ref, w_ref, lbraw_ref, o_ref, g_ref, sout_ref, st_ref, *, j, tl, c, ts, slot):
    t = pl.program_id(1)
    nt = pl.num_programs(1)

    @pl.when(t == 0)
    def _():
        st_ref[...] = jnp.zeros(st_ref.shape, F32)

    lb = _lower_bound(lbraw_ref, j)
    heads = range(C_HEADS)

    def project(s):
        rows = slice(s * ts, (s + 1) * ts)
        q, k, v, lf, g = _odd_proj(x_ref[0, rows, :].astype(BF16), w_ref, lb)
        g_ref[0, rows, :] = g.astype(g_ref.dtype)
        return q, k, v, lf

    def mix(s, qkvl, states):
        where = [(ic, h) for ic in range(ts // c) for h in heads]
        qs, ks, vs, lfs = ([a[ic * c:(ic + 1) * c, h * C_DK:(h + 1) * C_DK] for ic, h in where] for a in qkvl)
        outs, states = _hgrn_steps(qs, ks, vs, lfs, states, c, True)
        for (ic, h), o in zip(where, outs):
            o_ref[0, s * ts + ic * c:s * ts + (ic + 1) * c, h * C_DK:(h + 1) * C_DK] = o.astype(o_ref.dtype)
        return states

    states = [st_ref[h] for h in heads]
    n_sub = tl // ts
    pending = project(0)
    for s in range(n_sub):
        following = project(s + 1) if s + 1 < n_sub else None
        states = mix(s, pending, states)
        pending = following
    for h in heads:
        st_ref[h] = states[h]

    @pl.when(t == nt - 1)
    def _():
        out = _slot_view(sout_ref, slot)
        for h in heads:
            out[0, h] = st_ref[h].T


def _odd_fused(x, w, lb_raw, j, stacked, n_layers, *, tl, c, ts=SUB_ROWS, out_dtype=F32):
    n, l, _ = x.shape
    row_spec = pl.BlockSpec((1, tl, D_MODEL), lambda i, t: (i, t, 0))
    st_block = (1, C_HEADS, C_DK, C_DK)
    first = stacked is None
    body = functools.partial(_odd_fused_kernel, j=j, tl=tl, c=c, ts=ts, slot=(j, n_layers) if first else None)
    in_specs = [row_spec, _layer_spec(w, j), _const_spec(lb_raw.shape)]
    body, in_specs, args, aliases = _alias_stacked(body, in_specs, [x, w, lb_raw], stacked, 2)
    shp = jax.ShapeDtypeStruct((n, l, D_MODEL), out_dtype)
    return pl.pallas_call(
        body, grid=(n, l // tl), in_specs=in_specs,
        out_specs=[row_spec, row_spec, _stacked_spec(st_block, lambda i, t: (i, 0, 0, 0), j, n_layers, first)],
        out_shape=[shp, shp, jax.ShapeDtypeStruct((n_layers, n, C_HEADS, C_DK, C_DK), F32)],
        scratch_shapes=[pltpu.VMEM((C_HEADS, C_DK, C_DK), F32)], input_output_aliases=aliases,
        compiler_params=_params(2), name="odd_fused",
    )(*args)


def _group_rms(o, g_row, heads):
    outs = []
    for h in range(heads):
        c = o[:, h * LANES:(h + 1) * LANES]
        inv = lax.rsqrt(jnp.mean(c * c, axis=-1, keepdims=True) + NORM_EPS)
        outs.append(c * inv * g_row)
    return jnp.concatenate(outs, axis=1)


def _post_tail(x_ref, ys, p_ref, lng_ref, lnb_ref, wproj_ref, wgate_ref, out_ref, tiles):
    hns = []
    for rows, y in zip(tiles, ys):
        pre = DEEPNORM_ALPHA * x_ref[rows, :] + y
        mu = jnp.mean(pre, axis=-1, keepdims=True)
        xc = pre - mu
        var = jnp.mean(xc * xc, axis=-1, keepdims=True)
        hns.append(xc * lax.rsqrt(var + NORM_EPS) * lng_ref[...] + lnb_ref[...])
    embs = [jnp.dot(p_ref[rows, :].astype(BF16), wproj_ref[:, 0:D_MODEL], preferred_element_type=F32)
            for rows in tiles]
    gates = [_sigmoid(jnp.dot(hn.astype(BF16), wgate_ref[:, 0:D_MODEL], preferred_element_type=F32)) for hn in hns]
    for rows, hn, gate, emb in zip(tiles, hns, gates, embs):
        out_ref[rows, :] = hn + gate * emb


def _sub_tiles(n_rows):
    ts = min(n_rows, POST_SUB_ROWS)
    return [slice(r0, r0 + ts) for r0 in range(0, n_rows, ts)]


def _post_even_kernel(x_ref, oa_ref, ga_ref, ob_ref, gbt_ref, p_ref, wout_ref, ng_ref, lng_ref, lnb_ref,
                      wproj_ref, wgate_ref, out_ref):
    tiles = _sub_tiles(x_ref.shape[0])
    a_s = [(_group_rms(oa_ref[rows, :].astype(F32), ng_ref[...], A_HEADS) * ga_ref[rows, :].astype(F32)).astype(BF16)
           for rows in tiles]
    b_s = [(ob_ref[rows, :].astype(F32) * gbt_ref[rows, :].astype(F32)).astype(BF16) for rows in tiles]
    ys = [jnp.dot(a, wout_ref[0:A_VW, 0:D_MODEL], preferred_element_type=F32)
          + jnp.dot(b, wout_ref[A_VW:, 0:D_MODEL], preferred_element_type=F32) for a, b in zip(a_s, b_s)]
    _post_tail(x_ref, ys, p_ref, lng_ref, lnb_ref, wproj_ref, wgate_ref, out_ref, tiles)


def _post_odd_kernel(x_ref, o_ref, g_ref, p_ref, wout_ref, ng_ref, lng_ref, lnb_ref,
                     wproj_ref, wgate_ref, out_ref):
    tiles = _sub_tiles(x_ref.shape[0])
    a_s = [(_group_rms(o_ref[rows, :].astype(F32), ng_ref[...], C_HEADS) * g_ref[rows, :].astype(F32)).astype(BF16)
           for rows in tiles]
    ys = [jnp.dot(a, wout_ref[:, 0:D_MODEL], preferred_element_type=F32) for a in a_s]
    _post_tail(x_ref, ys, p_ref, lng_ref, lnb_ref, wproj_ref, wgate_ref, out_ref, tiles)


def _post(x2, acts, p3, layer, wout, j, ng, lng, lnb, wproj, wgate, *, even, tr):
    rows = x2.shape[0]
    full = pl.BlockSpec((tr, D_MODEL), lambda i: (i, 0))
    half = pl.BlockSpec((tr, A_VW), lambda i: (i, 0))
    act_specs = [half] * 4 if even else [full] * 2
    in_specs = ([full] + act_specs + [pl.BlockSpec((None, tr, PLE_DIM), lambda i: (layer, i, 0)),
                                      _layer_spec(wout, j), _const_spec((1, LANES)),
                                      _const_spec((1, D_MODEL)), _const_spec((1, D_MODEL)),
                                      _layer_spec(wproj, layer), _layer_spec(wgate, layer)])
    return pl.pallas_call(
        _post_even_kernel if even else _post_odd_kernel, grid=(rows // tr,),
        in_specs=in_specs, out_specs=full, out_shape=jax.ShapeDtypeStruct((rows, D_MODEL), F32),
        compiler_params=_params(1), name="post_even" if even else "post_odd",
    )(x2, *acts, p3, wout, ng, lng, lnb, wproj, wgate)


def _bf16_padded(w):
    return jnp.pad(w.astype(BF16), ((0, 0), (0, 0), (0, LANES)))


def _pack_even_w(w):
    w = w.astype(BF16)
    qkv = w[..., 0:1536]
    ga = w[..., 1536:2048]
    ab = w[..., 2048:2056]
    qb = w[..., 2056:2568]
    kb = w[..., 2568:2696]
    vb = w[..., 2696:2824]
    gbt = w[..., 2824:3336]
    ab = jnp.pad(ab, ((0, 0), (0, 0), (0, LANES - 2 * A_HEADS)))
    return jnp.concatenate([qkv, ga, qb, kb, vb, gbt, ab], axis=-1)


def _rot_tables(pos, reps):
    inv = ROPE_THETA ** (-jnp.arange(0, ROT_DIM, 2, dtype=F32) / ROT_DIM)
    ang = pos.astype(F32)[:, None] * inv[None, :]
    cos = jnp.cos(ang)
    sin = jnp.sin(ang)
    n = pos.shape[0]
    half = ROT_DIM // 2
    cos64 = jnp.concatenate([cos, cos, jnp.ones((n, B_HD - ROT_DIM), F32)], axis=1)
    up64 = jnp.concatenate([-sin, jnp.zeros((n, B_HD - half), F32)], axis=1)
    dn64 = jnp.concatenate([jnp.zeros((n, half), F32), sin, jnp.zeros((n, B_HD - ROT_DIM), F32)], axis=1)
    tabs = [jnp.tile(t, (reps, LANES // B_HD)) for t in (cos64, up64, dn64)]
    return jnp.stack(tabs)


def kernel(x_prompt, x_sample, state_conv_a, state_delta_a, cache_win_k, cache_win_v, state_hgrn_c,
           p_prompt, p_sample, w_in_even, conv_w_a, a_log, dt_bias, norm_a, sinks_b, w_out_even,
           w_in_odd, lb_raw, norm_c, w_out_odd, ln_g, ln_b, w_ple_proj, w_ple_gate):
    n_p, l_p, _ = x_prompt.shape
    n_s, l_s, _ = x_sample.shape
    nb_s = SAMPLE_NB_PROJ
    rot_p = _rot_tables(jnp.arange(l_p), 1)
    rot_s = _rot_tables(PAST_LEN + jnp.arange(l_s), nb_s)
    n_even, n_odd = w_in_even.shape[0], w_in_odd.shape[0]
    conv_p, conv_s, wk_p, wv_p = [], [], [], []
    delta_p = delta_s = hg_p = hg_s = win_s = None
    hp, hs = x_prompt, x_sample
    pp3 = p_prompt.reshape(DEPTH, n_p * l_p, PLE_DIM)
    ps3 = p_sample.reshape(DEPTH, n_s * l_s, PLE_DIM)
    ck_all = cache_win_k.reshape(-1, n_s, WINDOW, B_KVW)
    cv_all = cache_win_v.reshape(-1, n_s, WINDOW, B_KVW)
    w_even = _pack_even_w(w_in_even)
    w_odd = _bf16_padded(w_in_odd)
    wout_even = _bf16_padded(w_out_even)
    wout_odd = _bf16_padded(w_out_odd)
    wproj = _bf16_padded(w_ple_proj)
    wgate = _bf16_padded(w_ple_gate)
    for layer in range(DEPTH):
        j = layer // 2
        lng = ln_g[layer].reshape(1, D_MODEL)
        lnb = ln_b[layer].reshape(1, D_MODEL)
        hp2 = hp.reshape(n_p * l_p, D_MODEL)
        hs2 = hs.reshape(n_s * l_s, D_MODEL)
        if layer % 2 == 0:
            w = w_even
            misc = jnp.zeros((8, LANES), F32)
            misc = misc.at[0, :A_HEADS].set(a_log[j]).at[1, :A_HEADS].set(dt_bias[j])
            cw = conv_w_a[j]
            sinks = sinks_b[j]
            wout = wout_even
            ng = norm_a[j].reshape(1, LANES)
            qkv, gb, ga, qb, kvb, gbt, c1 = _even_in(hp, w, cw, misc, rot_p, None, j, nb=1, tl=PROMPT_TILE,
                                                     gate_dtype=ACT_DTYPE)
            oa, delta_p = _delta(qkv, gb, None, j, delta_p, n_even, nb=1, tl=PROMPT_TILE, c=A_CHUNK,
                                 out_dtype=ACT_DTYPE)
            ob = _swa_prompt(qb, kvb, sinks, nblk=SWA_BLOCKS, out_dtype=ACT_DTYPE)
            acts = [t.reshape(n_p * l_p, A_VW) for t in (oa, ga, ob, gbt)]
            hp = _post(hp2, acts, pp3, layer, wout, j, ng, lng, lnb, wproj, wgate, even=True, tr=POST_TILE)
            hp = hp.reshape(n_p, l_p, D_MODEL)
            k1 = kvb[:, l_p - WINDOW:, :B_KVW].reshape(n_p, WINDOW, B_KV_HEADS, B_HD)
            v1 = kvb[:, l_p - WINDOW:, B_KVW:].reshape(n_p, WINDOW, B_KV_HEADS, B_HD)
            qkv, gb, ga, qb, kvb, gbt, c2 = _even_in(hs, w, cw, misc, rot_s, state_conv_a, j, nb=nb_s, tl=l_s)
            oa, delta_s = _delta(qkv, gb, state_delta_a, j, delta_s, n_even, nb=SAMPLE_NB_DELTA, tl=l_s, c=l_s)
            ob, *win_s = _swa_sample(qb, kvb, ck_all, cv_all, sinks, j, win_s, nb=SAMPLE_NB_SWA)
            acts = [t.reshape(n_s * l_s, A_VW) for t in (oa, ga, ob, gbt)]
            hs = _post(hs2, acts, ps3, layer, wout, j, ng, lng, lnb, wproj, wgate, even=True, tr=POST_TILE)
            hs = hs.reshape(n_s, l_s, D_MODEL)
            conv_p.append(c1); conv_s.append(c2)
            wk_p.append(k1); wv_p.append(v1)
        else:
            w = w_odd
            wout = wout_odd
            ng = norm_c[j].reshape(1, LANES)
            o, g, hg_p = _odd_fused(hp, w, lb_raw, j, hg_p, n_odd, tl=PROMPT_TILE, c=C_CHUNK, out_dtype=ACT_DTYPE)
            acts = [t.reshape(n_p * l_p, D_MODEL) for t in (o, g)]
            hp = _post(hp2, acts, pp3, layer, wout, j, ng, lng, lnb, wproj, wgate, even=False, tr=POST_TILE)
            hp = hp.reshape(n_p, l_p, D_MODEL)
            q, k, v, lf, g = _odd_in(hs2, w, lb_raw, j, tr=SAMPLE_TILE_ODD)
            shp = (n_s, l_s, D_MODEL)
            o, hg_s = _hgrn(q.reshape(shp), k.reshape(shp), v.reshape(shp), lf.reshape(shp), state_hgrn_c, j,
                            hg_s, n_odd, nb=SAMPLE_NB_HGRN, hb=C_HEADS, tl=l_s, c=l_s)
            hs = _post(hs2, [o.reshape(n_s * l_s, D_MODEL), g], ps3, layer, wout, j, ng, lng, lnb, wproj, wgate,
                       even=False, tr=POST_TILE)
            hs = hs.reshape(n_s, l_s, D_MODEL)
    wk_s, wv_s = (t.reshape(n_even, n_s, WINDOW, B_KV_HEADS, B_HD) for t in win_s)
    return (hp, hs, jnp.stack(conv_p), jnp.stack(conv_s), delta_p, delta_s,
            jnp.stack(wk_p), wk_s, jnp.stack(wv_p), wv_s, hg_p, hg_s)
```

```python
import functools
import math

import jax
import jax.numpy as jnp
from jax import lax
from jax.experimental import pallas as pl
from jax.experimental.pallas import tpu as pltpu

F32 = jnp.float32
BF16 = jnp.bfloat16

D_MODEL = 1024
DEPTH = 4
PAST_LEN = 8192
PLE_DIM = 256
NORM_EPS = 1e-6
MASK_NEG = -1e30
F_FLOOR = 1e-30
DEEPNORM_ALPHA = (2 * DEPTH) ** 0.25

A_HEADS = 4
A_DK = 128
A_CONV = 4
A_QK = 512
A_VW = 512
A_CONV_CH = 1536
A_CHUNK = 64

B_HEADS = 8
B_KV_HEADS = 2
B_HD = 64
B_QW = 512
B_KVW = 128
WINDOW = 128
ROT_DIM = 16
ROPE_THETA = 500000.0

C_HEADS = 8
C_DK = 128
C_CHUNK = 64

LANES = 128
SUBLANES = 8
LOG2_E = math.log2(math.e)
VMEM_LIMIT = 56 * 1024 * 1024

SUB_ROWS = 128
POST_SUB_ROWS = 256
PROMPT_TILE = 512
POST_TILE = 1024
SWA_BLOCKS = 8
SAMPLE_NB_PROJ = 32
SAMPLE_NB_DELTA = 8
SAMPLE_NB_SWA = 8
SAMPLE_NB_HGRN = 4
SAMPLE_TILE_ODD = 256
ACT_DTYPE = BF16
CONV_ROW0 = SUBLANES
CONV_HIST = A_CONV - 1

E_QKV = 0
E_GA = 1536
E_QB = 2048
E_KB = 2560
E_VB = 2688
E_GB = 2816
E_AB = 3328
E_TOT = 3456

NT = (((1,), (1,)), ((), ()))
TN = (((0,), (0,)), ((), ()))


def _mm(a, b):
    return jnp.dot(a.astype(BF16), b.astype(BF16), preferred_element_type=F32)


def _mm_nt(a, b):
    return lax.dot_general(a.astype(BF16), b.astype(BF16), NT, preferred_element_type=F32)


def _mm_tn(a, b):
    return lax.dot_general(a.astype(BF16), b.astype(BF16), TN, preferred_element_type=F32)


def _sigmoid(x):
    return 1.0 / (1.0 + jnp.exp(-x))


def _silu(x):
    return x * _sigmoid(x)


def _cumsum_rows(x):
    n = x.shape[0]
    row = lax.broadcasted_iota(jnp.int32, x.shape, 0)
    s = 1
    while s < n:
        x = x + jnp.where(row >= s, pltpu.roll(x, s, axis=0), 0.0)
        s *= 2
    return x


def _params(n_grid):
    return pltpu.CompilerParams(dimension_semantics=("arbitrary",) * n_grid,
                                vmem_limit_bytes=VMEM_LIMIT)


def _const_spec(shape):
    nd = len(shape)
    return pl.BlockSpec(shape, lambda *_: (0,) * nd)


def _layer_spec(stacked, layer):
    nd = stacked.ndim
    return pl.BlockSpec((None,) + stacked.shape[1:], lambda *_: (layer,) + (0,) * (nd - 1))


def _ignore_first_ref(body, _stacked_ref, *refs):
    body(*refs)


def _stacked_spec(block, index_fn, j, n_layers, first):
    if first:
        return pl.BlockSpec((n_layers,) + block, lambda *g: (0,) + index_fn(*g))
    return pl.BlockSpec((None,) + block, lambda *g: (j,) + index_fn(*g))


def _slot_view(ref, slot):
    if slot is None:
        return ref
    j, n_layers = slot
    for i in range(n_layers):
        if i != j:
            ref[i] = jnp.zeros(ref.shape[1:], ref.dtype)
    return ref.at[j]


def _alias_stacked(body, in_specs, args, stacked, *out_indices):
    if stacked is None:
        return body, in_specs, args, {}
    if not isinstance(stacked, (list, tuple)):
        stacked = [stacked]
    for arr in reversed(stacked):
        body = functools.partial(_ignore_first_ref, body)
        in_specs = [pl.BlockSpec(memory_space=pl.ANY)] + list(in_specs)
        args = [arr] + list(args)
    return body, in_specs, args, {i: o for i, o in enumerate(out_indices)}


def _even_in_body(x_ref, w_ref, cw_ref, misc_ref, rot_ref, buf_ref,
                  qkv_ref, gb_ref, ga_ref, qb_ref, kvb_ref, gbt_ref, conv_ref, xp_ref,
                  *, nb, tl):
    t = pl.program_id(1)
    hist0 = CONV_ROW0 - CONV_HIST
    if buf_ref is not None:
        xp_ref[:, hist0:CONV_ROW0, :] = buf_ref[...]
    else:
        @pl.when(t == 0)
        def _():
            xp_ref[:, hist0:CONV_ROW0, :] = jnp.zeros((nb, CONV_HIST, A_CONV_CH), F32)

    neg_a = -jnp.exp(misc_ref[0:1, :])
    dt_bias = misc_ref[1:2, :]
    half = ROT_DIM // 2
    rep = B_QW // LANES
    q_scale = B_HD ** -0.5

    if nb == 1:
        nbs, ts = 1, min(tl, SUB_ROWS)
    else:
        nbs, ts = max(1, min(nb, SUB_ROWS // tl)), tl
    r = nbs * ts
    for b0 in range(0, nb, nbs):
        for r0 in range(0, tl, ts):
            bs = slice(b0, b0 + nbs)
            rs = slice(r0, r0 + ts)
            flat = slice(b0 * tl + r0, b0 * tl + r0 + r)
            x = x_ref[bs, rs, :].reshape(r, D_MODEL).astype(BF16)

            cw_blk = 2 * A_DK
            for c0 in range(0, A_CONV_CH, cw_blk):
                cols = slice(c0, c0 + cw_blk)
                z = jnp.dot(x, w_ref[:, E_QKV + c0:E_QKV + c0 + cw_blk], preferred_element_type=F32)
                xp_ref[bs, CONV_ROW0 + r0:CONV_ROW0 + r0 + ts, cols] = z.reshape(nbs, ts, cw_blk)
                y = xp_ref[bs, hist0 + r0:hist0 + r0 + ts, cols] * cw_ref[0:1, cols].reshape(1, 1, cw_blk)
                for j in range(1, A_CONV):
                    tap = xp_ref[bs, hist0 + j + r0:hist0 + j + r0 + ts, cols]
                    y = y + tap * cw_ref[j:j + 1, cols].reshape(1, 1, cw_blk)
                y = _silu(y).reshape(r, cw_blk)
                for h0 in range(0, cw_blk, A_DK):
                    c = y[:, h0:h0 + A_DK]
                    if c0 + h0 < 2 * A_QK:
                        inv = lax.rsqrt(jnp.sum(c * c, axis=-1, keepdims=True) + NORM_EPS)
                        if c0 + h0 < A_QK:
                            inv = inv * (A_DK ** -0.5)
                        c = c * inv
                    qkv_ref[bs, rs, c0 + h0:c0 + h0 + A_DK] = c.reshape(nbs, ts, A_DK)

            ab = jnp.dot(x, w_ref[:, E_AB:E_AB + LANES], preferred_element_type=F32)
            sp_in = ab + dt_bias
            softplus = jnp.maximum(sp_in, 0.0) + jnp.log1p(jnp.exp(-jnp.abs(sp_in)))
            lane = lax.broadcasted_iota(jnp.int32, ab.shape, 1)
            gb = jnp.where(lane < A_HEADS, neg_a * softplus, _sigmoid(ab))
            gb_ref[bs, rs, :] = gb.reshape(nbs, ts, LANES)

            ga = _silu(jnp.dot(x, w_ref[:, E_GA:E_GA + A_VW], preferred_element_type=F32))
            ga_ref[bs, rs, :] = ga.reshape(nbs, ts, A_VW).astype(ga_ref.dtype)
            gbt = _silu(jnp.dot(x, w_ref[:, E_GB:E_GB + B_QW], preferred_element_type=F32))
            gbt_ref[bs, rs, :] = gbt.reshape(nbs, ts, B_QW).astype(gbt_ref.dtype)

            cos_t = rot_ref[0, flat, :]
            s_up = rot_ref[1, flat, :]
            s_dn = rot_ref[2, flat, :]
            zq = jnp.dot(x, w_ref[:, E_QB:E_QB + B_QW], preferred_element_type=F32)
            cos4 = jnp.concatenate([cos_t * q_scale] * rep, axis=1)
            up4 = jnp.concatenate([s_up * q_scale] * rep, axis=1)
            dn4 = jnp.concatenate([s_dn * q_scale] * rep, axis=1)
            qrot = zq * cos4 + pltpu.roll(zq, B_QW - half, axis=1) * up4 + pltpu.roll(zq, half, axis=1) * dn4
            qb_ref[bs, rs, :] = qrot.reshape(nbs, ts, B_QW).astype(qb_ref.dtype)
            zk = jnp.dot(x, w_ref[:, E_KB:E_KB + B_KVW], preferred_element_type=F32)
            krot = zk * cos_t + pltpu.roll(zk, LANES - half, axis=1) * s_up + pltpu.roll(zk, half, axis=1) * s_dn
            kvb_ref[bs, rs, 0:B_KVW] = krot.reshape(nbs, ts, B_KVW)
            zv = jnp.dot(x, w_ref[:, E_VB:E_VB + B_KVW], preferred_element_type=F32)
            kvb_ref[bs, rs, B_KVW:] = zv.reshape(nbs, ts, B_KVW)

    last = xp_ref[:, hist0 + tl:CONV_ROW0 + tl, :]
    conv_ref[...] = last
    if buf_ref is None:
        xp_ref[:, hist0:CONV_ROW0, :] = last


def _even_in_kernel_buf(x_ref, w_ref, cw_ref, misc_ref, rot_ref, buf_ref, *rest, nb, tl):
    _even_in_body(x_ref, w_ref, cw_ref, misc_ref, rot_ref, buf_ref, *rest, nb=nb, tl=tl)


def _even_in_kernel_nobuf(x_ref, w_ref, cw_ref, misc_ref, rot_ref, *rest, nb, tl):
    _even_in_body(x_ref, w_ref, cw_ref, misc_ref, rot_ref, None, *rest, nb=nb, tl=tl)


def _even_in(x, w, cw, misc, rot, buf, j, *, nb, tl, gate_dtype=F32):
    n, l, _ = x.shape
    grid = (n // nb, l // tl)
    r = nb * tl
    row_map = lambda i, t: (i, t, 0)
    rot_map = (lambda i, t: (0, t, 0)) if buf is None else (lambda i, t: (0, 0, 0))
    in_specs = [
        pl.BlockSpec((nb, tl, D_MODEL), row_map),
        _layer_spec(w, j),
        _const_spec((A_CONV, A_CONV_CH)),
        _const_spec((8, LANES)),
        pl.BlockSpec((3, r, LANES), rot_map),
    ]
    args = [x, w, cw, misc, rot]
    if buf is not None:
        in_specs.append(pl.BlockSpec((None, nb, A_CONV - 1, A_CONV_CH), lambda i, t: (j, i, 0, 0)))
        args.append(buf)
        body = functools.partial(_even_in_kernel_buf, nb=nb, tl=tl)
    else:
        body = functools.partial(_even_in_kernel_nobuf, nb=nb, tl=tl)
    widths = (A_CONV_CH, LANES, A_VW, B_QW, 2 * B_KVW, B_QW)
    dtypes = (F32, F32, gate_dtype, gate_dtype, F32, gate_dtype)
    out_shape = [jax.ShapeDtypeStruct((n, l, wd), dt) for wd, dt in zip(widths, dtypes)]
    out_specs = [pl.BlockSpec((nb, tl, wd), row_map) for wd in widths]
    out_shape.append(jax.ShapeDtypeStruct((n, A_CONV - 1, A_CONV_CH), F32))
    out_specs.append(pl.BlockSpec((nb, A_CONV - 1, A_CONV_CH), lambda i, t: (i, 0, 0)))
    return pl.pallas_call(
        body, grid=grid, in_specs=in_specs, out_specs=out_specs, out_shape=out_shape,
        scratch_shapes=[pltpu.VMEM((nb, CONV_ROW0 + tl, A_CONV_CH), F32)],
        compiler_params=_params(2), name="even_in",
    )(*args)


def _tri_inv_all(a_list, c):
    ri = lax.broadcasted_iota(jnp.int32, (c, LANES), 0)
    ci = lax.broadcasted_iota(jnp.int32, (c, LANES), 1)
    eye = jnp.where(ri == ci, 1.0, 0.0)
    ps = [eye for _ in a_list]
    ms = [-a for a in a_list]
    span = 1
    while span < c:
        mps = [_mm(m[:, :c], jnp.concatenate([m, p], axis=1)) for m, p in zip(ms, ps)]
        ms = [mp[:, :LANES] for mp in mps]
        ps = [p + mp[:, LANES:] for p, mp in zip(ps, mps)]
        span *= 2
    return ps


def _delta_prep_all(qs, ks, vs, g_cols, g_rows, betas, g_lasts, c):
    ri = lax.broadcasted_iota(jnp.int32, (c, LANES), 0)
    ci = lax.broadcasted_iota(jnp.int32, (c, LANES), 1)
    causal = ri >= ci
    strict = ri > ci
    decays = [jnp.where(causal, jnp.exp(jnp.where(causal, gc - gr, 0.0)), 0.0) for gc, gr in zip(g_cols, g_rows)]
    kbs = [k * b for k, b in zip(ks, betas)]
    if c < LANES:
        pad = jnp.zeros((LANES - c, A_DK), F32)
        k_rows = [jnp.concatenate([k, pad], axis=0).astype(BF16) for k in ks]
    else:
        k_rows = [k.astype(BF16) for k in ks]
    a_list = [jnp.where(strict, _mm_nt(kb, kr) * d, 0.0) for kb, kr, d in zip(kbs, k_rows, decays)]
    qks = [(_mm_nt(q, kr) * d)[:, :c].astype(BF16) for q, kr, d in zip(qs, k_rows, decays)]
    t_invs = _tri_inv_all(a_list, c)
    e_gs = [jnp.exp(gc) for gc in g_cols]
    uws = [_mm(t[:, :c], jnp.concatenate([v * b, kb * e], axis=1))
           for t, v, b, kb, e in zip(t_invs, vs, betas, kbs, e_gs)]
    us = [uw[:, :A_DK] for uw in uws]
    wqs = [jnp.concatenate([uw[:, A_DK:], q * e], axis=0).astype(BF16) for uw, q, e in zip(uws, qs, e_gs)]
    k_decs = [(k * jnp.exp(gl - gc)).astype(BF16) for k, gl, gc in zip(ks, g_lasts, g_cols)]
    e_lasts = [jnp.exp(gl) for gl in g_lasts]
    return us, wqs, qks, k_decs, e_lasts


def _delta_body(qkv_ref, gb_ref, s0_ref, o_ref, sout_ref, s_ref, *, nb, tl, c, slot):
    t = pl.program_id(1)
    nt = pl.num_programs(1)

    @pl.when(t == 0)
    def _():
        if s0_ref is None:
            s_ref[...] = jnp.zeros(s_ref.shape, F32)
        else:
            s_ref[...] = s0_ref[...]

    nc = tl // c
    qs, ks, vs, g_cols, g_rows, betas, g_lasts = [], [], [], [], [], [], []
    for ic in range(nc):
        rows = slice(ic * c, (ic + 1) * c)
        for b in range(nb):
            gbc = gb_ref[b, rows, :]
            g_all = _cumsum_rows(gbc)
            if c < LANES:
                g_t = jnp.concatenate([g_all, jnp.zeros((LANES - c, LANES), F32)], axis=0).T
            else:
                g_t = g_all.T
            for h in range(A_HEADS):
                qs.append(qkv_ref[b, rows, h * A_DK:(h + 1) * A_DK])
                ks.append(qkv_ref[b, rows, A_QK + h * A_DK:A_QK + (h + 1) * A_DK])
                vs.append(qkv_ref[b, rows, 2 * A_QK + h * A_DK:2 * A_QK + (h + 1) * A_DK])
                g_cols.append(g_all[:, h:h + 1])
                g_rows.append(g_t[h:h + 1, :])
                betas.append(gbc[:, A_HEADS + h:A_HEADS + h + 1])
                g_lasts.append(g_all[c - 1:c, h:h + 1])
    us, wqs, qks, k_decs, e_lasts = _delta_prep_all(qs, ks, vs, g_cols, g_rows, betas, g_lasts, c)

    nch = nb * A_HEADS
    states = [s_ref[b, h] for b in range(nb) for h in range(A_HEADS)]
    for ic in range(nc):
        sl = slice(ic * nch, (ic + 1) * nch)
        ws_qs = [_mm(wq, s) for wq, s in zip(wqs[sl], states)]
        v_news = [(u - x[:c]).astype(BF16) for u, x in zip(us[sl], ws_qs)]
        outs = [x[c:] + _mm(qk, vn) for x, qk, vn in zip(ws_qs, qks[sl], v_news)]
        states = [s * el + _mm_tn(kd, vn) for s, el, kd, vn in zip(states, e_lasts[sl], k_decs[sl], v_news)]
        for i, o in enumerate(outs):
            b, h = divmod(i, A_HEADS)
            o_ref[b, ic * c:(ic + 1) * c, h * A_DK:(h + 1) * A_DK] = o.astype(o_ref.dtype)
    for i, s in enumerate(states):
        b, h = divmod(i, A_HEADS)
        s_ref[b, h] = s

    @pl.when(t == nt - 1)
    def _():
        _slot_view(sout_ref, slot)[...] = s_ref[...]


def _delta_kernel_s0(qkv_ref, gb_ref, s0_ref, o_ref, sout_ref, s_ref, **kw):
    _delta_body(qkv_ref, gb_ref, s0_ref, o_ref, sout_ref, s_ref, **kw)


def _delta_kernel_zero(qkv_ref, gb_ref, o_ref, sout_ref, s_ref, **kw):
    _delta_body(qkv_ref, gb_ref, None, o_ref, sout_ref, s_ref, **kw)


def _delta(qkv, gb, s0, j, stacked, n_layers, *, nb, tl, c, out_dtype=F32):
    n, l, _ = qkv.shape
    grid = (n // nb, l // tl)
    row_map = lambda i, t: (i, t, 0)
    st_block = (nb, A_HEADS, A_DK, A_DK)
    st_map = lambda i, t: (i, 0, 0, 0)
    first = stacked is None
    kw = dict(nb=nb, tl=tl, c=c, slot=(j, n_layers) if first else None)
    in_specs = [pl.BlockSpec((nb, tl, A_CONV_CH), row_map), pl.BlockSpec((nb, tl, LANES), row_map)]
    args = [qkv, gb]
    if s0 is not None:
        in_specs.append(_stacked_spec(st_block, st_map, j, n_layers, False))
        args.append(s0)
        body = functools.partial(_delta_kernel_s0, **kw)
    else:
        body = functools.partial(_delta_kernel_zero, **kw)
    body, in_specs, args, aliases = _alias_stacked(body, in_specs, args, stacked, 1)
    return pl.pallas_call(
        body, grid=grid, in_specs=in_specs,
        out_specs=[pl.BlockSpec((nb, tl, A_VW), row_map), _stacked_spec(st_block, st_map, j, n_layers, first)],
        out_shape=[jax.ShapeDtypeStruct((n, l, A_VW), out_dtype),
                   jax.ShapeDtypeStruct((n_layers, n, A_HEADS, A_DK, A_DK), F32)],
        scratch_shapes=[pltpu.VMEM(st_block, F32)], input_output_aliases=aliases,
        compiler_params=_params(2), name="delta",
    )(*args)


def _swa_tasks(qs, kxs, vxs, first_min_key, sink_ref):
    nq = qs[0].shape[0]
    nk = 2 * WINDOW
    rows = B_HEADS * nq
    lo = lax.broadcasted_iota(jnp.int32, (nq, LANES), 1) < B_HD
    ri = lax.broadcasted_iota(jnp.int32, (rows, nk), 0) % nq
    ci = lax.broadcasted_iota(jnp.int32, (rows, nk), 1)
    band = (ci >= ri) & (ci <= ri + WINDOW)
    masks = [band & (ci >= first_min_key)] + [band] * (len(qs) - 1)
    head = lax.broadcasted_iota(jnp.int32, (rows, 1), 0) // nq
    sink = jnp.zeros((rows, 1), F32)
    for h in range(B_HEADS):
        sink = jnp.where(head == h, sink_ref[h], sink)

    def stack_heads(q):
        parts = []
        for h in range(B_HEADS):
            j, e = divmod(h, 2)
            g = h // (B_HEADS // B_KV_HEADS)
            src = q[:, j * LANES:(j + 1) * LANES]
            if e != g:
                src = pltpu.roll(src, B_HD, axis=1)
            parts.append(jnp.where(lo, src, 0.0) if g == 0 else jnp.where(lo, 0.0, src))
        return jnp.concatenate(parts, axis=0).astype(BF16)

    lhs = [stack_heads(q) for q in qs]
    ss = [jnp.where(mask, _mm_nt(x, kx), MASK_NEG) for mask, x, kx in zip(masks, lhs, kxs)]
    ms = [jnp.maximum(jnp.max(s, axis=-1, keepdims=True), sink) for s in ss]
    ps = [jnp.where(mask, jnp.exp(s - m), 0.0) for mask, s, m in zip(masks, ss, ms)]
    dens = [jnp.sum(p, axis=-1, keepdims=True) + jnp.exp(sink - m) for p, m in zip(ps, ms)]
    os_ = [_mm(p, vx) / d for p, d, vx in zip(ps, dens, vxs)]
    outs = []
    for o in os_:
        cols = []
        for j in range(B_HEADS // 2):
            a = o[(2 * j) * nq:(2 * j + 1) * nq]
            b = o[(2 * j + 1) * nq:(2 * j + 2) * nq]
            if j < B_HEADS // 4:
                cols.append(jnp.where(lo, a, pltpu.roll(b, B_HD, axis=1)))
            else:
                cols.append(jnp.where(lo, pltpu.roll(a, B_HD, axis=1), b))
        outs.append(jnp.concatenate(cols, axis=1))
    return outs


def _swa_prompt_kernel(sink_ref, q_ref, kvc_ref, kvp_ref, o_ref, *, nblk):
    i = pl.program_id(1)
    kv = [kvp_ref[0].astype(BF16)] + [kvc_ref[0, t * WINDOW:(t + 1) * WINDOW, :].astype(BF16) for t in range(nblk)]
    qs = [q_ref[0, t * WINDOW:(t + 1) * WINDOW, :].astype(F32) for t in range(nblk)]
    kxs = [jnp.concatenate([kv[t][:, :B_KVW], kv[t + 1][:, :B_KVW]], axis=0) for t in range(nblk)]
    vxs = [jnp.concatenate([kv[t][:, B_KVW:], kv[t + 1][:, B_KVW:]], axis=0) for t in range(nblk)]
    outs = _swa_tasks(qs, kxs, vxs, jnp.where(i > 0, 0, WINDOW), sink_ref)
    for t, o in enumerate(outs):
        o_ref[0, t * WINDOW:(t + 1) * WINDOW, :] = o.astype(o_ref.dtype)


def _swa_prompt(qb, kvb, sinks, *, nblk, out_dtype=F32):
    n, l, _ = qb.shape
    tq = nblk * WINDOW
    return pl.pallas_call(
        functools.partial(_swa_prompt_kernel, nblk=nblk), grid=(n, l // tq),
        in_specs=[pl.BlockSpec(memory_space=pltpu.SMEM),
                  pl.BlockSpec((1, tq, B_QW), lambda b, i: (b, i, 0)),
                  pl.BlockSpec((1, tq, 2 * B_KVW), lambda b, i: (b, i, 0)),
                  pl.BlockSpec((1, WINDOW, 2 * B_KVW), lambda b, i: (b, jnp.maximum(i * nblk - 1, 0), 0))],
        out_specs=pl.BlockSpec((1, tq, B_QW), lambda b, i: (b, i, 0)),
        out_shape=jax.ShapeDtypeStruct((n, l, B_QW), out_dtype),
        compiler_params=_params(2), name="swa_prompt",
    )(sinks, qb, kvb, kvb)


def _swa_sample_kernel(sink_ref, q_ref, kv_ref, ck_ref, cv_ref, o_ref, nk_ref, nv_ref, *, nb, ls, slot):
    nk_ref = _slot_view(nk_ref, slot)
    nv_ref = _slot_view(nv_ref, slot)
    pad = jnp.zeros((WINDOW - ls, B_KVW), F32)
    qs, kxs, vxs = [], [], []
    for b in range(nb):
        kv = kv_ref[b]
        k_new = kv[:, :B_KVW]
        v_new = kv[:, B_KVW:]
        ck = ck_ref[b]
        cv = cv_ref[b]
        qs.append(q_ref[b])
        kxs.append(jnp.concatenate([ck, k_new, pad], axis=0).astype(BF16))
        vxs.append(jnp.concatenate([cv, v_new, pad], axis=0).astype(BF16))
        nk_ref[b, 0:WINDOW - ls, :] = ck[ls:, :]
        nk_ref[b, WINDOW - ls:, :] = k_new
        nv_ref[b, 0:WINDOW - ls, :] = cv[ls:, :]
        nv_ref[b, WINDOW - ls:, :] = v_new
    for b, o in enumerate(_swa_tasks(qs, kxs, vxs, 0, sink_ref)):
        o_ref[b] = o


def _swa_sample(qb, kvb, ck, cv, sinks, j, stacked, *, nb):
    n, ls, _ = qb.shape
    row_map = lambda i: (i, 0, 0)
    n_layers = ck.shape[0]
    first = stacked is None
    cache_block = (nb, WINDOW, B_KVW)
    cache_spec = _stacked_spec(cache_block, row_map, j, n_layers, False)
    cache_out = _stacked_spec(cache_block, row_map, j, n_layers, first)
    cache_shape = jax.ShapeDtypeStruct(ck.shape, F32)
    body = functools.partial(_swa_sample_kernel, nb=nb, ls=ls, slot=(j, n_layers) if first else None)
    in_specs = [pl.BlockSpec(memory_space=pltpu.SMEM),
                pl.BlockSpec((nb, ls, B_QW), row_map),
                pl.BlockSpec((nb, ls, 2 * B_KVW), row_map),
                cache_spec, cache_spec]
    args = [sinks, qb, kvb, ck, cv]
    body, in_specs, args, aliases = _alias_stacked(body, in_specs, args, stacked, 1, 2)
    return pl.pallas_call(
        body, grid=(n // nb,), in_specs=in_specs,
        out_specs=[pl.BlockSpec((nb, ls, B_QW), row_map), cache_out, cache_out],
        out_shape=[jax.ShapeDtypeStruct((n, ls, B_QW), F32), cache_shape, cache_shape],
        input_output_aliases=aliases, compiler_params=_params(1), name="swa_sample",
    )(*args)


def _lower_bound(lbraw_ref, j):
    raw = lbraw_ref[...]
    e = jnp.exp(raw - jnp.max(raw, axis=0, keepdims=True))
    sm = e / jnp.sum(e, axis=0, keepdims=True)
    lb = jnp.zeros((1, D_MODEL), F32)
    for i in range(1, j + 1):
        lb = lb + sm[i:i + 1, :]
    return lb


def _odd_proj(x, w_ref, lb):
    d = D_MODEL

    def proj(c0):
        return jnp.dot(x, w_ref[:, c0:c0 + d], preferred_element_type=F32)

    q = _silu(proj(0)) * (C_DK ** -0.5)
    zf = proj(d)
    one_m_lb = 1.0 - lb
    u = one_m_lb * _sigmoid(zf)
    lf = jnp.log(jnp.maximum(lb + u, F_FLOOR))
    k = one_m_lb - u
    v = proj(2 * d)
    g = _silu(proj(3 * d))
    return q, k, v, lf, g


def _odd_in_kernel(x_ref, w_ref, lbraw_ref, q_ref, k_ref, v_ref, lf_ref, g_ref, *, j, ts):
    lb = _lower_bound(lbraw_ref, j)
    for s in range(x_ref.shape[0] // ts):
        rows = slice(s * ts, (s + 1) * ts)
        q, k, v, lf, g = _odd_proj(x_ref[rows, :].astype(BF16), w_ref, lb)
        q_ref[rows, :] = q
        k_ref[rows, :] = k
        v_ref[rows, :] = v
        lf_ref[rows, :] = lf
        g_ref[rows, :] = g


def _odd_in(x2, w, lb_raw, j, *, tr, ts=SUB_ROWS):
    rows = x2.shape[0]
    n_odd = lb_raw.shape[0]
    row_spec = pl.BlockSpec((tr, D_MODEL), lambda i: (i, 0))
    shp = jax.ShapeDtypeStruct((rows, D_MODEL), F32)
    return pl.pallas_call(
        functools.partial(_odd_in_kernel, j=j, ts=min(ts, tr)), grid=(rows // tr,),
        in_specs=[row_spec, _layer_spec(w, j), _const_spec((n_odd, D_MODEL))],
        out_specs=[row_spec] * 5, out_shape=[shp] * 5,
        compiler_params=_params(1), name="odd_in",
    )(x2, w, lb_raw)


def _hgrn_prep_all(qs, ks, vs, lfs, c, state_t):
    ng = c // SUBLANES
    shape3 = (ng, SUBLANES, C_DK)
    sub = lax.broadcasted_iota(jnp.int32, shape3, 1)

    def grouped(x):
        return x.reshape(shape3)

    def flat(x):
        return x.reshape(c, C_DK)

    gs = []
    for lf in lfs:
        x = grouped(lf * LOG2_E)
        s = 1
        while s < SUBLANES:
            x = x + jnp.where(sub >= s, pltpu.roll(x, s, axis=1), 0.0)
            s *= 2
        if ng > 1:
            tot = x[:, SUBLANES - 1:SUBLANES, :]
            run = jnp.zeros((1, 1, C_DK), F32)
            offs = []
            for i in range(ng):
                offs.append(run)
                run = run + tot[i:i + 1]
            x = x + jnp.concatenate(offs, axis=0)
        gs.append(x)
    rbs = [jnp.broadcast_to(g[:, 0:1, :], shape3) for g in gs]

    def block_start(rb, m):
        x = rb.reshape(ng // m, m, SUBLANES, C_DK)[:, 0:1]
        return jnp.broadcast_to(x, (ng // m, m, SUBLANES, C_DK)).reshape(shape3)

    ri = lax.broadcasted_iota(jnp.int32, (c, c), 0)
    ci = lax.broadcasted_iota(jnp.int32, (c, c), 1)
    q3s = [grouped(q) for q in qs]
    k3s = [grouped(k) for k in ks]
    accs = None
    bs = list(gs)
    h = 1
    while h < c:
        if h < SUBLANES:
            bns = [pltpu.roll(b, SUBLANES - h, axis=1) for b in bs]
        else:
            m = h // SUBLANES
            bs = [block_start(rb, m) for rb in rbs]
            bns = [jnp.concatenate([b[m:], g[ng - m:]], axis=0) for b, g in zip(bs, gs)]
        qhs = q3s if h == 1 else [q * jnp.exp2(g - b) for q, g, b in zip(q3s, gs, bs)]
        khs = [k * jnp.exp2(bn - g) for k, bn, g in zip(k3s, bns, gs)]
        lvl = ((ri // h) == (ci // h) + 1) & ((ri // (2 * h)) == (ci // (2 * h)))
        if h == 1 and ng > 1:
            gap = jnp.zeros((LANES - c, C_DK), F32)
            both = [_mm_nt(flat(q), jnp.concatenate([flat(k), gap, flat(kh)], axis=0))
                    for q, k, kh in zip(q3s, k3s, khs)]
            accs = [jnp.where(ri == ci, x[:, :c], jnp.where(lvl, x[:, LANES:], 0.0)) for x in both]
        elif h == 1:
            accs = [jnp.where(ri == ci, _mm_nt(flat(q), flat(k)), jnp.where(lvl, _mm_nt(flat(q), flat(kh)), 0.0))
                    for q, k, kh in zip(q3s, k3s, khs)]
        else:
            accs = [jnp.where(lvl, _mm_nt(flat(qh), flat(kh)), a) for a, qh, kh in zip(accs, qhs, khs)]
        if 2 * h < SUBLANES:
            sel = (sub & h) != 0
            bs = [jnp.where(sel, pltpu.roll(b, h, axis=1), b) for b in bs]
        h *= 2
    if accs is None:
        accs = [jnp.where(ri == ci, _mm_nt(q, k), 0.0) for q, k in zip(qs, ks)]
    g_lasts = [g[ng - 1, SUBLANES - 1:SUBLANES, :] for g in gs]
    intra = [_mm(a, v) for a, v in zip(accs, vs)]
    q_decs = [flat(q * jnp.exp2(g)).astype(BF16) for q, g in zip(q3s, gs)]
    k_decs = [flat(k * jnp.exp2(gl - g)) for k, gl, g in zip(k3s, g_lasts, gs)]
    e_lasts = [jnp.exp2(gl) for gl in g_lasts]
    if state_t:
        kvs = [_mm_tn(v, kd) for v, kd in zip(vs, k_decs)]
    else:
        kvs = [_mm_tn(kd, v) for v, kd in zip(vs, k_decs)]
        e_lasts = [jnp.broadcast_to(el, (C_DK, C_DK)).T for el in e_lasts]
    return intra, q_decs, kvs, e_lasts


def _hgrn_steps(qs, ks, vs, lfs, states, c, state_t):
    intra, q_decs, kvs, e_lasts = _hgrn_prep_all(qs, ks, vs, lfs, c, state_t)
    nch = len(states)
    read_state = _mm_nt if state_t else _mm
    outs = []
    for ic in range(len(qs) // nch):
        sl = slice(ic * nch, (ic + 1) * nch)
        outs += [read_state(qd, st) + x for qd, st, x in zip(q_decs[sl], states, intra[sl])]
        states = [st * el + kv for st, el, kv in zip(states, e_lasts[sl], kvs[sl])]
    return outs, states


def _hgrn_body(q_ref, k_ref, v_ref, lf_ref, s0_ref, o_ref, sout_ref, st_ref, *, nb, hb, tl, c, slot):
    t = pl.program_id(2)
    nt = pl.num_programs(2)
    nc = tl // c
    state_t = s0_ref is None

    @pl.when(t == 0)
    def _():
        if s0_ref is None:
            st_ref[...] = jnp.zeros(st_ref.shape, F32)
        else:
            st_ref[...] = s0_ref[...]

    chains = [(b, h) for b in range(nb) for h in range(hb)]
    where = [(ic, b, h) for ic in range(nc) for b, h in chains]

    def load(ref):
        return [ref[b, ic * c:(ic + 1) * c, h * C_DK:(h + 1) * C_DK] for ic, b, h in where]

    states = [st_ref[b, h] for b, h in chains]
    outs, states = _hgrn_steps(load(q_ref), load(k_ref), load(v_ref), load(lf_ref), states, c, state_t)
    for (ic, b, h), o in zip(where, outs):
        o_ref[b, ic * c:(ic + 1) * c, h * C_DK:(h + 1) * C_DK] = o
    for (b, h), st in zip(chains, states):
        st_ref[b, h] = st

    @pl.when(t == nt - 1)
    def _():
        out = _slot_view(sout_ref, slot)
        for b in range(nb):
            for h in range(hb):
                out[b, h] = st_ref[b, h].T if state_t else st_ref[b, h]


def _hgrn_kernel_s0(q_ref, k_ref, v_ref, lf_ref, s0_ref, o_ref, sout_ref, st_ref, **kw):
    _hgrn_body(q_ref, k_ref, v_ref, lf_ref, s0_ref, o_ref, sout_ref, st_ref, **kw)


def _hgrn_kernel_zero(q_ref, k_ref, v_ref, lf_ref, o_ref, sout_ref, st_ref, **kw):
    _hgrn_body(q_ref, k_ref, v_ref, lf_ref, None, o_ref, sout_ref, st_ref, **kw)


def _hgrn(q, k, v, lf, s0, j, stacked, n_layers, *, nb, hb, tl, c):
    n, l, _ = q.shape
    grid = (n // nb, C_HEADS // hb, l // tl)
    row_spec = pl.BlockSpec((nb, tl, hb * C_DK), lambda i, g, t: (i, t, g))
    st_block = (nb, hb, C_DK, C_DK)
    st_map = lambda i, g, t: (i, g, 0, 0)
    first = stacked is None
    st_spec = _stacked_spec(st_block, st_map, j, n_layers, first)
    in_specs = [row_spec] * 4
    args = [q, k, v, lf]
    kw = dict(nb=nb, hb=hb, tl=tl, c=c, slot=(j, n_layers) if first else None)
    if s0 is not None:
        in_specs = in_specs + [_stacked_spec(st_block, st_map, j, n_layers, False)]
        args.append(s0)
        body = functools.partial(_hgrn_kernel_s0, **kw)
    else:
        body = functools.partial(_hgrn_kernel_zero, **kw)
    body, in_specs, args, aliases = _alias_stacked(body, in_specs, args, stacked, 1)
    return pl.pallas_call(
        body, grid=grid, in_specs=in_specs, out_specs=[row_spec, st_spec],
        out_shape=[jax.ShapeDtypeStruct((n, l, D_MODEL), F32),
                   jax.ShapeDtypeStruct((n_layers, n, C_HEADS, C_DK, C_DK), F32)],
        scratch_shapes=[pltpu.VMEM(st_block, F32)], input_output_aliases=aliases,
        compiler_params=_params(3), name="hgrn",
    )(*args)


def _odd_fused_kernel(x_ref, w_ref, lbraw_ref, o_ref, g_ref, sout_ref, st_ref, *, j, tl, c, ts, slot):
    t = pl.program_id(1)
    nt = pl.num_programs(1)

    @pl.when(t == 0)
    def _():
        st_ref[...] = jnp.zeros(st_ref.shape, F32)

    lb = _lower_bound(lbraw_ref, j)
    heads = range(C_HEADS)

    def project(s):
        rows = slice(s * ts, (s + 1) * ts)
        q, k, v, lf, g = _odd_proj(x_ref[0, rows, :].astype(BF16), w_ref, lb)
        g_ref[0, rows, :] = g.astype(g_ref.dtype)
        return q, k, v, lf

    def mix(s, qkvl, states):
        where = [(ic, h) for ic in range(ts // c) for h in heads]
        qs, ks, vs, lfs = ([a[ic * c:(ic + 1) * c, h * C_DK:(h + 1) * C_DK] for ic, h in where] for a in qkvl)
        outs, states = _hgrn_steps(qs, ks, vs, lfs, states, c, True)
        for (ic, h), o in zip(where, outs):
            o_ref[0, s * ts + ic * c:s * ts + (ic + 1) * c, h * C_DK:(h + 1) * C_DK] = o.astype(o_ref.dtype)
        return states

    states = [st_ref[h] for h in heads]
    n_sub = tl // ts
    pending = project(0)
    for s in range(n_sub):
        following = project(s + 1) if s + 1 < n_sub else None
        states = mix(s, pending, states)
        pending = following
    for h in heads:
        st_ref[h] = states[h]

    @pl.when(t == nt - 1)
    def _():
        out = _slot_view(sout_ref, slot)
        for h in heads:
            out[0, h] = st_ref[h].T


def _odd_fused(x, w, lb_raw, j, stacked, n_layers, *, tl, c, ts=SUB_ROWS, out_dtype=F32):
    n, l, _ = x.shape
    row_spec = pl.BlockSpec((1, tl, D_MODEL), lambda i, t: (i, t, 0))
    st_block = (1, C_HEADS, C_DK, C_DK)
    first = stacked is None
    body = functools.partial(_odd_fused_kernel, j=j, tl=tl, c=c, ts=ts, slot=(j, n_layers) if first else None)
    in_specs = [row_spec, _layer_spec(w, j), _const_spec(lb_raw.shape)]
    body, in_specs, args, aliases = _alias_stacked(body, in_specs, [x, w, lb_raw], stacked, 2)
    shp = jax.ShapeDtypeStruct((n, l, D_MODEL), out_dtype)
    return pl.pallas_call(
        body, grid=(n, l // tl), in_specs=in_specs,
        out_specs=[row_spec, row_spec, _stacked_spec(st_block, lambda i, t: (i, 0, 0, 0), j, n_layers, first)],
        out_shape=[shp, shp, jax.ShapeDtypeStruct((n_layers, n, C_HEADS, C_DK, C_DK), F32)],
        scratch_shapes=[pltpu.VMEM((C_HEADS, C_DK, C_DK), F32)], input_output_aliases=aliases,
        compiler_params=_params(2), name="odd_fused",
    )(*args)


def _group_rms(o, g_row, heads):
    outs = []
    for h in range(heads):
        c = o[:, h * LANES:(h + 1) * LANES]
        inv = lax.rsqrt(jnp.mean(c * c, axis=-1, keepdims=True) + NORM_EPS)
        outs.append(c * inv * g_row)
    return jnp.concatenate(outs, axis=1)


def _post_tail(x_ref, ys, p_ref, lng_ref, lnb_ref, wproj_ref, wgate_ref, out_ref, tiles):
    hns = []
    for rows, y in zip(tiles, ys):
        pre = DEEPNORM_ALPHA * x_ref[rows, :] + y
        mu = jnp.mean(pre, axis=-1, keepdims=True)
        xc = pre - mu
        var = jnp.mean(xc * xc, axis=-1, keepdims=True)
        hns.append(xc * lax.rsqrt(var + NORM_EPS) * lng_ref[...] + lnb_ref[...])
    embs = [jnp.dot(p_ref[rows, :].astype(BF16), wproj_ref[:, 0:D_MODEL], preferred_element_type=F32)
            for rows in tiles]
    gates = [_sigmoid(jnp.dot(hn.astype(BF16), wgate_ref[:, 0:D_MODEL], preferred_element_type=F32)) for hn in hns]
    for rows, hn, gate, emb in zip(tiles, hns, gates, embs):
        out_ref[rows, :] = hn + gate * emb


def _sub_tiles(n_rows):
    ts = min(n_rows, POST_SUB_ROWS)
    return [slice(r0, r0 + ts) for r0 in range(0, n_rows, ts)]


def _post_even_kernel(x_ref, oa_ref, ga_ref, ob_ref, gbt_ref, p_ref, wout_ref, ng_ref, lng_ref, lnb_ref,
                      wproj_ref, wgate_ref, out_ref):
    tiles = _sub_tiles(x_ref.shape[0])
    a_s = [(_group_rms(oa_ref[rows, :].astype(F32), ng_ref[...], A_HEADS) * ga_ref[rows, :].astype(F32)).astype(BF16)
           for rows in tiles]
    b_s = [(ob_ref[rows, :].astype(F32) * gbt_ref[rows, :].astype(F32)).astype(BF16) for rows in tiles]
    ys = [jnp.dot(a, wout_ref[0:A_VW, 0:D_MODEL], preferred_element_type=F32)
          + jnp.dot(b, wout_ref[A_VW:, 0:D_MODEL], preferred_element_type=F32) for a, b in zip(a_s, b_s)]
    _post_tail(x_ref, ys, p_ref, lng_ref, lnb_ref, wproj_ref, wgate_ref, out_ref, tiles)


def _post_odd_kernel(x_ref, o_ref, g_ref, p_ref, wout_ref, ng_ref, lng_ref, lnb_ref,
                     wproj_ref, wgate_ref, out_ref):
    tiles = _sub_tiles(x_ref.shape[0])
    a_s = [(_group_rms(o_ref[rows, :].astype(F32), ng_ref[...], C_HEADS) * g_ref[rows, :].astype(F32)).astype(BF16)
           for rows in tiles]
    ys = [jnp.dot(a, wout_ref[:, 0:D_MODEL], preferred_element_type=F32) for a in a_s]
    _post_tail(x_ref, ys, p_ref, lng_ref, lnb_ref, wproj_ref, wgate_ref, out_ref, tiles)


def _post(x2, acts, p3, layer, wout, j, ng, lng, lnb, wproj, wgate, *, even, tr):
    rows = x2.shape[0]
    full = pl.BlockSpec((tr, D_MODEL), lambda i: (i, 0))
    half = pl.BlockSpec((tr, A_VW), lambda i: (i, 0))
    act_specs = [half] * 4 if even else [full] * 2
    in_specs = ([full] + act_specs + [pl.BlockSpec((None, tr, PLE_DIM), lambda i: (layer, i, 0)),
                                      _layer_spec(wout, j), _const_spec((1, LANES)),
                                      _const_spec((1, D_MODEL)), _const_spec((1, D_MODEL)),
                                      _layer_spec(wproj, layer), _layer_spec(wgate, layer)])
    return pl.pallas_call(
        _post_even_kernel if even else _post_odd_kernel, grid=(rows // tr,),
        in_specs=in_specs, out_specs=full, out_shape=jax.ShapeDtypeStruct((rows, D_MODEL), F32),
        compiler_params=_params(1), name="post_even" if even else "post_odd",
    )(x2, *acts, p3, wout, ng, lng, lnb, wproj, wgate)


def _bf16_padded(w):
    return jnp.pad(w.astype(BF16), ((0, 0), (0, 0), (0, LANES)))


def _pack_even_w(w):
    w = w.astype(BF16)
    qkv = w[..., 0:1536]
    ga = w[..., 1536:2048]
    ab = w[..., 2048:2056]
    qb = w[..., 2056:2568]
    kb = w[..., 2568:2696]
    vb = w[..., 2696:2824]
    gbt = w[..., 2824:3336]
    ab = jnp.pad(ab, ((0, 0), (0, 0), (0, LANES - 2 * A_HEADS)))
    return jnp.concatenate([qkv, ga, qb, kb, vb, gbt, ab], axis=-1)


def _rot_tables(pos, reps):
    inv = ROPE_THETA ** (-jnp.arange(0, ROT_DIM, 2, dtype=F32) / ROT_DIM)
    ang = pos.astype(F32)[:, None] * inv[None, :]
    cos = jnp.cos(ang)
    sin = jnp.sin(ang)
    n = pos.shape[0]
    half = ROT_DIM // 2
    cos64 = jnp.concatenate([cos, cos, jnp.ones((n, B_HD - ROT_DIM), F32)], axis=1)
    up64 = jnp.concatenate([-sin, jnp.zeros((n, B_HD - half), F32)], axis=1)
    dn64 = jnp.concatenate([jnp.zeros((n, half), F32), sin, jnp.zeros((n, B_HD - ROT_DIM), F32)], axis=1)
    tabs = [jnp.tile(t, (reps, LANES // B_HD)) for t in (cos64, up64, dn64)]
    return jnp.stack(tabs)


def kernel(x_prompt, x_sample, state_conv_a, state_delta_a, cache_win_k, cache_win_v, state_hgrn_c,
           p_prompt, p_sample, w_in_even, conv_w_a, a_log, dt_bias, norm_a, sinks_b, w_out_even,
           w_in_odd, lb_raw, norm_c, w_out_odd, ln_g, ln_b, w_ple_proj, w_ple_gate):
    n_p, l_p, _ = x_prompt.shape
    n_s, l_s, _ = x_sample.shape
    nb_s = SAMPLE_NB_PROJ
    rot_p = _rot_tables(jnp.arange(l_p), 1)
    rot_s = _rot_tables(PAST_LEN + jnp.arange(l_s), nb_s)
    n_even, n_odd = w_in_even.shape[0], w_in_odd.shape[0]
    conv_p, conv_s, wk_p, wv_p = [], [], [], []
    delta_p = delta_s = hg_p = hg_s = win_s = None
    hp, hs = x_prompt, x_sample
    pp3 = p_prompt.reshape(DEPTH, n_p * l_p, PLE_DIM)
    ps3 = p_sample.reshape(DEPTH, n_s * l_s, PLE_DIM)
    ck_all = cache_win_k.reshape(-1, n_s, WINDOW, B_KVW)
    cv_all = cache_win_v.reshape(-1, n_s, WINDOW, B_KVW)
    w_even = _pack_even_w(w_in_even)
    w_odd = _bf16_padded(w_in_odd)
    wout_even = _bf16_padded(w_out_even)
    wout_odd = _bf16_padded(w_out_odd)
    wproj = _bf16_padded(w_ple_proj)
    wgate = _bf16_padded(w_ple_gate)
    for layer in range(DEPTH):
        j = layer // 2
        lng = ln_g[layer].reshape(1, D_MODEL)
        lnb = ln_b[layer].reshape(1, D_MODEL)
        hp2 = hp.reshape(n_p * l_p, D_MODEL)
        hs2 = hs.reshape(n_s * l_s, D_MODEL)
        if layer % 2 == 0:
            w = w_even
            misc = jnp.zeros((8, LANES), F32)
            misc = misc.at[0, :A_HEADS].set(a_log[j]).at[1, :A_HEADS].set(dt_bias[j])
            cw = conv_w_a[j]
            sinks = sinks_b[j]
            wout = wout_even
            ng = norm_a[j].reshape(1, LANES)
            qkv, gb, ga, qb, kvb, gbt, c1 = _even_in(hp, w, cw, misc, rot_p, None, j, nb=1, tl=PROMPT_TILE,
                                                     gate_dtype=ACT_DTYPE)
            oa, delta_p = _delta(qkv, gb, None, j, delta_p, n_even, nb=1, tl=PROMPT_TILE, c=A_CHUNK,
                                 out_dtype=ACT_DTYPE)
            ob = _swa_prompt(qb, kvb, sinks, nblk=SWA_BLOCKS, out_dtype=ACT_DTYPE)
            acts = [t.reshape(n_p * l_p, A_VW) for t in (oa, ga, ob, gbt)]
            hp = _post(hp2, acts, pp3, layer, wout, j, ng, lng, lnb, wproj, wgate, even=True, tr=POST_TILE)
            hp = hp.reshape(n_p, l_p, D_MODEL)
            k1 = kvb[:, l_p - WINDOW:, :B_KVW].reshape(n_p, WINDOW, B_KV_HEADS, B_HD)
            v1 = kvb[:, l_p - WINDOW:, B_KVW:].reshape(n_p, WINDOW, B_KV_HEADS, B_HD)
            qkv, gb, ga, qb, kvb, gbt, c2 = _even_in(hs, w, cw, misc, rot_s, state_conv_a, j, nb=nb_s, tl=l_s)
            oa, delta_s = _delta(qkv, gb, state_delta_a, j, delta_s, n_even, nb=SAMPLE_NB_DELTA, tl=l_s, c=l_s)
            ob, *win_s = _swa_sample(qb, kvb, ck_all, cv_all, sinks, j, win_s, nb=SAMPLE_NB_SWA)
            acts = [t.reshape(n_s * l_s, A_VW) for t in (oa, ga, ob, gbt)]
            hs = _post(hs2, acts, ps3, layer, wout, j, ng, lng, lnb, wproj, wgate, even=True, tr=POST_TILE)
            hs = hs.reshape(n_s, l_s, D_MODEL)
            conv_p.append(c1); conv_s.append(c2)
            wk_p.append(k1); wv_p.append(v1)
        else:
            w = w_odd
            wout = wout_odd
            ng = norm_c[j].reshape(1, LANES)
            o, g, hg_p = _odd_fused(hp, w, lb_raw, j, hg_p, n_odd, tl=PROMPT_TILE, c=C_CHUNK, out_dtype=ACT_DTYPE)
            acts = [t.reshape(n_p * l_p, D_MODEL) for t in (o, g)]
            hp = _post(hp2, acts, pp3, layer, wout, j, ng, lng, lnb, wproj, wgate, even=False, tr=POST_TILE)
            hp = hp.reshape(n_p, l_p, D_MODEL)
            q, k, v, lf, g = _odd_in(hs2, w, lb_raw, j, tr=SAMPLE_TILE_ODD)
            shp = (n_s, l_s, D_MODEL)
            o, hg_s = _hgrn(q.reshape(shp), k.reshape(shp), v.reshape(shp), lf.reshape(shp), state_hgrn_c, j,
                            hg_s, n_odd, nb=SAMPLE_NB_HGRN, hb=C_HEADS, tl=l_s, c=l_s)
            hs = _post(hs2, [o.reshape(n_s * l_s, D_MODEL), g], ps3, layer, wout, j, ng, lng, lnb, wproj, wgate,
                       even=False, tr=POST_TILE)
            hs = hs.reshape(n_s, l_s, D_MODEL)
    wk_s, wv_s = (t.reshape(n_even, n_s, WINDOW, B_KV_HEADS, B_HD) for t in win_s)
    return (hp, hs, jnp.stack(conv_p), jnp.stack(conv_s), delta_p, delta_s,
            jnp.stack(wk_p), wk_s, jnp.stack(wv_p), wv_s, hg_p, hg_s)
```
